```python
import math
import jax, jax.numpy as jnp
from jax import lax
import numpy as np

D_MODEL = 1024
BATCH = 8
SEQ = 4096
DEPTH = 2

N_A = (DEPTH + 1) // 2
N_B = DEPTH - N_A

GLA_HEADS = 4
QK_DIM = D_MODEL // 2
V_DIM = D_MODEL
DK = QK_DIM // GLA_HEADS
DV = V_DIM // GLA_HEADS
GATE_RANK = 16
GATE_NORMALIZER = 16.0
CHUNK = 64
A_IN_DIM = 2 * QK_DIM + V_DIM + GATE_RANK + V_DIM

SB_HEADS = 16
SB_HEAD_DIM = D_MODEL // SB_HEADS
SB_DIM = SB_HEADS * SB_HEAD_DIM
Q_BLOCK = 128

N_GROUPS = 4
EXPERTS_PER_GROUP = 4
N_EXPERTS = N_GROUPS * EXPERTS_PER_GROUP
TOP_K = 2
D_EXPERT = D_MODEL // 2

RMS_EPS = 1e-6

kernel_name = "yoco_gla_stickbreaking_hmoe"


def rms_norm(x, g):
    xf = x.astype(jnp.float32)
    y = xf * lax.rsqrt(jnp.mean(xf * xf, axis=-1, keepdims=True) + RMS_EPS)
    return (y * g.astype(jnp.float32)).astype(x.dtype)


def gla_mixer(u, w_in, w_gk2, b_gk2, o_norm, w_out):
    B, S, _ = u.shape
    n = S // CHUNK
    proj = u @ w_in
    q, k, v, gk_lr, g = jnp.split(
        proj, [QK_DIM, 2 * QK_DIM, 2 * QK_DIM + V_DIM, 2 * QK_DIM + V_DIM + GATE_RANK], axis=-1)
    gk = jax.nn.log_sigmoid((gk_lr @ w_gk2 + b_gk2).astype(jnp.float32)) / GATE_NORMALIZER

    def to_chunks(t, dh):
        return t.reshape(B, n, CHUNK, GLA_HEADS, dh).transpose(0, 3, 1, 2, 4).astype(jnp.float32)

    q = to_chunks(q, DK) * (DK ** -0.5)
    k = to_chunks(k, DK)
    v = to_chunks(v, DV)
    gk = to_chunks(gk, DK)
    b = jnp.cumsum(gk, axis=3)
    q_dec = q * jnp.exp(b)
    k_inv = k * jnp.exp(-b)

    causal = jnp.tril(jnp.ones((CHUNK, CHUNK), dtype=bool))
    att = jnp.where(causal, jnp.einsum('bhnid,bhnjd->bhnij', q_dec, k_inv), 0.0)
    o_intra = jnp.einsum('bhnij,bhnje->bhnie', att, v)

    b_last = b[:, :, :, -1, :]
    k_to_end = k * jnp.exp(b_last[:, :, :, None, :] - b)
    upd = jnp.einsum('bhncd,bhnce->bhnde', k_to_end, v)

    def step(state, inp):
        dec, u_n = inp
        return dec[..., None] * state + u_n, state

    init = jnp.zeros((B, GLA_HEADS, DK, DV), jnp.float32)
    _, prev = lax.scan(step, init, (jnp.moveaxis(jnp.exp(b_last), 2, 0), jnp.moveaxis(upd, 2, 0)))
    prev = jnp.moveaxis(prev, 0, 2)
    o = o_intra + jnp.einsum('bhncd,bhnde->bhnce', q_dec, prev)

    o = rms_norm(o, o_norm)
    o = o.transpose(0, 2, 3, 1, 4).reshape(B, S, V_DIM)
    o = o * jax.nn.silu(g.astype(jnp.float32))
    return o.astype(u.dtype) @ w_out


def stick_breaking_attention(q, k, v):
    S = q.shape[2]
    scale = 1.0 / math.sqrt(SB_HEAD_DIM)
    outs = []
    for blk in range(S // Q_BLOCK):
        t0 = blk * Q_BLOCK
        L = t0 + Q_BLOCK
        qb = q[:, :, t0:L].astype(jnp.float32)
        kb = k[:, :, :L].astype(jnp.float32)
        vb = v[:, :, :L].astype(jnp.float32)
        z = jnp.einsum('bhtd,bhsd->bhts', qb, kb) * scale
        tpos = t0 + jnp.arange(Q_BLOCK)
        spos = jnp.arange(L)
        mask = spos[None, :] < tpos[:, None]
        log_om = jnp.where(mask, jax.nn.log_sigmoid(-z), 0.0)
        log_beta = jax.nn.log_sigmoid(z)
        suffix = lax.cumsum(log_om, axis=3, reverse=True) - log_om
        a = jnp.where(mask, jnp.exp(log_beta + suffix), 0.0)
        outs.append(jnp.einsum('bhts,bhsd->bhtd', a, vb))
    return jnp.concatenate(outs, axis=2)


def stick_breaking_mixer(u, k, v, w_q, w_out):
    B, S, _ = u.shape
    q = (u @ w_q).reshape(B, S, SB_HEADS, SB_HEAD_DIM).transpose(0, 2, 1, 3)
    o = stick_breaking_attention(q, k, v)
    o = o.transpose(0, 2, 1, 3).reshape(B, S, SB_DIM).astype(u.dtype)
    return o @ w_out


def hier_moe(t, w_group, b_group, w_expert, b_expert, w_gate, w_up, w_down):
    T = t.shape[0]
    g_logits = (t @ w_group + b_group).astype(jnp.float32)
    g_prob = jax.nn.softmax(g_logits, axis=-1)
    g_w, g_idx = lax.top_k(g_prob, 1)
    e_logits = (t @ w_expert + b_expert).astype(jnp.float32).reshape(T, N_GROUPS, EXPERTS_PER_GROUP)
    e_sel = jnp.take_along_axis(e_logits, g_idx[:, :, None], axis=1)[:, 0]
    e_prob = jax.nn.softmax(e_sel, axis=-1)
    e_w, e_idx = lax.top_k(e_prob, TOP_K)
    e_w = e_w / jnp.sum(e_w, axis=-1, keepdims=True)
    weights = g_w * e_w
    global_idx = g_idx * EXPERTS_PER_GROUP + e_idx
    combine = jnp.einsum('tk,tke->te', weights, jax.nn.one_hot(global_idx, N_EXPERTS, dtype=jnp.float32))
    y = jnp.zeros(t.shape, jnp.float32)
    for e in range(N_EXPERTS):
        hdn = jax.nn.silu(t @ w_gate[e]) * (t @ w_up[e])
        y = y + combine[:, e:e + 1] * (hdn @ w_down[e]).astype(jnp.float32)
    return y.astype(t.dtype)


def setup_inputs(seed: int = 0) -> dict:
    key = jax.random.key(seed)
    ks = iter(jax.random.split(key, 32))
    f32 = jnp.float32

    def nrm(shape, scale):
        return jax.random.normal(next(ks), shape, f32) * scale

    def gain(shape):
        return 1.0 + 0.01 * jax.random.normal(next(ks), shape, f32)

    D = D_MODEL
    return {
        "x": jax.random.normal(next(ks), (BATCH, SEQ, D), f32),
        "a_norm": gain((N_A, D)),
        "a_w_in": nrm((N_A, D, A_IN_DIM), D ** -0.5),
        "a_w_gk2": nrm((N_A, GATE_RANK, QK_DIM), GATE_RANK ** -0.5),
        "a_b_gk2": nrm((N_A, QK_DIM), 0.01),
        "a_o_norm": gain((N_A, DV)),
        "a_w_out": nrm((N_A, V_DIM, D), V_DIM ** -0.5),
        "kv_norm": gain((D,)),
        "w_kv": nrm((D, 2 * SB_DIM), D ** -0.5),
        "b_norm": gain((N_B, D)),
        "b_w_q": nrm((N_B, D, SB_DIM), D ** -0.5),
        "b_w_out": nrm((N_B, SB_DIM, D), SB_DIM ** -0.5),
        "m_norm": gain((DEPTH, D)),
        "m_w_group": nrm((DEPTH, D, N_GROUPS), D ** -0.5),
        "m_b_group": nrm((DEPTH, N_GROUPS), 0.01),
        "m_w_expert": nrm((DEPTH, D, N_EXPERTS), D ** -0.5),
        "m_b_expert": nrm((DEPTH, N_EXPERTS), 0.01),
        "m_w_gate": nrm((DEPTH, N_EXPERTS, D, D_EXPERT), D ** -0.5),
        "m_w_up": nrm((DEPTH, N_EXPERTS, D, D_EXPERT), D ** -0.5),
        "m_w_down": nrm((DEPTH, N_EXPERTS, D_EXPERT, D), D_EXPERT ** -0.5),
        "final_norm": gain((D,)),
    }


def reference(x, a_norm, a_w_in, a_w_gk2, a_b_gk2, a_o_norm, a_w_out, kv_norm, w_kv,
              b_norm, b_w_q, b_w_out, m_norm, m_w_group, m_b_group, m_w_expert, m_b_expert,
              m_w_gate, m_w_up, m_w_down, final_norm):
    B, S, D = x.shape
    h = x
    k_shared = None
    v_shared = None
    for layer in range(DEPTH):
        if layer < N_A:
            h = h + gla_mixer(rms_norm(h, a_norm[layer]), a_w_in[layer], a_w_gk2[layer],
                              a_b_gk2[layer], a_o_norm[layer], a_w_out[layer])
        else:
            if layer == N_A:
                kv = rms_norm(h, kv_norm) @ w_kv
                k_s, v_s = jnp.split(kv, 2, axis=-1)
                k_shared = k_s.reshape(B, S, SB_HEADS, SB_HEAD_DIM).transpose(0, 2, 1, 3)
                v_shared = v_s.reshape(B, S, SB_HEADS, SB_HEAD_DIM).transpose(0, 2, 1, 3)
            j = layer - N_A
            h = h + stick_breaking_mixer(rms_norm(h, b_norm[j]), k_shared, v_shared, b_w_q[j], b_w_out[j])
        t = rms_norm(h, m_norm[layer]).reshape(B * S, D)
        h = h + hier_moe(t, m_w_group[layer], m_b_group[layer], m_w_expert[layer], m_b_expert[layer],
                         m_w_gate[layer], m_w_up[layer], m_w_down[layer]).reshape(B, S, D)
    return rms_norm(h, final_norm)
```

```python
import functools
import math

import jax
import jax.numpy as jnp
from jax import lax
from jax.experimental import pallas as pl
from jax.experimental.pallas import tpu as pltpu

RMS_EPS = 1e-6

GLA_HEADS = 4
GLA_CHUNK = 64
CHUNK_SHIFT = GLA_CHUNK.bit_length() - 1
GATE_NORMALIZER = 16.0
SB_HEADS = 16
N_GROUPS = 4
EXPERTS_PER_GROUP = 4
N_EXPERTS = N_GROUPS * EXPERTS_PER_GROUP

LANES = 128
VMEM_LIMIT = 56 * 1024 * 1024

GLA_ROWS = 256
PROJ_ROWS = 512
ATT_TQ = 256
ATT_TK = 256
MOE_ROWS = 256
MOE_TM = 512

BF16 = jnp.bfloat16
F32 = jnp.float32


def _dot(a, b):
    return jnp.dot(a, b, preferred_element_type=F32)


def _dot_nt(a, b):
    return lax.dot_general(a, b, (((1,), (1,)), ((), ())), preferred_element_type=F32)


def _split_dot(m01, x, left):
    hi = x.astype(BF16)
    lo = (x - hi.astype(F32)).astype(BF16)
    if left:
        return _dot(m01, hi) + _dot(m01, lo)
    return _dot(hi, m01) + _dot(lo, m01)


def _rms_scale(x):
    return lax.rsqrt(jnp.mean(x * x, axis=-1, keepdims=True) + RMS_EPS)


def _log_sigmoid(x):
    return jnp.minimum(x, 0.0) - jnp.log(1.0 + jnp.exp(-jnp.abs(x)))


def _softplus(x):
    return jnp.maximum(x, 0.0) + jnp.log(1.0 + jnp.exp(-jnp.abs(x)))


def _sigmoid(x):
    return 1.0 / (1.0 + jnp.exp(-x))


def _gla_kernel(h_ref, nrm_ref, wrow_ref, wkt_ref, wlrt_ref, wgk2_ref, bgk2_ref, wgk2t_ref, bgk2c_ref,
                onorm_ref, wout_ref, out_ref, state_ref, *, dk, dv, qk_dim, v_dim, log_scale):
    R = h_ref.shape[0]
    n_chunks = R // GLA_CHUNK

    @pl.when(pl.program_id(1) == 0)
    def _():
        state_ref[...] = jnp.zeros_like(state_ref)

    h = h_ref[...]
    u = (h * _rms_scale(h) * nrm_ref[...]).astype(BF16)

    proj = _dot(u, wrow_ref[...])
    q = proj[:, :qk_dim]
    v = proj[:, qk_dim:qk_dim + v_dim].astype(BF16)
    g = proj[:, qk_dim + v_dim:qk_dim + 2 * v_dim]
    lr = proj[:, qk_dim + 2 * v_dim:].astype(BF16)
    kt = _dot_nt(wkt_ref[...], u)
    lrt = _dot_nt(wlrt_ref[...], u).astype(BF16)

    gk = _log_sigmoid(_dot(lr, wgk2_ref[...]) + bgk2_ref[...]) * (1.0 / GATE_NORMALIZER)
    gkt = _log_sigmoid(_dot(wgk2t_ref[...], lrt) + bgk2c_ref[...]) * (1.0 / GATE_NORMALIZER)

    row = lax.broadcasted_iota(jnp.int32, (R, R), 0)
    col = lax.broadcasted_iota(jnp.int32, (R, R), 1)
    same_chunk = (row >> CHUNK_SHIFT) == (col >> CHUNK_SHIFT)
    causal = same_chunk & (col <= row)
    lbd = jnp.where(causal, 1.0, 0.0).astype(BF16)
    ubd = jnp.where(same_chunk & (row <= col), 1.0, 0.0).astype(BF16)
    after = jnp.where(same_chunk & (row > col), 1.0, 0.0).astype(BF16)

    b = _split_dot(lbd, gk, left=True)
    bt = _split_dot(ubd, gkt, left=False)
    tail_t = _split_dot(after, gkt, left=False)

    q_dec = (q * jnp.exp(b + log_scale)).astype(BF16)
    k_inv_t = (kt * jnp.exp(-bt)).astype(BF16)
    k_end_t = (kt * jnp.exp(tail_t)).astype(BF16)
    chunk_decay_t = jnp.exp(bt + tail_t)

    lane_chunk = lax.broadcasted_iota(jnp.int32, (dk, R), 1) >> CHUNK_SHIFT

    acc = h
    for hd in range(GLA_HEADS):
        ks = slice(hd * dk, (hd + 1) * dk)
        vs = slice(hd * dv, (hd + 1) * dv)
        qd_h = q_dec[:, ks]
        v_h = v[:, vs]
        att = _dot(qd_h, k_inv_t[ks, :])
        att = jnp.where(causal, att, 0.0).astype(BF16)
        o_h = _dot(att, v_h)
        kend_h = k_end_t[ks, :]
        state = state_ref[hd]
        inter = []
        for c in range(n_chunks):
            rows = slice(c * GLA_CHUNK, (c + 1) * GLA_CHUNK)
            inter.append(_dot(qd_h[rows], state.astype(BF16)))
            kend_c = jnp.where(lane_chunk == c, kend_h, jnp.zeros_like(kend_h))
            decay = chunk_decay_t[ks, c * GLA_CHUNK:c * GLA_CHUNK + 1]
            state = decay * state + _dot(kend_c, v_h)
        state_ref[hd] = state
        o_h = o_h + jnp.concatenate(inter, axis=0)
        o_h = o_h * _rms_scale(o_h) * onorm_ref[...]
        g_h = g[:, vs]
        o_h = o_h * (g_h * _sigmoid(g_h))
        acc = acc + _dot(o_h.astype(BF16), wout_ref[vs, :])
    out_ref[...] = acc


def _gla_layer(h, batch, seq, a_norm, w_in, w_gk2, b_gk2, o_norm, w_out):
    T, D = h.shape
    rank, qk_dim = w_gk2.shape
    v_dim = w_out.shape[0]
    dk = qk_dim // GLA_HEADS
    dv = v_dim // GLA_HEADS
    R = GLA_ROWS
    assert seq % R == 0 and R % GLA_CHUNK == 0 and rank <= LANES
    assert w_in.shape[1] == 2 * qk_dim + 2 * v_dim + rank
    n_s = seq // R

    o_q, o_k, o_v, o_lr, o_g = 0, qk_dim, 2 * qk_dim, 2 * qk_dim + v_dim, 2 * qk_dim + v_dim + rank
    w_lr = jnp.pad(w_in[:, o_lr:o_lr + rank], ((0, 0), (0, LANES - rank)))
    w_row = jnp.concatenate([w_in[:, o_q:o_k], w_in[:, o_v:o_lr], w_in[:, o_g:], w_lr], axis=1).astype(BF16)
    w_kt = w_in[:, o_k:o_v].T.astype(BF16)
    w_lrt = w_lr.T.astype(BF16)
    w_gk2p = jnp.pad(w_gk2, ((0, LANES - rank), (0, 0))).astype(BF16)
    w_gk2t = w_gk2p.T
    const = lambda b, s: (0, 0)
    kern = functools.partial(_gla_kernel, dk=dk, dv=dv, qk_dim=qk_dim, v_dim=v_dim,
                             log_scale=math.log(dk ** -0.5))
    return pl.pallas_call(
        kern,
        grid=(batch, n_s),
        in_specs=[
            pl.BlockSpec((R, D), lambda b, s: (b * n_s + s, 0)),
            pl.BlockSpec((1, D), const),
            pl.BlockSpec(w_row.shape, const),
            pl.BlockSpec(w_kt.shape, const),
            pl.BlockSpec(w_lrt.shape, const),
            pl.BlockSpec(w_gk2p.shape, const),
            pl.BlockSpec((1, qk_dim), const),
            pl.BlockSpec(w_gk2t.shape, const),
            pl.BlockSpec((qk_dim, 1), const),
            pl.BlockSpec((1, dv), const),
            pl.BlockSpec((v_dim, D), const),
        ],
        out_specs=pl.BlockSpec((R, D), lambda b, s: (b * n_s + s, 0)),
        out_shape=jax.ShapeDtypeStruct((T, D), F32),
        scratch_shapes=[pltpu.VMEM((GLA_HEADS, dk, dv), F32)],
        compiler_params=pltpu.CompilerParams(
            dimension_semantics=("arbitrary", "arbitrary"), vmem_limit_bytes=VMEM_LIMIT),
        name="gla_layer",
    )(h, a_norm.reshape(1, D), w_row, w_kt, w_lrt, w_gk2p, b_gk2.reshape(1, qk_dim), w_gk2t,
      b_gk2.reshape(qk_dim, 1), o_norm.reshape(1, dv), w_out.astype(BF16))


def _router_kernel(h_ref, nrm_ref, wr_ref, br_ref, tri_ref, idx_ref, wgt_ref, cnt_ref, carry_ref):
    R = h_ref.shape[0]

    @pl.when(pl.program_id(0) == 0)
    def _():
        carry_ref[...] = jnp.zeros_like(carry_ref)

    h = h_ref[...]
    t = h * _rms_scale(h) * nrm_ref[...]
    logits = jnp.dot(t, wr_ref[...], preferred_element_type=F32, precision=lax.Precision.HIGHEST) + br_ref[...]
    lane = lax.broadcasted_iota(jnp.int32, (R, LANES), 1)
    neg = jnp.float32(-jnp.inf)
    big = jnp.int32(LANES)

    def first_argmax(vals):
        m = jnp.max(vals, axis=-1, keepdims=True)
        i = jnp.min(jnp.where(vals == m, lane, big), axis=-1, keepdims=True)
        return m, i

    gl = jnp.where(lane < N_GROUPS, logits, neg)
    g_max, g_idx = first_argmax(gl)
    g_w = 1.0 / jnp.sum(jnp.exp(gl - g_max), axis=-1, keepdims=True)

    e_lo = N_GROUPS + g_idx * EXPERTS_PER_GROUP
    el = jnp.where((lane >= e_lo) & (lane < e_lo + EXPERTS_PER_GROUP), logits, neg)
    m1, i1 = first_argmax(el)
    m2, i2 = first_argmax(jnp.where(lane == i1, neg, el))
    r = jnp.exp(m2 - m1)
    w1 = g_w / (1.0 + r)
    w2 = g_w * r / (1.0 + r)
    e1 = i1 - N_GROUPS
    e2 = i2 - N_GROUPS

    onehot = jnp.where((lane == e1) | (lane == e2), 1.0, 0.0)
    before = _dot(tri_ref[...], onehot.astype(BF16)) + carry_ref[0:1, :]
    rank1 = jnp.sum(jnp.where(lane == e1, before, 0.0), axis=-1, keepdims=True)
    rank2 = jnp.sum(jnp.where(lane == e2, before, 0.0), axis=-1, keepdims=True)
    total = carry_ref[0:1, :] + jnp.sum(onehot, axis=0, keepdims=True)
    carry_ref[...] = jnp.broadcast_to(total, carry_ref.shape)
    cnt_ref[...] = jnp.broadcast_to(total, cnt_ref.shape).astype(jnp.int32)

    rec = jnp.where(lane == 0, e1, jnp.where(lane == 1, e2, jnp.where(
        lane == 2, rank1.astype(jnp.int32), jnp.where(lane == 3, rank2.astype(jnp.int32), 0))))
    idx_ref[...] = rec
    wgt_ref[...] = jnp.where(lane == 0, w1, jnp.where(lane == 1, w2, 0.0))


def _router(h, m_norm, w_group, b_group, w_expert, b_expert):
    T, D = h.shape
    R = MOE_ROWS
    assert T % R == 0
    n_r = N_GROUPS + N_EXPERTS
    w_r = jnp.pad(jnp.concatenate([w_group, w_expert], axis=1), ((0, 0), (0, LANES - n_r)))
    b_r = jnp.pad(jnp.concatenate([b_group, b_expert]), (0, LANES - n_r)).reshape(1, LANES)
    idx = jnp.arange(R)
    tri = (idx[None, :] < idx[:, None]).astype(BF16)
    const = lambda i: (0, 0)
    return pl.pallas_call(
        _router_kernel,
        grid=(T // R,),
        in_specs=[
            pl.BlockSpec((R, D), lambda i: (i, 0)),
            pl.BlockSpec((1, D), const),
            pl.BlockSpec((D, LANES), const),
            pl.BlockSpec((1, LANES), const),
            pl.BlockSpec((R, R), const),
        ],
        out_specs=[
            pl.BlockSpec((R, LANES), lambda i: (i, 0)),
            pl.BlockSpec((R, LANES), lambda i: (i, 0)),
            pl.BlockSpec((8, LANES), const),
        ],
        out_shape=[
            jax.ShapeDtypeStruct((T, LANES), jnp.int32),
            jax.ShapeDtypeStruct((T, LANES), F32),
            jax.ShapeDtypeStruct((8, LANES), jnp.int32),
        ],
        scratch_shapes=[pltpu.VMEM((8, LANES), F32)],
        compiler_params=pltpu.CompilerParams(dimension_semantics=("arbitrary",), vmem_limit_bytes=VMEM_LIMIT),
        name="moe_router",
    )(h, m_norm.reshape(1, D), w_r, b_r, tri)


def _dispatch_kernel(pos_ref, h_ref, xs_in_ref, xs_ref, sem):
    del xs_in_ref
    R = h_ref.shape[0]

    def row_copy(r, k):
        dst = pos_ref[0, 0, 2 * r + k]
        return pltpu.make_async_copy(h_ref.at[pl.ds(r, 1)], xs_ref.at[pl.ds(dst, 1)], sem)

    def issue(r, c):
        row_copy(r, 0).start()
        row_copy(r, 1).start()
        return c

    lax.fori_loop(0, R, issue, 0)

    def drain(r, c):
        row_copy(r, 0).wait()
        row_copy(r, 1).wait()
        return c

    lax.fori_loop(0, R, drain, 0)


def _dispatch(h, pos, n_rows):
    T, D = h.shape
    R = MOE_ROWS
    n_t = T // R
    xs0 = jnp.zeros((n_rows, D), F32)
    return pl.pallas_call(
        _dispatch_kernel,
        grid=(n_t,),
        in_specs=[
            pl.BlockSpec((1, 1, 2 * R), lambda i: (i, 0, 0), memory_space=pltpu.SMEM),
            pl.BlockSpec((R, D), lambda i: (i, 0)),
            pl.BlockSpec(memory_space=pl.ANY),
        ],
        out_specs=pl.BlockSpec(memory_space=pl.ANY),
        out_shape=jax.ShapeDtypeStruct((n_rows, D), F32),
        scratch_shapes=[pltpu.SemaphoreType.DMA],
        input_output_aliases={2: 0},
        compiler_params=pltpu.CompilerParams(dimension_semantics=("arbitrary",), vmem_limit_bytes=VMEM_LIMIT),
        name="moe_dispatch",
    )(pos.reshape(n_t, 1, 2 * R), h, xs0)


def _experts_kernel(te_ref, xs_ref, nrm_ref, wgu_ref, wd_ref, ys_ref, *, d_expert):
    del te_ref
    x = xs_ref[...]
    t = (x * _rms_scale(x) * nrm_ref[...]).astype(BF16)
    gu = _dot(t, wgu_ref[0])
    gate = gu[:, :d_expert]
    hdn = (gate * _sigmoid(gate) * gu[:, d_expert:]).astype(BF16)
    ys_ref[...] = _dot(hdn, wd_ref[0])


def _experts(xs, tile_expert, m_norm, w_gate, w_up, w_down):
    n_rows, D = xs.shape
    d_expert = w_gate.shape[-1]
    TM = MOE_TM
    n_tiles = n_rows // TM
    w_gu = jnp.concatenate([w_gate, w_up], axis=-1).astype(BF16)
    w_d = w_down.astype(BF16)
    grid_spec = pltpu.PrefetchScalarGridSpec(
        num_scalar_prefetch=1,
        grid=(n_tiles,),
        in_specs=[
            pl.BlockSpec((TM, D), lambda i, te: (i, 0)),
            pl.BlockSpec((1, D), lambda i, te: (0, 0)),
            pl.BlockSpec((1, D, 2 * d_expert), lambda i, te: (te[i], 0, 0)),
            pl.BlockSpec((1, d_expert, D), lambda i, te: (te[i], 0, 0)),
        ],
        out_specs=pl.BlockSpec((TM, D), lambda i, te: (i, 0)),
    )
    return pl.pallas_call(
        functools.partial(_experts_kernel, d_expert=d_expert),
        grid_spec=grid_spec,
        out_shape=jax.ShapeDtypeStruct((n_rows, D), F32),
        compiler_params=pltpu.CompilerParams(dimension_semantics=("arbitrary",), vmem_limit_bytes=VMEM_LIMIT),
        name="moe_experts",
    )(tile_expert, xs, m_norm.reshape(1, D), w_gu, w_d)


def _combine_kernel(pos_ref, h_ref, wgt_ref, nrm_ref, ys_ref, out_ref, buf_ref, sem, *, final_norm):
    R = h_ref.shape[0]

    def row_copy(r, k):
        src = pos_ref[0, 0, 2 * r + k]
        return pltpu.make_async_copy(ys_ref.at[pl.ds(src, 1)], buf_ref.at[k, pl.ds(r, 1)], sem)

    def issue(r, c):
        row_copy(r, 0).start()
        row_copy(r, 1).start()
        return c

    lax.fori_loop(0, R, issue, 0)

    def drain(r, c):
        row_copy(r, 0).wait()
        row_copy(r, 1).wait()
        return c

    lax.fori_loop(0, R, drain, 0)

    w = wgt_ref[...]
    out = h_ref[...] + w[:, 0:1] * buf_ref[0] + w[:, 1:2] * buf_ref[1]
    if final_norm:
        out = out * _rms_scale(out) * nrm_ref[...]
    out_ref[...] = out


def _combine(h, pos, wgt, ys, norm, final_norm):
    T, D = h.shape
    R = MOE_ROWS
    n_t = T // R
    return pl.pallas_call(
        functools.partial(_combine_kernel, final_norm=final_norm),
        grid=(n_t,),
        in_specs=[
            pl.BlockSpec((1, 1, 2 * R), lambda i: (i, 0, 0), memory_space=pltpu.SMEM),
            pl.BlockSpec((R, D), lambda i: (i, 0)),
            pl.BlockSpec((R, LANES), lambda i: (i, 0)),
            pl.BlockSpec((1, D), lambda i: (0, 0)),
            pl.BlockSpec(memory_space=pl.ANY),
        ],
        out_specs=pl.BlockSpec((R, D), lambda i: (i, 0)),
        out_shape=jax.ShapeDtypeStruct((T, D), F32),
        scratch_shapes=[pltpu.VMEM((2, R, D), F32), pltpu.SemaphoreType.DMA],
        compiler_params=pltpu.CompilerParams(dimension_semantics=("arbitrary",), vmem_limit_bytes=VMEM_LIMIT),
        name="moe_combine",
    )(pos.reshape(n_t, 1, 2 * R), h, wgt, norm.reshape(1, D), ys)


def _hier_moe(h, m_norm, w_group, b_group, w_expert, b_expert, w_gate, w_up, w_down, out_norm, final_norm):
    T, D = h.shape
    TM = MOE_TM
    idx, wgt, cnt = _router(h, m_norm, w_group, b_group, w_expert, b_expert)

    counts = cnt[0, :N_EXPERTS]
    padded = ((counts + TM - 1) // TM) * TM
    ends = jnp.cumsum(padded)
    starts = ends - padded
    n_rows = 2 * T + N_EXPERTS * TM
    n_tiles = n_rows // TM
    tile_expert = jnp.minimum(
        jnp.searchsorted(ends, jnp.arange(n_tiles, dtype=jnp.int32) * TM, side="right"), N_EXPERTS - 1
    ).astype(jnp.int32)
    pos = (starts[idx[:, 0:2]] + idx[:, 2:4]).astype(jnp.int32)

    xs = _dispatch(h, pos, n_rows)
    ys = _experts(xs, tile_expert, m_norm, w_gate, w_up, w_down)
    return _combine(h, pos, wgt, ys, out_norm, final_norm)


def _qkv_kernel(h_ref, qn_ref, kvn_ref, wq_ref, wkt_ref, wv_ref, q_ref, kt_ref, v_ref):
    h = h_ref[...]
    xhat = h * _rms_scale(h)
    uq = (xhat * qn_ref[...]).astype(BF16)
    ukv = (xhat * kvn_ref[...]).astype(BF16)
    q_ref[...] = _dot(uq, wq_ref[...]).astype(BF16)
    v_ref[...] = _dot(ukv, wv_ref[...]).astype(BF16)
    kt_ref[0] = _dot_nt(wkt_ref[...], ukv).astype(BF16)


def _qkv_proj(h, batch, seq, q_norm, kv_norm, w_q, w_kv, scale):
    T, D = h.shape
    sb_dim = w_q.shape[1]
    R = PROJ_ROWS
    assert seq % R == 0
    n_s = seq // R
    w_qs = (w_q * scale).astype(BF16)
    w_kt = w_kv[:, :sb_dim].T.astype(BF16)
    w_v = w_kv[:, sb_dim:].astype(BF16)
    const = lambda b, s: (0, 0)
    return pl.pallas_call(
        _qkv_kernel,
        grid=(batch, n_s),
        in_specs=[
            pl.BlockSpec((R, D), lambda b, s: (b * n_s + s, 0)),
            pl.BlockSpec((1, D), const),
            pl.BlockSpec((1, D), const),
            pl.BlockSpec((D, sb_dim), const),
            pl.BlockSpec((sb_dim, D), const),
            pl.BlockSpec((D, sb_dim), const),
        ],
        out_specs=[
            pl.BlockSpec((R, sb_dim), lambda b, s: (b * n_s + s, 0)),
            pl.BlockSpec((1, sb_dim, R), lambda b, s: (b, 0, s)),
            pl.BlockSpec((R, sb_dim), lambda b, s: (b * n_s + s, 0)),
        ],
        out_shape=[
            jax.ShapeDtypeStruct((T, sb_dim), BF16),
            jax.ShapeDtypeStruct((batch, sb_dim, seq), BF16),
            jax.ShapeDtypeStruct((T, sb_dim), BF16),
        ],
        compiler_params=pltpu.CompilerParams(
            dimension_semantics=("arbitrary", "arbitrary"), vmem_limit_bytes=VMEM_LIMIT),
        name="qkv_proj",
    )(h, q_norm.reshape(1, D), kv_norm.reshape(1, D), w_qs, w_kt, w_v)


def _sb_attention_kernel(q_ref, kt_ref, v_ref, o_ref, *, head_dim):
    S = q_ref.shape[0]
    TQ, TK = ATT_TQ, ATT_TK
    n_q = S // TQ
    ratio = TQ // TK
    lane_q = lax.broadcasted_iota(jnp.int32, (TQ, LANES), 1)
    trow = lax.broadcasted_iota(jnp.int32, (TQ, TK), 0)
    scol = lax.broadcasted_iota(jnp.int32, (TQ, TK), 1)
    kr = lax.broadcasted_iota(jnp.int32, (TK, TK), 0)
    kc = lax.broadcasted_iota(jnp.int32, (TK, TK), 1)
    suffix = jnp.where(kr >= kc, 1.0, 0.0).astype(BF16)
    ones = jnp.ones((TK, TK), BF16)

    def q_block(qi, carry):
        q0 = pl.multiple_of(qi * TQ, TQ)
        q_pair = q_ref[pl.ds(q0, TQ), :]
        result = jnp.zeros((TQ, LANES), F32)
        for head in range(LANES // head_dim):
            in_head = (lane_q >= head * head_dim) & (lane_q < (head + 1) * head_dim)
            q_h = jnp.where(in_head, q_pair, jnp.zeros_like(q_pair))

            def k_tile(step, st, diagonal):
                acc, run = st
                j = (qi + 1) * ratio - 1 - step
                k0 = pl.multiple_of(j * TK, TK)
                z = _dot(q_h, kt_ref[0, :, pl.ds(k0, TK)])
                sp = _softplus(z)
                if diagonal:
                    mask = (k0 + scol) < (q0 + trow)
                    sp = jnp.where(mask, sp, 0.0)
                sp16 = sp.astype(BF16)
                within = _dot(sp16, suffix)
                a = jnp.exp(z - within - run)
                if diagonal:
                    a = jnp.where(mask, a, 0.0)
                acc = acc + _dot(a.astype(BF16), v_ref[pl.ds(k0, TK), :])
                run = run + _dot(sp16, ones)
                return acc, run

            st = (jnp.zeros((TQ, LANES), F32), jnp.zeros((TQ, TK), F32))
            for step in range(ratio):
                st = k_tile(step, st, True)
            st = lax.fori_loop(ratio, (qi + 1) * ratio, functools.partial(k_tile, diagonal=False), st)
            result = jnp.where(in_head, st[0], result)
        o_ref[pl.ds(q0, TQ), :] = result.astype(o_ref.dtype)
        return carry

    lax.fori_loop(0, n_q, q_block, 0)


def _sb_attention(q, kt, v, batch, seq, head_dim):
    T, sb_dim = q.shape
    assert LANES % head_dim == 0 and seq % ATT_TQ == 0 and ATT_TQ % ATT_TK == 0
    n_p = sb_dim // LANES
    return pl.pallas_call(
        functools.partial(_sb_attention_kernel, head_dim=head_dim),
        grid=(batch, n_p),
        in_specs=[
            pl.BlockSpec((seq, LANES), lambda b, p: (b, p)),
            pl.BlockSpec((1, LANES, seq), lambda b, p: (b, p, 0)),
            pl.BlockSpec((seq, LANES), lambda b, p: (b, p)),
        ],
        out_specs=pl.BlockSpec((seq, LANES), lambda b, p: (b, p)),
        out_shape=jax.ShapeDtypeStruct((T, sb_dim), BF16),
        compiler_params=pltpu.CompilerParams(
            dimension_semantics=("arbitrary", "arbitrary"), vmem_limit_bytes=VMEM_LIMIT),
        name="sb_attention",
    )(q, kt, v)


def _out_proj_kernel(o_ref, w_ref, h_ref, out_ref):
    out_ref[...] = h_ref[...] + _dot(o_ref[...], w_ref[...])


def _out_proj(o, w_out, h):
    T, D = h.shape
    R = PROJ_ROWS
    return pl.pallas_call(
        _out_proj_kernel,
        grid=(T // R,),
        in_specs=[
            pl.BlockSpec((R, o.shape[1]), lambda i: (i, 0)),
            pl.BlockSpec(w_out.shape, lambda i: (0, 0)),
            pl.BlockSpec((R, D), lambda i: (i, 0)),
        ],
        out_specs=pl.BlockSpec((R, D), lambda i: (i, 0)),
        out_shape=jax.ShapeDtypeStruct((T, D), F32),
        compiler_params=pltpu.CompilerParams(dimension_semantics=("arbitrary",), vmem_limit_bytes=VMEM_LIMIT),
        name="out_proj",
    )(o, w_out.astype(BF16), h)


def kernel(x, a_norm, a_w_in, a_w_gk2, a_b_gk2, a_o_norm, a_w_out, kv_norm, w_kv, b_norm, b_w_q, b_w_out,
           m_norm, m_w_group, m_b_group, m_w_expert, m_b_expert, m_w_gate, m_w_up, m_w_down, final_norm):
    B, S, D = x.shape
    assert a_norm.shape[0] == 1 and b_norm.shape[0] == 1 and m_norm.shape[0] == 2
    head_dim = b_w_q.shape[2] // SB_HEADS
    h = x.reshape(B * S, D)

    def moe(h, layer, out_norm, final):
        return _hier_moe(h, m_norm[layer], m_w_group[layer], m_b_group[layer], m_w_expert[layer],
                         m_b_expert[layer], m_w_gate[layer], m_w_up[layer], m_w_down[layer], out_norm, final)

    h = _gla_layer(h, B, S, a_norm[0], a_w_in[0], a_w_gk2[0], a_b_gk2[0], a_o_norm[0], a_w_out[0])
    h = moe(h, 0, final_norm, False)

    q, kt, v = _qkv_proj(h, B, S, b_norm[0], kv_norm, b_w_q[0], w_kv, 1.0 / math.sqrt(head_dim))
    o = _sb_attention(q, kt, v, B, S, head_dim)
    h = _out_proj(o, b_w_out[0], h)
    h = moe(h, 1, final_norm, True)
    return h.reshape(B, S, D)
```

```python
import functools
import math

import jax
import jax.numpy as jnp
from jax import lax
from jax.experimental import pallas as pl
from jax.experimental.pallas import tpu as pltpu

RMS_EPS = 1e-6

GLA_HEADS = 4
GLA_CHUNK = 64
CHUNK_SHIFT = GLA_CHUNK.bit_length() - 1
GATE_NORMALIZER = 16.0
SB_HEADS = 16
N_GROUPS = 4
EXPERTS_PER_GROUP = 4
N_EXPERTS = N_GROUPS * EXPERTS_PER_GROUP
PAIRS_PER_GROUP = EXPERTS_PER_GROUP * (EXPERTS_PER_GROUP - 1) // 2
N_CLASSES = N_GROUPS * PAIRS_PER_GROUP

LANES = 128
VMEM_LIMIT = 56 * 1024 * 1024

GLA_ROWS = 256
PROJ_ROWS = 512
ATT_TILE = 256
ATT_ZERO_MARGIN = 160.0
MOE_ROWS = 256
MOE_TM = 256
DMA_UNROLL = 8

BF16 = jnp.bfloat16
F32 = jnp.float32


def _dot(a, b):
    return jnp.dot(a, b, preferred_element_type=F32)


def _dot_nt(a, b):
    return lax.dot_general(a, b, (((1,), (1,)), ((), ())), preferred_element_type=F32)


def _split_dot(m01, x, left):
    hi = x.astype(BF16)
    lo = (x - hi.astype(F32)).astype(BF16)
    if left:
        return _dot(m01, hi) + _dot(m01, lo)
    return _dot(hi, m01) + _dot(lo, m01)


def _rms_scale(x):
    return lax.rsqrt(jnp.mean(x * x, axis=-1, keepdims=True) + RMS_EPS)


def _log_sigmoid(x):
    return jnp.minimum(x, 0.0) - jnp.log(1.0 + jnp.exp(-jnp.abs(x)))


def _softplus(x):
    return jnp.maximum(x, 0.0) + jnp.log(1.0 + jnp.exp(-jnp.abs(x)))


def _sigmoid(x):
    return 1.0 / (1.0 + jnp.exp(-x))


def _gla_kernel(h_ref, nrm_ref, wrow_ref, wkt_ref, wlrt_ref, wgk2_ref, bgk2_ref, wgk2t_ref, bgk2c_ref,
                onorm_ref, wout_ref, out_ref, state_ref, *, dk, dv, qk_dim, v_dim, log_scale):
    R = h_ref.shape[0]
    n_chunks = R // GLA_CHUNK

    @pl.when(pl.program_id(1) == 0)
    def _():
        state_ref[...] = jnp.zeros_like(state_ref)

    h = h_ref[...]
    u = (h * _rms_scale(h) * nrm_ref[...]).astype(BF16)

    proj = _dot(u, wrow_ref[...])
    q = proj[:, :qk_dim]
    v = proj[:, qk_dim:qk_dim + v_dim].astype(BF16)
    g = proj[:, qk_dim + v_dim:qk_dim + 2 * v_dim]
    lr = proj[:, qk_dim + 2 * v_dim:].astype(BF16)
    kt = _dot_nt(wkt_ref[...], u)
    lrt = _dot_nt(wlrt_ref[...], u).astype(BF16)

    gk = _log_sigmoid(_dot(lr, wgk2_ref[...]) + bgk2_ref[...]) * (1.0 / GATE_NORMALIZER)
    gkt = _log_sigmoid(_dot(wgk2t_ref[...], lrt) + bgk2c_ref[...]) * (1.0 / GATE_NORMALIZER)

    row = lax.broadcasted_iota(jnp.int32, (R, R), 0)
    col = lax.broadcasted_iota(jnp.int32, (R, R), 1)
    same_chunk = (row >> CHUNK_SHIFT) == (col >> CHUNK_SHIFT)
    causal = same_chunk & (col <= row)
    lbd = jnp.where(causal, 1.0, 0.0).astype(BF16)
    ubd = jnp.where(same_chunk & (row <= col), 1.0, 0.0).astype(BF16)
    after = jnp.where(same_chunk & (row > col), 1.0, 0.0).astype(BF16)

    b = _split_dot(lbd, gk, left=True)
    bt = _split_dot(ubd, gkt, left=False)
    tail_t = _split_dot(after, gkt, left=False)

    q_dec = (q * jnp.exp(b + log_scale)).astype(BF16)
    k_inv_t = (kt * jnp.exp(-bt)).astype(BF16)
    k_end_t = (kt * jnp.exp(tail_t)).astype(BF16)
    chunk_decay_t = jnp.exp(bt + tail_t)

    lane_chunk = lax.broadcasted_iota(jnp.int32, (dk, R), 1) >> CHUNK_SHIFT

    acc = h
    for hd in range(GLA_HEADS):
        ks = slice(hd * dk, (hd + 1) * dk)
        vs = slice(hd * dv, (hd + 1) * dv)
        qd_h = q_dec[:, ks]
        v_h = v[:, vs]
        att = _dot(qd_h, k_inv_t[ks, :])
        att = jnp.where(causal, att, 0.0).astype(BF16)
        o_h = _dot(att, v_h)
        kend_h = k_end_t[ks, :]
        state = state_ref[hd]
        inter = []
        for c in range(n_chunks):
            rows = slice(c * GLA_CHUNK, (c + 1) * GLA_CHUNK)
            inter.append(_dot(qd_h[rows], state.astype(BF16)))
            kend_c = jnp.where(lane_chunk == c, kend_h, jnp.zeros_like(kend_h))
            decay = chunk_decay_t[ks, c * GLA_CHUNK:c * GLA_CHUNK + 1]
            state = decay * state + _dot(kend_c, v_h)
        state_ref[hd] = state
        o_h = o_h + jnp.concatenate(inter, axis=0)
        o_h = o_h * _rms_scale(o_h) * onorm_ref[...]
        g_h = g[:, vs]
        o_h = o_h * (g_h * _sigmoid(g_h))
        acc = acc + _dot(o_h.astype(BF16), wout_ref[vs, :])
    out_ref[...] = acc


def _gla_layer(h, batch, seq, a_norm, w_in, w_gk2, b_gk2, o_norm, w_out):
    T, D = h.shape
    rank, qk_dim = w_gk2.shape
    v_dim = w_out.shape[0]
    dk = qk_dim // GLA_HEADS
    dv = v_dim // GLA_HEADS
    R = GLA_ROWS
    assert seq % R == 0 and R % GLA_CHUNK == 0 and rank <= LANES
    assert w_in.shape[1] == 2 * qk_dim + 2 * v_dim + rank
    n_s = seq // R

    o_q, o_k, o_v, o_lr, o_g = 0, qk_dim, 2 * qk_dim, 2 * qk_dim + v_dim, 2 * qk_dim + v_dim + rank
    w_lr = jnp.pad(w_in[:, o_lr:o_lr + rank], ((0, 0), (0, LANES - rank)))
    w_row = jnp.concatenate([w_in[:, o_q:o_k], w_in[:, o_v:o_lr], w_in[:, o_g:], w_lr], axis=1).astype(BF16)
    w_kt = w_in[:, o_k:o_v].T.astype(BF16)
    w_lrt = w_lr.T.astype(BF16)
    w_gk2p = jnp.pad(w_gk2, ((0, LANES - rank), (0, 0))).astype(BF16)
    w_gk2t = w_gk2p.T
    const = lambda b, s: (0, 0)
    kern = functools.partial(_gla_kernel, dk=dk, dv=dv, qk_dim=qk_dim, v_dim=v_dim,
                             log_scale=math.log(dk ** -0.5))
    return pl.pallas_call(
        kern,
        grid=(batch, n_s),
        in_specs=[
            pl.BlockSpec((R, D), lambda b, s: (b * n_s + s, 0)),
            pl.BlockSpec((1, D), const),
            pl.BlockSpec(w_row.shape, const),
            pl.BlockSpec(w_kt.shape, const),
            pl.BlockSpec(w_lrt.shape, const),
            pl.BlockSpec(w_gk2p.shape, const),
            pl.BlockSpec((1, qk_dim), const),
            pl.BlockSpec(w_gk2t.shape, const),
            pl.BlockSpec((qk_dim, 1), const),
            pl.BlockSpec((1, dv), const),
            pl.BlockSpec((v_dim, D), const),
        ],
        out_specs=pl.BlockSpec((R, D), lambda b, s: (b * n_s + s, 0)),
        out_shape=jax.ShapeDtypeStruct((T, D), F32),
        scratch_shapes=[pltpu.VMEM((GLA_HEADS, dk, dv), F32)],
        compiler_params=pltpu.CompilerParams(
            dimension_semantics=("arbitrary", "arbitrary"), vmem_limit_bytes=VMEM_LIMIT),
        name="gla_layer",
    )(h, a_norm.reshape(1, D), w_row, w_kt, w_lrt, w_gk2p, b_gk2.reshape(1, qk_dim), w_gk2t,
      b_gk2.reshape(qk_dim, 1), o_norm.reshape(1, dv), w_out.astype(BF16))


def _router_kernel(h_ref, nrm_ref, wr_ref, br_ref, tri_ref, idx_ref, wgt_ref, cnt_ref, carry_ref):
    R = h_ref.shape[0]

    @pl.when(pl.program_id(0) == 0)
    def _():
        carry_ref[...] = jnp.zeros_like(carry_ref)

    h = h_ref[...]
    t = h * _rms_scale(h) * nrm_ref[...]
    t_hi = t.astype(BF16)
    t_lo = (t - t_hi.astype(F32)).astype(BF16)
    p = _dot(t_hi, wr_ref[...])
    logits = p[:, :LANES] + p[:, LANES:] + _dot(t_lo, wr_ref[:, :LANES]) + br_ref[...]
    lane = lax.broadcasted_iota(jnp.int32, (R, LANES), 1)
    neg = jnp.float32(-jnp.inf)
    big = jnp.int32(LANES)

    def first_argmax(vals):
        m = jnp.max(vals, axis=-1, keepdims=True)
        i = jnp.min(jnp.where(vals == m, lane, big), axis=-1, keepdims=True)
        return m, i

    gl = jnp.where(lane < N_GROUPS, logits, neg)
    g_max, g_idx = first_argmax(gl)
    g_w = 1.0 / jnp.sum(jnp.exp(gl - g_max), axis=-1, keepdims=True)

    e_lo = N_GROUPS + g_idx * EXPERTS_PER_GROUP
    el = jnp.where((lane >= e_lo) & (lane < e_lo + EXPERTS_PER_GROUP), logits, neg)
    m1, i1 = first_argmax(el)
    m2, i2 = first_argmax(jnp.where(lane == i1, neg, el))
    r = jnp.exp(m2 - m1)
    w1 = g_w / (1.0 + r)
    w2 = g_w * r / (1.0 + r)
    l1 = i1 - e_lo
    l2 = i2 - e_lo
    first_is_lo = l1 < l2
    lo = jnp.minimum(l1, l2)
    hi = jnp.maximum(l1, l2)
    cls = g_idx * PAIRS_PER_GROUP + ((lo * (2 * EXPERTS_PER_GROUP - 1 - lo)) >> 1) + (hi - lo - 1)
    w_lo = jnp.where(first_is_lo, w1, w2)
    w_hi = jnp.where(first_is_lo, w2, w1)

    onehot = jnp.where(lane == cls, 1.0, 0.0)
    before = _dot(tri_ref[...], onehot.astype(BF16)) + carry_ref[0:1, :]
    rank = jnp.sum(jnp.where(lane == cls, before, 0.0), axis=-1, keepdims=True)
    total = carry_ref[0:1, :] + jnp.sum(onehot, axis=0, keepdims=True)
    carry_ref[...] = jnp.broadcast_to(total, carry_ref.shape)
    cnt_ref[...] = jnp.broadcast_to(total, cnt_ref.shape).astype(jnp.int32)

    idx_ref[...] = jnp.where(lane == 0, cls, jnp.where(lane == 1, rank.astype(jnp.int32), 0))
    wgt_ref[...] = jnp.where(lane == 0, w_lo, jnp.where(lane == 1, w_hi, 0.0))


def _router(h, m_norm, w_group, b_group, w_expert, b_expert):
    T, D = h.shape
    R = MOE_ROWS
    assert T % R == 0
    n_r = N_GROUPS + N_EXPERTS
    w_r = jnp.pad(jnp.concatenate([w_group, w_expert], axis=1), ((0, 0), (0, LANES - n_r)))
    w_r_hi = w_r.astype(BF16)
    w_r = jnp.concatenate([w_r_hi, (w_r - w_r_hi.astype(F32)).astype(BF16)], axis=1)
    b_r = jnp.pad(jnp.concatenate([b_group, b_expert]), (0, LANES - n_r)).reshape(1, LANES)
    idx = jnp.arange(R)
    tri = (idx[None, :] < idx[:, None]).astype(BF16)
    const = lambda i: (0, 0)
    return pl.pallas_call(
        _router_kernel,
        grid=(T // R,),
        in_specs=[
            pl.BlockSpec((R, D), lambda i: (i, 0)),
            pl.BlockSpec((1, D), const),
            pl.BlockSpec((D, 2 * LANES), const),
            pl.BlockSpec((1, LANES), const),
            pl.BlockSpec((R, R), const),
        ],
        out_specs=[
            pl.BlockSpec((R, LANES), lambda i: (i, 0)),
            pl.BlockSpec((R, LANES), lambda i: (i, 0)),
            pl.BlockSpec((8, LANES), const),
        ],
        out_shape=[
            jax.ShapeDtypeStruct((T, LANES), jnp.int32),
            jax.ShapeDtypeStruct((T, LANES), F32),
            jax.ShapeDtypeStruct((8, LANES), jnp.int32),
        ],
        scratch_shapes=[pltpu.VMEM((8, LANES), F32)],
        compiler_params=pltpu.CompilerParams(dimension_semantics=("arbitrary",), vmem_limit_bytes=VMEM_LIMIT),
        name="moe_router",
    )(h, m_norm.reshape(1, D), w_r, b_r, tri)


def _dispatch_kernel(pos_ref, h_ref, wgt_ref, xs_in_ref, xs_ref, row_ref, sem):
    del xs_in_ref
    R, D = h_ref.shape
    row_ref[:, :D] = h_ref[...]
    row_ref[:, D:] = wgt_ref[...]

    def row_copy(r):
        return pltpu.make_async_copy(row_ref.at[pl.ds(r, 1)], xs_ref.at[pl.ds(pos_ref[0, 0, r], 1)], sem)

    def issue(r, c):
        row_copy(r).start()
        return c

    lax.fori_loop(0, R, issue, 0, unroll=DMA_UNROLL)

    def drain(r, c):
        row_copy(r).wait()
        return c

    lax.fori_loop(0, R, drain, 0, unroll=DMA_UNROLL)


def _dispatch(h, wgt, pos, n_rows):
    T, D = h.shape
    R = MOE_ROWS
    n_t = T // R
    xs0 = jnp.zeros((n_rows, D + LANES), F32)
    return pl.pallas_call(
        _dispatch_kernel,
        grid=(n_t,),
        in_specs=[
            pl.BlockSpec((1, 1, R), lambda i: (i, 0, 0), memory_space=pltpu.SMEM),
            pl.BlockSpec((R, D), lambda i: (i, 0)),
            pl.BlockSpec((R, LANES), lambda i: (i, 0)),
            pl.BlockSpec(memory_space=pl.ANY),
        ],
        out_specs=pl.BlockSpec(memory_space=pl.ANY),
        out_shape=jax.ShapeDtypeStruct((n_rows, D + LANES), F32),
        scratch_shapes=[pltpu.VMEM((R, D + LANES), F32), pltpu.SemaphoreType.DMA],
        input_output_aliases={3: 0},
        compiler_params=pltpu.CompilerParams(dimension_semantics=("arbitrary",), vmem_limit_bytes=VMEM_LIMIT),
        name="moe_dispatch",
    )(pos.reshape(n_t, 1, R), h, wgt, xs0)


def _experts_kernel(used_ref, lo_ref, hi_ref, xs_ref, nrm_ref, wgu_lo_ref, wd_lo_ref, wgu_hi_ref, wd_hi_ref,
                    ys_ref, *, d_model, d_expert):
    del lo_ref, hi_ref
    active = pl.program_id(0) < used_ref[0]

    @pl.when(active)
    def _():
        x = xs_ref[:, :d_model]
        t = (x * _rms_scale(x) * nrm_ref[...]).astype(BF16)
        y = None
        for k, (wgu_ref, wd_ref) in enumerate(((wgu_lo_ref, wd_lo_ref), (wgu_hi_ref, wd_hi_ref))):
            gu = _dot(t, wgu_ref[0])
            gate = gu[:, :d_expert]
            hdn = (gate * _sigmoid(gate) * gu[:, d_expert:]).astype(BF16)
            y_k = xs_ref[:, d_model + k:d_model + k + 1] * _dot(hdn, wd_ref[0])
            y = y_k if y is None else y + y_k
        ys_ref[...] = y

    @pl.when(jnp.logical_not(active))
    def _():
        ys_ref[...] = jnp.zeros_like(ys_ref)


def _experts(xs, n_used, tile_lo, tile_hi, m_norm, w_gate, w_up, w_down):
    n_rows = xs.shape[0]
    D = m_norm.shape[0]
    d_expert = w_gate.shape[-1]
    TM = MOE_TM
    n_tiles = n_rows // TM
    w_gu = jnp.concatenate([w_gate, w_up], axis=-1).astype(BF16)
    w_d = w_down.astype(BF16)
    grid_spec = pltpu.PrefetchScalarGridSpec(
        num_scalar_prefetch=3,
        grid=(n_tiles,),
        in_specs=[
            pl.BlockSpec((TM, D + LANES), lambda i, nu, lo, hi: (i, 0)),
            pl.BlockSpec((1, D), lambda i, nu, lo, hi: (0, 0)),
            pl.BlockSpec((1, D, 2 * d_expert), lambda i, nu, lo, hi: (lo[i], 0, 0)),
            pl.BlockSpec((1, d_expert, D), lambda i, nu, lo, hi: (lo[i], 0, 0)),
            pl.BlockSpec((1, D, 2 * d_expert), lambda i, nu, lo, hi: (hi[i], 0, 0)),
            pl.BlockSpec((1, d_expert, D), lambda i, nu, lo, hi: (hi[i], 0, 0)),
        ],
        out_specs=pl.BlockSpec((TM, D), lambda i, nu, lo, hi: (i, 0)),
    )
    return pl.pallas_call(
        functools.partial(_experts_kernel, d_model=D, d_expert=d_expert),
        grid_spec=grid_spec,
        out_shape=jax.ShapeDtypeStruct((n_rows, D), F32),
        compiler_params=pltpu.CompilerParams(dimension_semantics=("arbitrary",), vmem_limit_bytes=VMEM_LIMIT),
        name="moe_experts",
    )(n_used, tile_lo, tile_hi, xs, m_norm.reshape(1, D), w_gu, w_d, w_gu, w_d)


def _combine_kernel(pos_ref, h_ref, nrm_ref, ys_ref, out_ref, buf_ref, sem, *, final_norm):
    R = h_ref.shape[0]

    def row_copy(r):
        return pltpu.make_async_copy(ys_ref.at[pl.ds(pos_ref[0, 0, r], 1)], buf_ref.at[pl.ds(r, 1)], sem)

    def issue(r, c):
        row_copy(r).start()
        return c

    lax.fori_loop(0, R, issue, 0, unroll=DMA_UNROLL)

    def drain(r, c):
        row_copy(r).wait()
        return c

    lax.fori_loop(0, R, drain, 0, unroll=DMA_UNROLL)

    out = h_ref[...] + buf_ref[...]
    if final_norm:
        out = out * _rms_scale(out) * nrm_ref[...]
    out_ref[...] = out


def _combine(h, pos, ys, norm, final_norm):
    T, D = h.shape
    R = MOE_ROWS
    n_t = T // R
    return pl.pallas_call(
        functools.partial(_combine_kernel, final_norm=final_norm),
        grid=(n_t,),
        in_specs=[
            pl.BlockSpec((1, 1, R), lambda i: (i, 0, 0), memory_space=pltpu.SMEM),
            pl.BlockSpec((R, D), lambda i: (i, 0)),
            pl.BlockSpec((1, D), lambda i: (0, 0)),
            pl.BlockSpec(memory_space=pl.ANY),
        ],
        out_specs=pl.BlockSpec((R, D), lambda i: (i, 0)),
        out_shape=jax.ShapeDtypeStruct((T, D), F32),
        scratch_shapes=[pltpu.VMEM((R, D), F32), pltpu.SemaphoreType.DMA],
        compiler_params=pltpu.CompilerParams(dimension_semantics=("arbitrary",), vmem_limit_bytes=VMEM_LIMIT),
        name="moe_combine",
    )(pos.reshape(n_t, 1, R), h, norm.reshape(1, D), ys)


def _class_experts():
    lo, hi = [], []
    for g in range(N_GROUPS):
        for a in range(EXPERTS_PER_GROUP):
            for b in range(a + 1, EXPERTS_PER_GROUP):
                lo.append(g * EXPERTS_PER_GROUP + a)
                hi.append(g * EXPERTS_PER_GROUP + b)
    return jnp.asarray(lo, jnp.int32), jnp.asarray(hi, jnp.int32)


def _hier_moe(h, m_norm, w_group, b_group, w_expert, b_expert, w_gate, w_up, w_down, out_norm, final_norm):
    T, D = h.shape
    TM = MOE_TM
    idx, wgt, cnt = _router(h, m_norm, w_group, b_group, w_expert, b_expert)

    counts = cnt[0, :N_CLASSES]
    padded = ((counts + TM - 1) // TM) * TM
    ends = jnp.cumsum(padded)
    starts = ends - padded
    n_rows = T + N_CLASSES * TM
    n_tiles = n_rows // TM
    tile_start = jnp.arange(n_tiles, dtype=jnp.int32) * TM
    tile_class = jnp.minimum(
        jnp.sum((ends[None, :] <= tile_start[:, None]).astype(jnp.int32), axis=1), N_CLASSES - 1)
    class_lo, class_hi = _class_experts()
    n_used = (ends[-1] // TM).astype(jnp.int32).reshape(1)
    pos = (starts[idx[:, 0]] + idx[:, 1]).astype(jnp.int32)

    xs = _dispatch(h, wgt, pos, n_rows)
    ys = _experts(xs, n_used, class_lo[tile_class], class_hi[tile_class], m_norm, w_gate, w_up, w_down)
    return _combine(h, pos, ys, out_norm, final_norm)


def _qkv_kernel(h_ref, qn_ref, kvn_ref, wq_ref, wkt_ref, wv_ref, q_ref, kt_ref, v_ref):
    h = h_ref[...]
    xhat = h * _rms_scale(h)
    uq = (xhat * qn_ref[...]).astype(BF16)
    ukv = (xhat * kvn_ref[...]).astype(BF16)
    q_ref[...] = _dot(uq, wq_ref[...]).astype(BF16)
    v_ref[...] = _dot(ukv, wv_ref[...]).astype(BF16)
    kt_ref[0] = _dot_nt(wkt_ref[...], ukv).astype(BF16)


def _qkv_proj(h, batch, seq, q_norm, kv_norm, w_q, w_kv, scale):
    T, D = h.shape
    sb_dim = w_q.shape[1]
    R = PROJ_ROWS
    assert seq % R == 0
    n_s = seq // R
    w_qs = (w_q * scale).astype(BF16)
    w_kt = w_kv[:, :sb_dim].T.astype(BF16)
    w_v = w_kv[:, sb_dim:].astype(BF16)
    const = lambda b, s: (0, 0)
    return pl.pallas_call(
        _qkv_kernel,
        grid=(batch, n_s),
        in_specs=[
            pl.BlockSpec((R, D), lambda b, s: (b * n_s + s, 0)),
            pl.BlockSpec((1, D), const),
            pl.BlockSpec((1, D), const),
            pl.BlockSpec((D, sb_dim), const),
            pl.BlockSpec((sb_dim, D), const),
            pl.BlockSpec((D, sb_dim), const),
        ],
        out_specs=[
            pl.BlockSpec((R, sb_dim), lambda b, s: (b * n_s + s, 0)),
            pl.BlockSpec((1, sb_dim, R), lambda b, s: (b, 0, s)),
            pl.BlockSpec((R, sb_dim), lambda b, s: (b * n_s + s, 0)),
        ],
        out_shape=[
            jax.ShapeDtypeStruct((T, sb_dim), BF16),
            jax.ShapeDtypeStruct((batch, sb_dim, seq), BF16),
            jax.ShapeDtypeStruct((T, sb_dim), BF16),
        ],
        compiler_params=pltpu.CompilerParams(
            dimension_semantics=("arbitrary", "arbitrary"), vmem_limit_bytes=VMEM_LIMIT),
        name="qkv_proj",
    )(h, q_norm.reshape(1, D), kv_norm.reshape(1, D), w_qs, w_kt, w_v)


def _sb_attention_kernel(q_ref, kt_ref, v_ref, o_ref, *, head_dim):
    S = q_ref.shape[0]
    T = ATT_TILE
    n_q = S // T
    n_heads = LANES // head_dim
    lane_q = lax.broadcasted_iota(jnp.int32, (T, LANES), 1)
    trow = lax.broadcasted_iota(jnp.int32, (T, T), 0)
    scol = lax.broadcasted_iota(jnp.int32, (T, T), 1)
    strictly_before = scol < trow
    suffix = jnp.where(trow >= scol, 1.0, 0.0).astype(BF16)
    in_head = [(lane_q >= hd * head_dim) & (lane_q < (hd + 1) * head_dim) for hd in range(n_heads)]

    ksq = jnp.square(kt_ref[0].astype(F32))
    k_max = [jnp.sqrt(jnp.max(jnp.sum(ksq[hd * head_dim:(hd + 1) * head_dim], axis=0, keepdims=True),
                              axis=1, keepdims=True)) for hd in range(n_heads)]

    def tile(q_h, k0, acc, run, mask):
        z = _dot(q_h, kt_ref[0, :, pl.ds(k0, T)])
        sp = jnp.maximum(z, 0.0) + jnp.log2(1.0 + jnp.exp2(-jnp.abs(z)))
        if mask is not None:
            sp = jnp.where(mask, sp, 0.0)
        within = _dot(sp.astype(BF16), suffix)
        a = jnp.exp2(z - within - run)
        if mask is not None:
            a = jnp.where(mask, a, 0.0)
        acc = acc + _dot(a.astype(BF16), v_ref[pl.ds(k0, T), :])
        return acc, run + within[:, 0:1]

    def q_block(qi, with_left):
        q0 = qi * T if isinstance(qi, int) else pl.multiple_of(qi * T, T)
        q_pair = q_ref[pl.ds(q0, T), :]
        qsq = jnp.square(q_pair.astype(F32))
        q_hs = [jnp.where(m, q_pair, jnp.zeros_like(q_pair)) for m in in_head]
        z_bound = [jnp.sqrt(jnp.sum(jnp.where(m, qsq, 0.0), axis=-1, keepdims=True)) * km
                   for m, km in zip(in_head, k_max)]

        def all_zero_from_here(runs):
            slack = runs[0] - z_bound[0]
            for r, zb in zip(runs[1:], z_bound[1:]):
                slack = jnp.minimum(slack, r - zb)
            return (jnp.min(slack) > ATT_ZERO_MARGIN).astype(jnp.int32)

        st = [tile(q_h, q0, jnp.zeros((T, LANES), F32), jnp.zeros((T, 1), F32), strictly_before) for q_h in q_hs]
        if with_left:
            st = [tile(q_h, pl.multiple_of(q0 - T, T), a, r, None) for q_h, (a, r) in zip(q_hs, st)]
        accs = [a for a, _ in st]
        runs = [r for _, r in st]

        def cond(c):
            return (c[0] >= 0) & (c[1] == 0)

        def body(c):
            j, _, accs, runs = c
            k0 = pl.multiple_of(j * T, T)
            st = [tile(q_h, k0, a, r, None) for q_h, a, r in zip(q_hs, accs, runs)]
            accs = [a for a, _ in st]
            runs = [r for _, r in st]
            return j - 1, all_zero_from_here(runs), accs, runs

        if with_left:
            _, _, accs, _ = lax.while_loop(cond, body, (qi - 2, all_zero_from_here(runs), accs, runs))
        result = accs[0]
        for m, a in zip(in_head[1:], accs[1:]):
            result = jnp.where(m, a, result)
        o_ref[pl.ds(q0, T), :] = result.astype(o_ref.dtype)

    q_block(0, False)

    def later_block(qi, carry):
        q_block(qi, True)
        return carry

    lax.fori_loop(1, n_q, later_block, 0)


def _sb_attention(q, kt, v, batch, seq, head_dim):
    T, sb_dim = q.shape
    assert LANES % head_dim == 0 and seq % ATT_TILE == 0
    n_p = sb_dim // LANES
    return pl.pallas_call(
        functools.partial(_sb_attention_kernel, head_dim=head_dim),
        grid=(batch, n_p),
        in_specs=[
            pl.BlockSpec((seq, LANES), lambda b, p: (b, p)),
            pl.BlockSpec((1, LANES, seq), lambda b, p: (b, p, 0)),
            pl.BlockSpec((seq, LANES), lambda b, p: (b, p)),
        ],
        out_specs=pl.BlockSpec((seq, LANES), lambda b, p: (b, p)),
        out_shape=jax.ShapeDtypeStruct((T, sb_dim), BF16),
        compiler_params=pltpu.CompilerParams(
            dimension_semantics=("arbitrary", "arbitrary"), vmem_limit_bytes=VMEM_LIMIT),
        name="sb_attention",
    )(q, kt, v)


def _out_proj_kernel(o_ref, w_ref, h_ref, out_ref):
    out_ref[...] = h_ref[...] + _dot(o_ref[...], w_ref[...])


def _out_proj(o, w_out, h):
    T, D = h.shape
    R = PROJ_ROWS
    return pl.pallas_call(
        _out_proj_kernel,
        grid=(T // R,),
        in_specs=[
            pl.BlockSpec((R, o.shape[1]), lambda i: (i, 0)),
            pl.BlockSpec(w_out.shape, lambda i: (0, 0)),
            pl.BlockSpec((R, D), lambda i: (i, 0)),
        ],
        out_specs=pl.BlockSpec((R, D), lambda i: (i, 0)),
        out_shape=jax.ShapeDtypeStruct((T, D), F32),
        compiler_params=pltpu.CompilerParams(dimension_semantics=("arbitrary",), vmem_limit_bytes=VMEM_LIMIT),
        name="out_proj",
    )(o, w_out.astype(BF16), h)


def kernel(x, a_norm, a_w_in, a_w_gk2, a_b_gk2, a_o_norm, a_w_out, kv_norm, w_kv, b_norm, b_w_q, b_w_out,
           m_norm, m_w_group, m_b_group, m_w_expert, m_b_expert, m_w_gate, m_w_up, m_w_down, final_norm):
    B, S, D = x.shape
    assert a_norm.shape[0] == 1 and b_norm.shape[0] == 1 and m_norm.shape[0] == 2
    head_dim = b_w_q.shape[2] // SB_HEADS
    h = x.reshape(B * S, D)

    def moe(h, layer, out_norm, final):
        return _hier_moe(h, m_norm[layer], m_w_group[layer], m_b_group[layer], m_w_expert[layer],
                         m_b_expert[layer], m_w_gate[layer], m_w_up[layer], m_w_down[layer], out_norm, final)

    h = _gla_layer(h, B, S, a_norm[0], a_w_in[0], a_w_gk2[0], a_b_gk2[0], a_o_norm[0], a_w_out[0])
    h = moe(h, 0, final_norm, False)

    q, kt, v = _qkv_proj(h, B, S, b_norm[0], kv_norm, b_w_q[0], w_kv, math.log2(math.e) / math.sqrt(head_dim))
    o = _sb_attention(q, kt, v, B, S, head_dim)
    h = _out_proj(o, b_w_out[0], h)
    h = moe(h, 1, final_norm, True)
    return h.reshape(B, S, D)
```

```python
import functools
import math

import jax
import jax.numpy as jnp
from jax import lax
from jax.experimental import pallas as pl
from jax.experimental.pallas import tpu as pltpu

RMS_EPS = 1e-6

GLA_HEADS = 4
GLA_CHUNK = 64
CHUNK_SHIFT = GLA_CHUNK.bit_length() - 1
GATE_NORMALIZER = 16.0
SB_HEADS = 16
N_GROUPS = 4
EXPERTS_PER_GROUP = 4
N_EXPERTS = N_GROUPS * EXPERTS_PER_GROUP
PAIRS_PER_GROUP = EXPERTS_PER_GROUP * (EXPERTS_PER_GROUP - 1) // 2
N_CLASSES = N_GROUPS * PAIRS_PER_GROUP

LANES = 128
SUBLANES = 8
ROUTE_ROWS = 32
EXPERT_ROW0 = 8
VMEM_LIMIT = 56 * 1024 * 1024

GLA_ROWS = 256
PROJ_ROWS = 512
ATT_TILE = 256
ATT_ZERO_MARGIN = 160.0
MOE_ROWS = 512
MOE_TM = 256

BF16 = jnp.bfloat16
F32 = jnp.float32


def _dot(a, b):
    return jnp.dot(a, b, preferred_element_type=F32)


def _dot_nt(a, b):
    return lax.dot_general(a, b, (((1,), (1,)), ((), ())), preferred_element_type=F32)


def _split_dot(m01, x, left):
    hi = x.astype(BF16)
    lo = (x - hi.astype(F32)).astype(BF16)
    if left:
        return _dot(m01, hi) + _dot(m01, lo)
    return _dot(hi, m01) + _dot(lo, m01)


def _rms_scale(x):
    return lax.rsqrt(jnp.mean(x * x, axis=-1, keepdims=True) + RMS_EPS)


def _log_sigmoid(x):
    return jnp.minimum(x, 0.0) - jnp.log(1.0 + jnp.exp(-jnp.abs(x)))


def _softplus(x):
    return jnp.maximum(x, 0.0) + jnp.log(1.0 + jnp.exp(-jnp.abs(x)))


def _sigmoid(x):
    return 1.0 / (1.0 + jnp.exp(-x))


def _gla_kernel(h_ref, nrm_ref, wrow_ref, wkt_ref, wlrt_ref, wgk2_ref, bgk2_ref, wgk2t_ref, bgk2c_ref,
                onorm_ref, wout_ref, out_ref, state_ref, *, dk, dv, qk_dim, v_dim, log_scale):
    R = h_ref.shape[0]
    n_chunks = R // GLA_CHUNK

    @pl.when(pl.program_id(1) == 0)
    def _():
        state_ref[...] = jnp.zeros_like(state_ref)

    h = h_ref[...]
    u = (h * _rms_scale(h) * nrm_ref[...]).astype(BF16)

    proj = _dot(u, wrow_ref[...])
    q = proj[:, :qk_dim]
    v = proj[:, qk_dim:qk_dim + v_dim].astype(BF16)
    g = proj[:, qk_dim + v_dim:qk_dim + 2 * v_dim]
    lr = proj[:, qk_dim + 2 * v_dim:].astype(BF16)
    kt = _dot_nt(wkt_ref[...], u)
    lrt = _dot_nt(wlrt_ref[...], u).astype(BF16)

    gk = _log_sigmoid(_dot(lr, wgk2_ref[...]) + bgk2_ref[...]) * (1.0 / GATE_NORMALIZER)
    gkt = _log_sigmoid(_dot(wgk2t_ref[...], lrt) + bgk2c_ref[...]) * (1.0 / GATE_NORMALIZER)

    row = lax.broadcasted_iota(jnp.int32, (R, R), 0)
    col = lax.broadcasted_iota(jnp.int32, (R, R), 1)
    same_chunk = (row >> CHUNK_SHIFT) == (col >> CHUNK_SHIFT)
    causal = same_chunk & (col <= row)
    lbd = jnp.where(causal, 1.0, 0.0).astype(BF16)
    ubd = jnp.where(same_chunk & (row <= col), 1.0, 0.0).astype(BF16)
    after = jnp.where(same_chunk & (row > col), 1.0, 0.0).astype(BF16)

    b = _split_dot(lbd, gk, left=True)
    bt = _split_dot(ubd, gkt, left=False)
    tail_t = _split_dot(after, gkt, left=False)

    q_dec = (q * jnp.exp(b + log_scale)).astype(BF16)
    k_inv_t = (kt * jnp.exp(-bt)).astype(BF16)
    k_end_t = (kt * jnp.exp(tail_t)).astype(BF16)
    chunk_decay_t = jnp.exp(bt + tail_t)

    lane_chunk = lax.broadcasted_iota(jnp.int32, (dk, R), 1) >> CHUNK_SHIFT

    acc = h
    for hd in range(GLA_HEADS):
        ks = slice(hd * dk, (hd + 1) * dk)
        vs = slice(hd * dv, (hd + 1) * dv)
        qd_h = q_dec[:, ks]
        v_h = v[:, vs]
        att = _dot(qd_h, k_inv_t[ks, :])
        att = jnp.where(causal, att, 0.0).astype(BF16)
        o_h = _dot(att, v_h)
        kend_h = k_end_t[ks, :]
        state = state_ref[hd]
        inter = []
        for c in range(n_chunks):
            rows = slice(c * GLA_CHUNK, (c + 1) * GLA_CHUNK)
            inter.append(_dot(qd_h[rows], state.astype(BF16)))
            kend_c = jnp.where(lane_chunk == c, kend_h, jnp.zeros_like(kend_h))
            decay = chunk_decay_t[ks, c * GLA_CHUNK:c * GLA_CHUNK + 1]
            state = decay * state + _dot(kend_c, v_h)
        state_ref[hd] = state
        o_h = o_h + jnp.concatenate(inter, axis=0)
        o_h = o_h * _rms_scale(o_h) * onorm_ref[...]
        g_h = g[:, vs]
        o_h = o_h * (g_h * _sigmoid(g_h))
        acc = acc + _dot(o_h.astype(BF16), wout_ref[vs, :])
    out_ref[...] = acc


def _gla_layer(h, batch, seq, a_norm, w_in, w_gk2, b_gk2, o_norm, w_out):
    T, D = h.shape
    rank, qk_dim = w_gk2.shape
    v_dim = w_out.shape[0]
    dk = qk_dim // GLA_HEADS
    dv = v_dim // GLA_HEADS
    R = GLA_ROWS
    assert seq % R == 0 and R % GLA_CHUNK == 0 and rank <= LANES
    assert w_in.shape[1] == 2 * qk_dim + 2 * v_dim + rank
    n_s = seq // R

    o_q, o_k, o_v, o_lr, o_g = 0, qk_dim, 2 * qk_dim, 2 * qk_dim + v_dim, 2 * qk_dim + v_dim + rank
    w_lr = jnp.pad(w_in[:, o_lr:o_lr + rank], ((0, 0), (0, LANES - rank)))
    w_row = jnp.concatenate([w_in[:, o_q:o_k], w_in[:, o_v:o_lr], w_in[:, o_g:], w_lr], axis=1).astype(BF16)
    w_kt = w_in[:, o_k:o_v].T.astype(BF16)
    w_lrt = w_lr.T.astype(BF16)
    w_gk2p = jnp.pad(w_gk2, ((0, LANES - rank), (0, 0))).astype(BF16)
    w_gk2t = w_gk2p.T
    const = lambda b, s: (0, 0)
    kern = functools.partial(_gla_kernel, dk=dk, dv=dv, qk_dim=qk_dim, v_dim=v_dim,
                             log_scale=math.log(dk ** -0.5))
    return pl.pallas_call(
        kern,
        grid=(batch, n_s),
        in_specs=[
            pl.BlockSpec((R, D), lambda b, s: (b * n_s + s, 0)),
            pl.BlockSpec((1, D), const),
            pl.BlockSpec(w_row.shape, const),
            pl.BlockSpec(w_kt.shape, const),
            pl.BlockSpec(w_lrt.shape, const),
            pl.BlockSpec(w_gk2p.shape, const),
            pl.BlockSpec((1, qk_dim), const),
            pl.BlockSpec(w_gk2t.shape, const),
            pl.BlockSpec((qk_dim, 1), const),
            pl.BlockSpec((1, dv), const),
            pl.BlockSpec((v_dim, D), const),
        ],
        out_specs=pl.BlockSpec((R, D), lambda b, s: (b * n_s + s, 0)),
        out_shape=jax.ShapeDtypeStruct((T, D), F32),
        scratch_shapes=[pltpu.VMEM((GLA_HEADS, dk, dv), F32)],
        compiler_params=pltpu.CompilerParams(
            dimension_semantics=("arbitrary", "arbitrary"), vmem_limit_bytes=VMEM_LIMIT),
        name="gla_layer",
    )(h, a_norm.reshape(1, D), w_row, w_kt, w_lrt, w_gk2p, b_gk2.reshape(1, qk_dim), w_gk2t,
      b_gk2.reshape(qk_dim, 1), o_norm.reshape(1, dv), w_out.astype(BF16))


def _router_kernel(h_ref, nrm_ref, wt_ref, bt_ref, upper_ref, idx_ref, wgt_ref, cnt_ref, carry_ref):
    R = h_ref.shape[0]

    @pl.when(pl.program_id(0) == 0)
    def _():
        carry_ref[...] = jnp.zeros_like(carry_ref)

    h = h_ref[...]
    t = h * _rms_scale(h) * nrm_ref[...]
    t_hi = t.astype(BF16)
    t_lo = (t - t_hi.astype(F32)).astype(BF16)
    p = _dot_nt(wt_ref[...], t_hi)
    logits = p[:ROUTE_ROWS] + p[ROUTE_ROWS:] + _dot_nt(wt_ref[:ROUTE_ROWS], t_lo) + bt_ref[:, 0:1]
    row = lax.broadcasted_iota(jnp.int32, (ROUTE_ROWS, R), 0)
    neg = jnp.float32(-jnp.inf)

    def first_argmax(vals):
        m = jnp.max(vals, axis=0, keepdims=True)
        i = jnp.min(jnp.where(vals == m, row, ROUTE_ROWS), axis=0, keepdims=True)
        return m, i

    gl = jnp.where(row < N_GROUPS, logits, neg)
    g_max, g_idx = first_argmax(gl)
    g_w = 1.0 / jnp.sum(jnp.exp(gl - g_max), axis=0, keepdims=True)

    base = EXPERT_ROW0 + g_idx * EXPERTS_PER_GROUP
    el = jnp.where((row >= base) & (row < base + EXPERTS_PER_GROUP), logits, neg)
    m1, i1 = first_argmax(el)
    m2, i2 = first_argmax(jnp.where(row == i1, neg, el))
    r = jnp.exp(m2 - m1)
    w1 = g_w / (1.0 + r)
    w2 = g_w * r / (1.0 + r)
    l1 = i1 - base
    l2 = i2 - base
    first_is_lo = l1 < l2
    lo = jnp.minimum(l1, l2)
    hi = jnp.maximum(l1, l2)
    cls = g_idx * PAIRS_PER_GROUP + ((lo * (2 * EXPERTS_PER_GROUP - 1 - lo)) >> 1) + (hi - lo - 1)
    w_lo = jnp.where(first_is_lo, w1, w2)
    w_hi = jnp.where(first_is_lo, w2, w1)

    onehot = jnp.where(row == cls, 1.0, 0.0)
    before = _dot(onehot.astype(BF16), upper_ref[...]) + carry_ref[:, 0:1]
    rank = jnp.sum(jnp.where(row == cls, before, 0.0), axis=0, keepdims=True)
    total = carry_ref[:, 0:1] + jnp.sum(onehot, axis=1, keepdims=True)
    carry_ref[...] = jnp.broadcast_to(total, carry_ref.shape)
    cnt_ref[...] = jnp.broadcast_to(total, cnt_ref.shape).astype(jnp.int32)

    row8 = lax.broadcasted_iota(jnp.int32, (SUBLANES, R), 0)
    idx_ref[0] = jnp.where(row8 == 0, cls, jnp.where(row8 == 1, rank.astype(jnp.int32), 0))
    w_rows = jnp.where(row == 0, w_lo, jnp.where(row == 1, w_hi, 0.0))
    w_rows = jnp.concatenate([w_rows, jnp.zeros((LANES - ROUTE_ROWS, R), F32)], axis=0)
    wgt_ref[...] = w_rows.T


def _router(h, m_norm, w_group, b_group, w_expert, b_expert):
    T, D = h.shape
    R = MOE_ROWS
    assert T % R == 0 and N_CLASSES <= ROUTE_ROWS
    pad_g = EXPERT_ROW0 - N_GROUPS
    pad_e = ROUTE_ROWS - EXPERT_ROW0 - N_EXPERTS
    w_t = jnp.pad(jnp.concatenate([w_group.T, jnp.zeros((pad_g, D), F32), w_expert.T], axis=0), ((0, pad_e), (0, 0)))
    w_t_hi = w_t.astype(BF16)
    w_t = jnp.concatenate([w_t_hi, (w_t - w_t_hi.astype(F32)).astype(BF16)], axis=0)
    b_t = jnp.pad(jnp.concatenate([b_group, jnp.zeros((pad_g,), F32), b_expert]), (0, pad_e))
    b_t = jnp.broadcast_to(b_t[:, None], (ROUTE_ROWS, LANES))
    idx = jnp.arange(R)
    upper = (idx[:, None] < idx[None, :]).astype(BF16)
    const = lambda i: (0, 0)
    return pl.pallas_call(
        _router_kernel,
        grid=(T // R,),
        in_specs=[
            pl.BlockSpec((R, D), lambda i: (i, 0)),
            pl.BlockSpec((1, D), const),
            pl.BlockSpec((2 * ROUTE_ROWS, D), const),
            pl.BlockSpec((ROUTE_ROWS, LANES), const),
            pl.BlockSpec((R, R), const),
        ],
        out_specs=[
            pl.BlockSpec((1, SUBLANES, R), lambda i: (i, 0, 0)),
            pl.BlockSpec((R, LANES), lambda i: (i, 0)),
            pl.BlockSpec((ROUTE_ROWS, LANES), const),
        ],
        out_shape=[
            jax.ShapeDtypeStruct((T // R, SUBLANES, R), jnp.int32),
            jax.ShapeDtypeStruct((T, LANES), F32),
            jax.ShapeDtypeStruct((ROUTE_ROWS, LANES), jnp.int32),
        ],
        scratch_shapes=[pltpu.VMEM((ROUTE_ROWS, LANES), F32)],
        compiler_params=pltpu.CompilerParams(dimension_semantics=("arbitrary",), vmem_limit_bytes=VMEM_LIMIT),
        name="moe_router",
    )(h, m_norm.reshape(1, D), w_t, b_t, upper)


def _dispatch_kernel(pos_ref, pos_prev_ref, h_ref, wgt_ref, xs_in_ref, xs_ref, row_ref, sem):
    del xs_in_ref
    i = pl.program_id(0)
    n = pl.num_programs(0)
    R, D = h_ref.shape
    G = R // SUBLANES
    slot = i % 2
    row_ref[slot, :, :, :D] = h_ref[...].reshape(G, SUBLANES, D)
    row_ref[slot, :, :, D:] = wgt_ref[...].reshape(G, SUBLANES, LANES)

    def row_copy(p_ref, s, g, j):
        dst = p_ref[0, 0, g * SUBLANES + j]
        return pltpu.make_async_copy(row_ref.at[s, g, pl.ds(j, 1)], xs_ref.at[pl.ds(dst, 1)], sem.at[s])

    def issue(g, c):
        for j in range(SUBLANES):
            row_copy(pos_ref, slot, g, j).start()
        return c

    lax.fori_loop(0, G, issue, 0)

    def drain(p_ref, s):
        def body(g, c):
            for j in range(SUBLANES):
                row_copy(p_ref, s, g, j).wait()
            return c
        lax.fori_loop(0, G, body, 0)

    @pl.when(i > 0)
    def _():
        drain(pos_prev_ref, 1 - slot)

    @pl.when(i == n - 1)
    def _():
        drain(pos_ref, slot)


def _dispatch(h, wgt, pos, n_rows):
    T, D = h.shape
    R = MOE_ROWS
    n_t = T // R
    xs0 = jnp.zeros((n_rows, D + LANES), F32)
    pos3 = pos.reshape(n_t, 1, R)
    return pl.pallas_call(
        _dispatch_kernel,
        grid=(n_t,),
        in_specs=[
            pl.BlockSpec((1, 1, R), lambda i: (i, 0, 0), memory_space=pltpu.SMEM),
            pl.BlockSpec((1, 1, R), lambda i: (jnp.maximum(i - 1, 0), 0, 0), memory_space=pltpu.SMEM),
            pl.BlockSpec((R, D), lambda i: (i, 0)),
            pl.BlockSpec((R, LANES), lambda i: (i, 0)),
            pl.BlockSpec(memory_space=pl.ANY),
        ],
        out_specs=pl.BlockSpec(memory_space=pl.ANY),
        out_shape=jax.ShapeDtypeStruct((n_rows, D + LANES), F32),
        scratch_shapes=[pltpu.VMEM((2, R // SUBLANES, SUBLANES, D + LANES), F32), pltpu.SemaphoreType.DMA((2,))],
        input_output_aliases={4: 0},
        compiler_params=pltpu.CompilerParams(dimension_semantics=("arbitrary",), vmem_limit_bytes=VMEM_LIMIT),
        name="moe_dispatch",
    )(pos3, pos3, h, wgt, xs0)


def _experts_kernel(used_ref, lo_ref, hi_ref, xs_ref, nrm_ref, wgu_lo_ref, wd_lo_ref, wgu_hi_ref, wd_hi_ref,
                    ys_ref, *, d_model, d_expert):
    del lo_ref, hi_ref
    active = pl.program_id(0) < used_ref[0]

    @pl.when(active)
    def _():
        x = xs_ref[:, :d_model]
        t = (x * _rms_scale(x) * nrm_ref[...]).astype(BF16)
        y = None
        for k, (wgu_ref, wd_ref) in enumerate(((wgu_lo_ref, wd_lo_ref), (wgu_hi_ref, wd_hi_ref))):
            gu = _dot(t, wgu_ref[0])
            gate = gu[:, :d_expert]
            hdn = (gate * _sigmoid(gate) * gu[:, d_expert:]).astype(BF16)
            y_k = xs_ref[:, d_model + k:d_model + k + 1] * _dot(hdn, wd_ref[0])
            y = y_k if y is None else y + y_k
        ys_ref[...] = y

    @pl.when(jnp.logical_not(active))
    def _():
        ys_ref[...] = jnp.zeros_like(ys_ref)


def _experts(xs, n_used, tile_lo, tile_hi, m_norm, w_gate, w_up, w_down):
    n_rows = xs.shape[0]
    D = m_norm.shape[0]
    d_expert = w_gate.shape[-1]
    TM = MOE_TM
    n_tiles = n_rows // TM
    w_gu = jnp.concatenate([w_gate, w_up], axis=-1).astype(BF16)
    w_d = w_down.astype(BF16)
    grid_spec = pltpu.PrefetchScalarGridSpec(
        num_scalar_prefetch=3,
        grid=(n_tiles,),
        in_specs=[
            pl.BlockSpec((TM, D + LANES), lambda i, nu, lo, hi: (i, 0)),
            pl.BlockSpec((1, D), lambda i, nu, lo, hi: (0, 0)),
            pl.BlockSpec((1, D, 2 * d_expert), lambda i, nu, lo, hi: (lo[i], 0, 0)),
            pl.BlockSpec((1, d_expert, D), lambda i, nu, lo, hi: (lo[i], 0, 0)),
            pl.BlockSpec((1, D, 2 * d_expert), lambda i, nu, lo, hi: (hi[i], 0, 0)),
            pl.BlockSpec((1, d_expert, D), lambda i, nu, lo, hi: (hi[i], 0, 0)),
        ],
        out_specs=pl.BlockSpec((TM, D), lambda i, nu, lo, hi: (i, 0)),
    )
    return pl.pallas_call(
        functools.partial(_experts_kernel, d_model=D, d_expert=d_expert),
        grid_spec=grid_spec,
        out_shape=jax.ShapeDtypeStruct((n_rows, D), F32),
        compiler_params=pltpu.CompilerParams(dimension_semantics=("arbitrary",), vmem_limit_bytes=VMEM_LIMIT),
        name="moe_experts",
    )(n_used, tile_lo, tile_hi, xs, m_norm.reshape(1, D), w_gu, w_d, w_gu, w_d)


def _combine_kernel(pos_ref, pos_next_ref, h_ref, nrm_ref, ys_ref, out_ref, buf_ref, sem, *, final_norm):
    i = pl.program_id(0)
    n = pl.num_programs(0)
    R, D = h_ref.shape
    G = R // SUBLANES
    slot = i % 2

    def row_copy(p_ref, s, g, j):
        src = p_ref[0, 0, g * SUBLANES + j]
        return pltpu.make_async_copy(ys_ref.at[pl.ds(src, 1)], buf_ref.at[s, g, pl.ds(j, 1)], sem.at[s])

    def issue(p_ref, s):
        def body(g, c):
            for j in range(SUBLANES):
                row_copy(p_ref, s, g, j).start()
            return c
        lax.fori_loop(0, G, body, 0)

    @pl.when(i == 0)
    def _():
        issue(pos_ref, slot)

    @pl.when(i < n - 1)
    def _():
        issue(pos_next_ref, 1 - slot)

    def drain(g, c):
        for j in range(SUBLANES):
            row_copy(pos_ref, slot, g, j).wait()
        return c

    lax.fori_loop(0, G, drain, 0)

    out = h_ref[...] + buf_ref[slot].reshape(R, D)
    if final_norm:
        out = out * _rms_scale(out) * nrm_ref[...]
    out_ref[...] = out


def _combine(h, pos, ys, norm, final_norm):
    T, D = h.shape
    R = MOE_ROWS
    n_t = T // R
    pos3 = pos.reshape(n_t, 1, R)
    return pl.pallas_call(
        functools.partial(_combine_kernel, final_norm=final_norm),
        grid=(n_t,),
        in_specs=[
            pl.BlockSpec((1, 1, R), lambda i: (i, 0, 0), memory_space=pltpu.SMEM),
            pl.BlockSpec((1, 1, R), lambda i: (jnp.minimum(i + 1, n_t - 1), 0, 0), memory_space=pltpu.SMEM),
            pl.BlockSpec((R, D), lambda i: (i, 0)),
            pl.BlockSpec((1, D), lambda i: (0, 0)),
            pl.BlockSpec(memory_space=pl.ANY),
        ],
        out_specs=pl.BlockSpec((R, D), lambda i: (i, 0)),
        out_shape=jax.ShapeDtypeStruct((T, D), F32),
        scratch_shapes=[pltpu.VMEM((2, R // SUBLANES, SUBLANES, D), F32), pltpu.SemaphoreType.DMA((2,))],
        compiler_params=pltpu.CompilerParams(dimension_semantics=("arbitrary",), vmem_limit_bytes=VMEM_LIMIT),
        name="moe_combine",
    )(pos3, pos3, h, norm.reshape(1, D), ys)


def _class_experts():
    lo, hi = [], []
    for g in range(N_GROUPS):
        for a in range(EXPERTS_PER_GROUP):
            for b in range(a + 1, EXPERTS_PER_GROUP):
                lo.append(g * EXPERTS_PER_GROUP + a)
                hi.append(g * EXPERTS_PER_GROUP + b)
    return jnp.asarray(lo, jnp.int32), jnp.asarray(hi, jnp.int32)


def _hier_moe(h, m_norm, w_group, b_group, w_expert, b_expert, w_gate, w_up, w_down, out_norm, final_norm):
    T, D = h.shape
    TM = MOE_TM
    idx, wgt, cnt = _router(h, m_norm, w_group, b_group, w_expert, b_expert)

    counts = cnt[:N_CLASSES, 0]
    padded = ((counts + TM - 1) // TM) * TM
    ends = jnp.cumsum(padded)
    starts = ends - padded
    n_rows = T + N_CLASSES * TM
    n_tiles = n_rows // TM
    tile_start = jnp.arange(n_tiles, dtype=jnp.int32) * TM
    tile_class = jnp.minimum(
        jnp.sum((ends[None, :] <= tile_start[:, None]).astype(jnp.int32), axis=1), N_CLASSES - 1)
    class_lo, class_hi = _class_experts()
    n_used = (ends[-1] // TM).astype(jnp.int32).reshape(1)
    pos = (starts[idx[:, 0, :]] + idx[:, 1, :]).astype(jnp.int32)

    xs = _dispatch(h, wgt, pos, n_rows)
    ys = _experts(xs, n_used, class_lo[tile_class], class_hi[tile_class], m_norm, w_gate, w_up, w_down)
    return _combine(h, pos, ys, out_norm, final_norm)


def _qkv_kernel(h_ref, qn_ref, kvn_ref, wq_ref, wkt_ref, wv_ref, q_ref, kt_ref, v_ref):
    h = h_ref[...]
    xhat = h * _rms_scale(h)
    uq = (xhat * qn_ref[...]).astype(BF16)
    ukv = (xhat * kvn_ref[...]).astype(BF16)
    q_ref[...] = _dot(uq, wq_ref[...]).astype(BF16)
    v_ref[...] = _dot(ukv, wv_ref[...]).astype(BF16)
    kt_ref[0] = _dot_nt(wkt_ref[...], ukv).astype(BF16)


def _qkv_proj(h, batch, seq, q_norm, kv_norm, w_q, w_kv, scale):
    T, D = h.shape
    sb_dim = w_q.shape[1]
    R = PROJ_ROWS
    assert seq % R == 0
    n_s = seq // R
    w_qs = (w_q * scale).astype(BF16)
    w_kt = w_kv[:, :sb_dim].T.astype(BF16)
    w_v = w_kv[:, sb_dim:].astype(BF16)
    const = lambda b, s: (0, 0)
    return pl.pallas_call(
        _qkv_kernel,
        grid=(batch, n_s),
        in_specs=[
            pl.BlockSpec((R, D), lambda b, s: (b * n_s + s, 0)),
            pl.BlockSpec((1, D), const),
            pl.BlockSpec((1, D), const),
            pl.BlockSpec((D, sb_dim), const),
            pl.BlockSpec((sb_dim, D), const),
            pl.BlockSpec((D, sb_dim), const),
        ],
        out_specs=[
            pl.BlockSpec((R, sb_dim), lambda b, s: (b * n_s + s, 0)),
            pl.BlockSpec((1, sb_dim, R), lambda b, s: (b, 0, s)),
            pl.BlockSpec((R, sb_dim), lambda b, s: (b * n_s + s, 0)),
        ],
        out_shape=[
            jax.ShapeDtypeStruct((T, sb_dim), BF16),
            jax.ShapeDtypeStruct((batch, sb_dim, seq), BF16),
            jax.ShapeDtypeStruct((T, sb_dim), BF16),
        ],
        compiler_params=pltpu.CompilerParams(
            dimension_semantics=("arbitrary", "arbitrary"), vmem_limit_bytes=VMEM_LIMIT),
        name="qkv_proj",
    )(h, q_norm.reshape(1, D), kv_norm.reshape(1, D), w_qs, w_kt, w_v)


def _sb_attention_kernel(q_ref, kt_ref, v_ref, o_ref, *, head_dim):
    S = q_ref.shape[0]
    T = ATT_TILE
    n_q = S // T
    n_heads = LANES // head_dim
    lane_q = lax.broadcasted_iota(jnp.int32, (T, LANES), 1)
    trow = lax.broadcasted_iota(jnp.int32, (T, T), 0)
    scol = lax.broadcasted_iota(jnp.int32, (T, T), 1)
    strictly_before = scol < trow
    suffix = jnp.where(trow >= scol, 1.0, 0.0).astype(BF16)
    in_head = [(lane_q >= hd * head_dim) & (lane_q < (hd + 1) * head_dim) for hd in range(n_heads)]

    ksq = jnp.square(kt_ref[0].astype(F32))
    k_max = [jnp.sqrt(jnp.max(jnp.sum(ksq[hd * head_dim:(hd + 1) * head_dim], axis=0, keepdims=True),
                              axis=1, keepdims=True)) for hd in range(n_heads)]

    def tile(q_h, k0, acc, run, mask):
        z = _dot(q_h, kt_ref[0, :, pl.ds(k0, T)])
        sp = jnp.maximum(z, 0.0) + jnp.log2(1.0 + jnp.exp2(-jnp.abs(z)))
        if mask is not None:
            sp = jnp.where(mask, sp, 0.0)
        within = _dot(sp.astype(BF16), suffix)
        a = jnp.exp2(z - within - run)
        if mask is not None:
            a = jnp.where(mask, a, 0.0)
        acc = acc + _dot(a.astype(BF16), v_ref[pl.ds(k0, T), :])
        return acc, run + within[:, 0:1]

    def q_block(qi, with_left):
        q0 = qi * T if isinstance(qi, int) else pl.multiple_of(qi * T, T)
        q_pair = q_ref[pl.ds(q0, T), :]
        qsq = jnp.square(q_pair.astype(F32))
        q_hs = [jnp.where(m, q_pair, jnp.zeros_like(q_pair)) for m in in_head]
        z_bound = [jnp.sqrt(jnp.sum(jnp.where(m, qsq, 0.0), axis=-1, keepdims=True)) * km
                   for m, km in zip(in_head, k_max)]

        def all_zero_from_here(runs):
            slack = runs[0] - z_bound[0]
            for r, zb in zip(runs[1:], z_bound[1:]):
                slack = jnp.minimum(slack, r - zb)
            return (jnp.min(slack) > ATT_ZERO_MARGIN).astype(jnp.int32)

        st = [tile(q_h, q0, jnp.zeros((T, LANES), F32), jnp.zeros((T, 1), F32), strictly_before) for q_h in q_hs]
        if with_left:
            st = [tile(q_h, pl.multiple_of(q0 - T, T), a, r, None) for q_h, (a, r) in zip(q_hs, st)]
        accs = [a for a, _ in st]
        runs = [r for _, r in st]

        def cond(c):
            return (c[0] >= 0) & (c[1] == 0)

        def body(c):
            j, _, accs, runs = c
            k0 = pl.multiple_of(j * T, T)
            st = [tile(q_h, k0, a, r, None) for q_h, a, r in zip(q_hs, accs, runs)]
            accs = [a for a, _ in st]
            runs = [r for _, r in st]
            return j - 1, all_zero_from_here(runs), accs, runs

        if with_left:
            _, _, accs, _ = lax.while_loop(cond, body, (qi - 2, all_zero_from_here(runs), accs, runs))
        result = accs[0]
        for m, a in zip(in_head[1:], accs[1:]):
            result = jnp.where(m, a, result)
        o_ref[pl.ds(q0, T), :] = result.astype(o_ref.dtype)

    q_block(0, False)

    def later_block(qi, carry):
        q_block(qi, True)
        return carry

    lax.fori_loop(1, n_q, later_block, 0)


def _sb_attention(q, kt, v, batch, seq, head_dim):
    T, sb_dim = q.shape
    assert LANES % head_dim == 0 and seq % ATT_TILE == 0
    n_p = sb_dim // LANES
    return pl.pallas_call(
        functools.partial(_sb_attention_kernel, head_dim=head_dim),
        grid=(batch, n_p),
        in_specs=[
            pl.BlockSpec((seq, LANES), lambda b, p: (b, p)),
            pl.BlockSpec((1, LANES, seq), lambda b, p: (b, p, 0)),
            pl.BlockSpec((seq, LANES), lambda b, p: (b, p)),
        ],
        out_specs=pl.BlockSpec((seq, LANES), lambda b, p: (b, p)),
        out_shape=jax.ShapeDtypeStruct((T, sb_dim), BF16),
        compiler_params=pltpu.CompilerParams(
            dimension_semantics=("arbitrary", "arbitrary"), vmem_limit_bytes=VMEM_LIMIT),
        name="sb_attention",
    )(q, kt, v)


def _out_proj_kernel(o_ref, w_ref, h_ref, out_ref):
    out_ref[...] = h_ref[...] + _dot(o_ref[...], w_ref[...])


def _out_proj(o, w_out, h):
    T, D = h.shape
    R = PROJ_ROWS
    return pl.pallas_call(
        _out_proj_kernel,
        grid=(T // R,),
        in_specs=[
            pl.BlockSpec((R, o.shape[1]), lambda i: (i, 0)),
            pl.BlockSpec(w_out.shape, lambda i: (0, 0)),
            pl.BlockSpec((R, D), lambda i: (i, 0)),
        ],
        out_specs=pl.BlockSpec((R, D), lambda i: (i, 0)),
        out_shape=jax.ShapeDtypeStruct((T, D), F32),
        compiler_params=pltpu.CompilerParams(dimension_semantics=("arbitrary",), vmem_limit_bytes=VMEM_LIMIT),
        name="out_proj",
    )(o, w_out.astype(BF16), h)


def kernel(x, a_norm, a_w_in, a_w_gk2, a_b_gk2, a_o_norm, a_w_out, kv_norm, w_kv, b_norm, b_w_q, b_w_out,
           m_norm, m_w_group, m_b_group, m_w_expert, m_b_expert, m_w_gate, m_w_up, m_w_down, final_norm):
    B, S, D = x.shape
    assert a_norm.shape[0] == 1 and b_norm.shape[0] == 1 and m_norm.shape[0] == 2
    head_dim = b_w_q.shape[2] // SB_HEADS
    h = x.reshape(B * S, D)

    def moe(h, layer, out_norm, final):
        return _hier_moe(h, m_norm[layer], m_w_group[layer], m_b_group[layer], m_w_expert[layer],
                         m_b_expert[layer], m_w_gate[layer], m_w_up[layer], m_w_down[layer], out_norm, final)

    h = _gla_layer(h, B, S, a_norm[0], a_w_in[0], a_w_gk2[0], a_b_gk2[0], a_o_norm[0], a_w_out[0])
    h = moe(h, 0, final_norm, False)

    q, kt, v = _qkv_proj(h, B, S, b_norm[0], kv_norm, b_w_q[0], w_kv, math.log2(math.e) / math.sqrt(head_dim))
    o = _sb_attention(q, kt, v, B, S, head_dim)
    h = _out_proj(o, b_w_out[0], h)
    h = moe(h, 1, final_norm, True)
    return h.reshape(B, S, D)
```

```python
import functools
import math

import jax
import jax.numpy as jnp
from jax import lax
from jax.experimental import pallas as pl
from jax.experimental.pallas import tpu as pltpu

RMS_EPS = 1e-6

GLA_HEADS = 4
GLA_CHUNK = 64
CHUNK_SHIFT = GLA_CHUNK.bit_length() - 1
GATE_NORMALIZER = 16.0
SB_HEADS = 16
N_GROUPS = 4
EXPERTS_PER_GROUP = 4
N_EXPERTS = N_GROUPS * EXPERTS_PER_GROUP
PAIRS_PER_GROUP = EXPERTS_PER_GROUP * (EXPERTS_PER_GROUP - 1) // 2
N_CLASSES = N_GROUPS * PAIRS_PER_GROUP

LANES = 128
SUBLANES = 8
ROUTE_ROWS = 32
EXPERT_ROW0 = 8
VMEM_LIMIT = 56 * 1024 * 1024

GLA_ROWS = 512
GLA_BLOCK = 256
PROJ_ROWS = 512
ATT_TILE = 256
ATT_ZERO_MARGIN = 160.0
MOE_ROWS = 512
MOE_TM = 256

BF16 = jnp.bfloat16
F32 = jnp.float32


def _dot(a, b):
    return jnp.dot(a, b, preferred_element_type=F32)


def _dot_nt(a, b):
    return lax.dot_general(a, b, (((1,), (1,)), ((), ())), preferred_element_type=F32)


def _split_dot(m01, x, left):
    hi = x.astype(BF16)
    lo = (x - hi.astype(F32)).astype(BF16)
    if left:
        return _dot(m01, hi) + _dot(m01, lo)
    return _dot(hi, m01) + _dot(lo, m01)


def _rms_scale(x):
    return lax.rsqrt(jnp.mean(x * x, axis=-1, keepdims=True) + RMS_EPS)


def _log_sigmoid(x):
    return jnp.minimum(x, 0.0) - jnp.log(1.0 + jnp.exp(-jnp.abs(x)))


def _softplus(x):
    return jnp.maximum(x, 0.0) + jnp.log(1.0 + jnp.exp(-jnp.abs(x)))


def _sigmoid(x):
    return 1.0 / (1.0 + jnp.exp(-x))


def _gla_block(h, states, nrm_ref, wrow_ref, wkt_ref, wlrt_ref, wgk2_ref, bgk2_ref, wgk2t_ref, bgk2c_ref,
               onorm_ref, wout_ref, *, dk, dv, qk_dim, v_dim, log_scale):
    R = h.shape[0]
    n_chunks = R // GLA_CHUNK
    u = (h * _rms_scale(h) * nrm_ref[...]).astype(BF16)

    proj = _dot(u, wrow_ref[...])
    q = proj[:, :qk_dim]
    v = proj[:, qk_dim:qk_dim + v_dim].astype(BF16)
    g = proj[:, qk_dim + v_dim:qk_dim + 2 * v_dim]
    lr = proj[:, qk_dim + 2 * v_dim:].astype(BF16)
    kt = _dot_nt(wkt_ref[...], u)
    lrt = _dot_nt(wlrt_ref[...], u).astype(BF16)

    gk = _log_sigmoid(_dot(lr, wgk2_ref[...]) + bgk2_ref[...]) * (1.0 / GATE_NORMALIZER)
    gkt = _log_sigmoid(_dot(wgk2t_ref[...], lrt) + bgk2c_ref[...]) * (1.0 / GATE_NORMALIZER)

    row = lax.broadcasted_iota(jnp.int32, (R, R), 0)
    col = lax.broadcasted_iota(jnp.int32, (R, R), 1)
    same_chunk = (row >> CHUNK_SHIFT) == (col >> CHUNK_SHIFT)
    causal = same_chunk & (col <= row)
    lbd = jnp.where(causal, 1.0, 0.0).astype(BF16)
    ubd = jnp.where(same_chunk & (row <= col), 1.0, 0.0).astype(BF16)
    after = jnp.where(same_chunk & (row > col), 1.0, 0.0).astype(BF16)

    b = _split_dot(lbd, gk, left=True)
    bt = _split_dot(ubd, gkt, left=False)
    tail_t = _split_dot(after, gkt, left=False)

    q_dec = (q * jnp.exp(b + log_scale)).astype(BF16)
    k_inv_t = (kt * jnp.exp(-bt)).astype(BF16)
    k_end_t = (kt * jnp.exp(tail_t)).astype(BF16)
    chunk_decay_t = jnp.exp(bt + tail_t)

    lane_chunk = lax.broadcasted_iota(jnp.int32, (dk, R), 1) >> CHUNK_SHIFT

    acc = h
    new_states = []
    for hd in range(GLA_HEADS):
        ks = slice(hd * dk, (hd + 1) * dk)
        vs = slice(hd * dv, (hd + 1) * dv)
        qd_h = q_dec[:, ks]
        v_h = v[:, vs]
        att = _dot(qd_h, k_inv_t[ks, :])
        att = jnp.where(causal, att, 0.0).astype(BF16)
        o_h = _dot(att, v_h)
        kend_h = k_end_t[ks, :]
        state = states[hd]
        inter = []
        for c in range(n_chunks):
            rows = slice(c * GLA_CHUNK, (c + 1) * GLA_CHUNK)
            inter.append(_dot(qd_h[rows], state.astype(BF16)))
            kend_c = jnp.where(lane_chunk == c, kend_h, jnp.zeros_like(kend_h))
            decay = chunk_decay_t[ks, c * GLA_CHUNK:c * GLA_CHUNK + 1]
            state = decay * state + _dot(kend_c, v_h)
        new_states.append(state)
        o_h = o_h + jnp.concatenate(inter, axis=0)
        o_h = o_h * _rms_scale(o_h) * onorm_ref[...]
        g_h = g[:, vs]
        o_h = o_h * (g_h * _sigmoid(g_h))
        acc = acc + _dot(o_h.astype(BF16), wout_ref[vs, :])
    return acc, new_states


def _gla_kernel(h_ref, *refs, **dims):
    *w_refs, out_ref, state_ref = refs

    @pl.when(pl.program_id(1) == 0)
    def _():
        state_ref[...] = jnp.zeros_like(state_ref)

    states = [state_ref[hd] for hd in range(GLA_HEADS)]
    for blk in range(h_ref.shape[0] // GLA_BLOCK):
        rows = slice(blk * GLA_BLOCK, (blk + 1) * GLA_BLOCK)
        out, states = _gla_block(h_ref[rows, :], states, *w_refs, **dims)
        out_ref[rows, :] = out
    for hd in range(GLA_HEADS):
        state_ref[hd] = states[hd]


def _gla_layer(h, batch, seq, a_norm, w_in, w_gk2, b_gk2, o_norm, w_out):
    T, D = h.shape
    rank, qk_dim = w_gk2.shape
    v_dim = w_out.shape[0]
    dk = qk_dim // GLA_HEADS
    dv = v_dim // GLA_HEADS
    R = GLA_ROWS
    assert seq % R == 0 and R % GLA_CHUNK == 0 and rank <= LANES
    assert w_in.shape[1] == 2 * qk_dim + 2 * v_dim + rank
    n_s = seq // R

    o_q, o_k, o_v, o_lr, o_g = 0, qk_dim, 2 * qk_dim, 2 * qk_dim + v_dim, 2 * qk_dim + v_dim + rank
    w_lr = jnp.pad(w_in[:, o_lr:o_lr + rank], ((0, 0), (0, LANES - rank)))
    w_row = jnp.concatenate([w_in[:, o_q:o_k], w_in[:, o_v:o_lr], w_in[:, o_g:], w_lr], axis=1).astype(BF16)
    w_kt = w_in[:, o_k:o_v].T.astype(BF16)
    w_lrt = w_lr.T.astype(BF16)
    w_gk2p = jnp.pad(w_gk2, ((0, LANES - rank), (0, 0))).astype(BF16)
    w_gk2t = w_gk2p.T
    const = lambda b, s: (0, 0)
    kern = functools.partial(_gla_kernel, dk=dk, dv=dv, qk_dim=qk_dim, v_dim=v_dim,
                             log_scale=math.log(dk ** -0.5))
    return pl.pallas_call(
        kern,
        grid=(batch, n_s),
        in_specs=[
            pl.BlockSpec((R, D), lambda b, s: (b * n_s + s, 0)),
            pl.BlockSpec((1, D), const),
            pl.BlockSpec(w_row.shape, const),
            pl.BlockSpec(w_kt.shape, const),
            pl.BlockSpec(w_lrt.shape, const),
            pl.BlockSpec(w_gk2p.shape, const),
            pl.BlockSpec((1, qk_dim), const),
            pl.BlockSpec(w_gk2t.shape, const),
            pl.BlockSpec((qk_dim, 1), const),
            pl.BlockSpec((1, dv), const),
            pl.BlockSpec((v_dim, D), const),
        ],
        out_specs=pl.BlockSpec((R, D), lambda b, s: (b * n_s + s, 0)),
        out_shape=jax.ShapeDtypeStruct((T, D), F32),
        scratch_shapes=[pltpu.VMEM((GLA_HEADS, dk, dv), F32)],
        compiler_params=pltpu.CompilerParams(
            dimension_semantics=("arbitrary", "arbitrary"), vmem_limit_bytes=VMEM_LIMIT),
        name="gla_layer",
    )(h, a_norm.reshape(1, D), w_row, w_kt, w_lrt, w_gk2p, b_gk2.reshape(1, qk_dim), w_gk2t,
      b_gk2.reshape(qk_dim, 1), o_norm.reshape(1, dv), w_out.astype(BF16))


def _router_kernel(h_ref, nrm_ref, wt_ref, bt_ref, upper_ref, idx_ref, wgt_ref, cnt_ref, carry_ref):
    R = h_ref.shape[0]

    @pl.when(pl.program_id(0) == 0)
    def _():
        carry_ref[...] = jnp.zeros_like(carry_ref)

    h = h_ref[...]
    t = h * _rms_scale(h) * nrm_ref[...]
    t_hi = t.astype(BF16)
    t_lo = (t - t_hi.astype(F32)).astype(BF16)
    p = _dot_nt(wt_ref[...], t_hi)
    logits = p[:ROUTE_ROWS] + p[ROUTE_ROWS:] + _dot_nt(wt_ref[:ROUTE_ROWS], t_lo) + bt_ref[:, 0:1]
    row = lax.broadcasted_iota(jnp.int32, (ROUTE_ROWS, R), 0)
    neg = jnp.float32(-jnp.inf)

    def first_argmax(vals):
        m = jnp.max(vals, axis=0, keepdims=True)
        i = jnp.min(jnp.where(vals == m, row, ROUTE_ROWS), axis=0, keepdims=True)
        return m, i

    gl = jnp.where(row < N_GROUPS, logits, neg)
    g_max, g_idx = first_argmax(gl)
    g_w = 1.0 / jnp.sum(jnp.exp(gl - g_max), axis=0, keepdims=True)

    base = EXPERT_ROW0 + g_idx * EXPERTS_PER_GROUP
    el = jnp.where((row >= base) & (row < base + EXPERTS_PER_GROUP), logits, neg)
    m1, i1 = first_argmax(el)
    m2, i2 = first_argmax(jnp.where(row == i1, neg, el))
    r = jnp.exp(m2 - m1)
    w1 = g_w / (1.0 + r)
    w2 = g_w * r / (1.0 + r)
    l1 = i1 - base
    l2 = i2 - base
    first_is_lo = l1 < l2
    lo = jnp.minimum(l1, l2)
    hi = jnp.maximum(l1, l2)
    cls = g_idx * PAIRS_PER_GROUP + ((lo * (2 * EXPERTS_PER_GROUP - 1 - lo)) >> 1) + (hi - lo - 1)
    w_lo = jnp.where(first_is_lo, w1, w2)
    w_hi = jnp.where(first_is_lo, w2, w1)

    onehot = jnp.where(row == cls, 1.0, 0.0)
    before = _dot(onehot.astype(BF16), upper_ref[...]) + carry_ref[:, 0:1]
    rank = jnp.sum(jnp.where(row == cls, before, 0.0), axis=0, keepdims=True)
    total = carry_ref[:, 0:1] + jnp.sum(onehot, axis=1, keepdims=True)
    carry_ref[...] = jnp.broadcast_to(total, carry_ref.shape)
    cnt_ref[...] = jnp.broadcast_to(total, cnt_ref.shape).astype(jnp.int32)

    row8 = lax.broadcasted_iota(jnp.int32, (SUBLANES, R), 0)
    idx_ref[0] = jnp.where(row8 == 0, cls, jnp.where(row8 == 1, rank.astype(jnp.int32), 0))
    w_rows = jnp.where(row == 0, w_lo, jnp.where(row == 1, w_hi, 0.0))
    w_rows = jnp.concatenate([w_rows, jnp.zeros((LANES - ROUTE_ROWS, R), F32)], axis=0)
    wgt_ref[...] = w_rows.T


def _router(h, m_norm, w_group, b_group, w_expert, b_expert):
    T, D = h.shape
    R = MOE_ROWS
    assert T % R == 0 and N_CLASSES <= ROUTE_ROWS
    pad_g = EXPERT_ROW0 - N_GROUPS
    pad_e = ROUTE_ROWS - EXPERT_ROW0 - N_EXPERTS
    w_t = jnp.pad(jnp.concatenate([w_group.T, jnp.zeros((pad_g, D), F32), w_expert.T], axis=0), ((0, pad_e), (0, 0)))
    w_t_hi = w_t.astype(BF16)
    w_t = jnp.concatenate([w_t_hi, (w_t - w_t_hi.astype(F32)).astype(BF16)], axis=0)
    b_t = jnp.pad(jnp.concatenate([b_group, jnp.zeros((pad_g,), F32), b_expert]), (0, pad_e))
    b_t = jnp.broadcast_to(b_t[:, None], (ROUTE_ROWS, LANES))
    idx = jnp.arange(R)
    upper = (idx[:, None] < idx[None, :]).astype(BF16)
    const = lambda i: (0, 0)
    return pl.pallas_call(
        _router_kernel,
        grid=(T // R,),
        in_specs=[
            pl.BlockSpec((R, D), lambda i: (i, 0)),
            pl.BlockSpec((1, D), const),
            pl.BlockSpec((2 * ROUTE_ROWS, D), const),
            pl.BlockSpec((ROUTE_ROWS, LANES), const),
            pl.BlockSpec((R, R), const),
        ],
        out_specs=[
            pl.BlockSpec((1, SUBLANES, R), lambda i: (i, 0, 0)),
            pl.BlockSpec((R, LANES), lambda i: (i, 0)),
            pl.BlockSpec((ROUTE_ROWS, LANES), const),
        ],
        out_shape=[
            jax.ShapeDtypeStruct((T // R, SUBLANES, R), jnp.int32),
            jax.ShapeDtypeStruct((T, LANES), F32),
            jax.ShapeDtypeStruct((ROUTE_ROWS, LANES), jnp.int32),
        ],
        scratch_shapes=[pltpu.VMEM((ROUTE_ROWS, LANES), F32)],
        compiler_params=pltpu.CompilerParams(dimension_semantics=("arbitrary",), vmem_limit_bytes=VMEM_LIMIT),
        name="moe_router",
    )(h, m_norm.reshape(1, D), w_t, b_t, upper)


def _dispatch_kernel(pos_ref, pos_prev_ref, h_ref, wgt_ref, xs_in_ref, xs_ref, row_ref, sem):
    del xs_in_ref
    i = pl.program_id(0)
    n = pl.num_programs(0)
    R, D = h_ref.shape
    G = R // SUBLANES
    slot = i % 2
    row_ref[slot, :, :, :D] = h_ref[...].reshape(G, SUBLANES, D)
    row_ref[slot, :, :, D:] = wgt_ref[...].reshape(G, SUBLANES, LANES)

    def row_copy(p_ref, s, g, j):
        dst = p_ref[0, 0, g * SUBLANES + j]
        return pltpu.make_async_copy(row_ref.at[s, g, pl.ds(j, 1)], xs_ref.at[pl.ds(dst, 1)], sem.at[s])

    def issue(g, c):
        for j in range(SUBLANES):
            row_copy(pos_ref, slot, g, j).start()
        return c

    lax.fori_loop(0, G, issue, 0)

    def drain(p_ref, s):
        def body(g, c):
            for j in range(SUBLANES):
                row_copy(p_ref, s, g, j).wait()
            return c
        lax.fori_loop(0, G, body, 0)

    @pl.when(i > 0)
    def _():
        drain(pos_prev_ref, 1 - slot)

    @pl.when(i == n - 1)
    def _():
        drain(pos_ref, slot)


def _dispatch(h, wgt, pos, n_rows):
    T, D = h.shape
    R = MOE_ROWS
    n_t = T // R
    xs0 = jnp.zeros((n_rows, D + LANES), F32)
    pos3 = pos.reshape(n_t, 1, R)
    return pl.pallas_call(
        _dispatch_kernel,
        grid=(n_t,),
        in_specs=[
            pl.BlockSpec((1, 1, R), lambda i: (i, 0, 0), memory_space=pltpu.SMEM),
            pl.BlockSpec((1, 1, R), lambda i: (jnp.maximum(i - 1, 0), 0, 0), memory_space=pltpu.SMEM),
            pl.BlockSpec((R, D), lambda i: (i, 0)),
            pl.BlockSpec((R, LANES), lambda i: (i, 0)),
            pl.BlockSpec(memory_space=pl.ANY),
        ],
        out_specs=pl.BlockSpec(memory_space=pl.ANY),
        out_shape=jax.ShapeDtypeStruct((n_rows, D + LANES), F32),
        scratch_shapes=[pltpu.VMEM((2, R // SUBLANES, SUBLANES, D + LANES), F32), pltpu.SemaphoreType.DMA((2,))],
        input_output_aliases={4: 0},
        compiler_params=pltpu.CompilerParams(dimension_semantics=("arbitrary",), vmem_limit_bytes=VMEM_LIMIT),
        name="moe_dispatch",
    )(pos3, pos3, h, wgt, xs0)


def _experts_kernel(used_ref, lo_ref, hi_ref, xs_ref, nrm_ref, wgu_lo_ref, wd_lo_ref, wgu_hi_ref, wd_hi_ref,
                    ys_ref, *, d_model, d_expert):
    del lo_ref, hi_ref
    active = pl.program_id(0) < used_ref[0]

    @pl.when(active)
    def _():
        x = xs_ref[:, :d_model]
        t = (x * _rms_scale(x) * nrm_ref[...]).astype(BF16)
        y = None
        for k, (wgu_ref, wd_ref) in enumerate(((wgu_lo_ref, wd_lo_ref), (wgu_hi_ref, wd_hi_ref))):
            gu = _dot(t, wgu_ref[0])
            gate = gu[:, :d_expert]
            hdn = (gate * _sigmoid(gate) * gu[:, d_expert:]).astype(BF16)
            y_k = xs_ref[:, d_model + k:d_model + k + 1] * _dot(hdn, wd_ref[0])
            y = y_k if y is None else y + y_k
        ys_ref[...] = y

    @pl.when(jnp.logical_not(active))
    def _():
        ys_ref[...] = jnp.zeros_like(ys_ref)


def _experts(xs, n_used, tile_lo, tile_hi, m_norm, w_gate, w_up, w_down):
    n_rows = xs.shape[0]
    D = m_norm.shape[0]
    d_expert = w_gate.shape[-1]
    TM = MOE_TM
    n_tiles = n_rows // TM
    w_gu = jnp.concatenate([w_gate, w_up], axis=-1).astype(BF16)
    w_d = w_down.astype(BF16)
    grid_spec = pltpu.PrefetchScalarGridSpec(
        num_scalar_prefetch=3,
        grid=(n_tiles,),
        in_specs=[
            pl.BlockSpec((TM, D + LANES), lambda i, nu, lo, hi: (i, 0)),
            pl.BlockSpec((1, D), lambda i, nu, lo, hi: (0, 0)),
            pl.BlockSpec((1, D, 2 * d_expert), lambda i, nu, lo, hi: (lo[i], 0, 0)),
            pl.BlockSpec((1, d_expert, D), lambda i, nu, lo, hi: (lo[i], 0, 0)),
            pl.BlockSpec((1, D, 2 * d_expert), lambda i, nu, lo, hi: (hi[i], 0, 0)),
            pl.BlockSpec((1, d_expert, D), lambda i, nu, lo, hi: (hi[i], 0, 0)),
        ],
        out_specs=pl.BlockSpec((TM, D), lambda i, nu, lo, hi: (i, 0)),
    )
    return pl.pallas_call(
        functools.partial(_experts_kernel, d_model=D, d_expert=d_expert),
        grid_spec=grid_spec,
        out_shape=jax.ShapeDtypeStruct((n_rows, D), F32),
        compiler_params=pltpu.CompilerParams(dimension_semantics=("arbitrary",), vmem_limit_bytes=VMEM_LIMIT),
        name="moe_experts",
    )(n_used, tile_lo, tile_hi, xs, m_norm.reshape(1, D), w_gu, w_d, w_gu, w_d)


def _combine_norm_kernel(pos_ref, pos_next_ref, h_ref, nrm_ref, ys_ref, out_ref, buf_ref, sem):
    i = pl.program_id(0)
    n = pl.num_programs(0)
    R, D = h_ref.shape
    G = R // SUBLANES
    slot = i % 2

    def row_copy(p_ref, s, g, j):
        src = p_ref[0, 0, g * SUBLANES + j]
        return pltpu.make_async_copy(ys_ref.at[pl.ds(src, 1)], buf_ref.at[s, g, pl.ds(j, 1)], sem.at[s])

    def issue(p_ref, s):
        def body(g, c):
            for j in range(SUBLANES):
                row_copy(p_ref, s, g, j).start()
            return c
        lax.fori_loop(0, G, body, 0)

    @pl.when(i == 0)
    def _():
        issue(pos_ref, slot)

    @pl.when(i < n - 1)
    def _():
        issue(pos_next_ref, 1 - slot)

    def drain(g, c):
        for j in range(SUBLANES):
            row_copy(pos_ref, slot, g, j).wait()
        return c

    lax.fori_loop(0, G, drain, 0)

    out = h_ref[...] + buf_ref[slot].reshape(R, D)
    out_ref[...] = out * _rms_scale(out) * nrm_ref[...]


def _combine_norm(h, pos, ys, norm):
    T, D = h.shape
    R = MOE_ROWS
    n_t = T // R
    pos3 = pos.reshape(n_t, 1, R)
    return pl.pallas_call(
        _combine_norm_kernel,
        grid=(n_t,),
        in_specs=[
            pl.BlockSpec((1, 1, R), lambda i: (i, 0, 0), memory_space=pltpu.SMEM),
            pl.BlockSpec((1, 1, R), lambda i: (jnp.minimum(i + 1, n_t - 1), 0, 0), memory_space=pltpu.SMEM),
            pl.BlockSpec((R, D), lambda i: (i, 0)),
            pl.BlockSpec((1, D), lambda i: (0, 0)),
            pl.BlockSpec(memory_space=pl.ANY),
        ],
        out_specs=pl.BlockSpec((R, D), lambda i: (i, 0)),
        out_shape=jax.ShapeDtypeStruct((T, D), F32),
        scratch_shapes=[pltpu.VMEM((2, R // SUBLANES, SUBLANES, D), F32), pltpu.SemaphoreType.DMA((2,))],
        compiler_params=pltpu.CompilerParams(dimension_semantics=("arbitrary",), vmem_limit_bytes=VMEM_LIMIT),
        name="moe_combine",
    )(pos3, pos3, h, norm.reshape(1, D), ys)


def _class_experts():
    lo, hi = [], []
    for g in range(N_GROUPS):
        for a in range(EXPERTS_PER_GROUP):
            for b in range(a + 1, EXPERTS_PER_GROUP):
                lo.append(g * EXPERTS_PER_GROUP + a)
                hi.append(g * EXPERTS_PER_GROUP + b)
    return jnp.asarray(lo, jnp.int32), jnp.asarray(hi, jnp.int32)


def _moe_sorted(h, m_norm, w_group, b_group, w_expert, b_expert, w_gate, w_up, w_down):
    T, D = h.shape
    TM = MOE_TM
    idx, wgt, cnt = _router(h, m_norm, w_group, b_group, w_expert, b_expert)

    counts = cnt[:N_CLASSES, 0]
    padded = ((counts + TM - 1) // TM) * TM
    ends = jnp.cumsum(padded)
    starts = ends - padded
    n_rows = T + N_CLASSES * TM
    n_tiles = n_rows // TM
    tile_start = jnp.arange(n_tiles, dtype=jnp.int32) * TM
    tile_class = jnp.minimum(
        jnp.sum((ends[None, :] <= tile_start[:, None]).astype(jnp.int32), axis=1), N_CLASSES - 1)
    class_lo, class_hi = _class_experts()
    n_used = (ends[-1] // TM).astype(jnp.int32).reshape(1)
    cls, pos = idx[:, 0, :], idx[:, 1, :]
    for c in range(N_CLASSES):
        pos = pos + jnp.where(cls == c, starts[c], 0)
    pos = pos.astype(jnp.int32)

    xs = _dispatch(h, wgt, pos, n_rows)
    ys = _experts(xs, n_used, class_lo[tile_class], class_hi[tile_class], m_norm, w_gate, w_up, w_down)
    return pos, ys


def _combine_qkv_kernel(pos_ref, pos_next_ref, h_ref, ys_ref, qn_ref, kvn_ref, wq_ref, wkt_ref, wv_ref,
                        hout_ref, q_ref, kt_ref, v_ref, buf_ref, sem):
    i = pl.program_id(0)
    n = pl.num_programs(0)
    R, D = h_ref.shape
    G = R // SUBLANES
    slot = i % 2

    def row_copy(p_ref, s, g, j):
        src = p_ref[0, 0, g * SUBLANES + j]
        return pltpu.make_async_copy(ys_ref.at[pl.ds(src, 1)], buf_ref.at[s, g, pl.ds(j, 1)], sem.at[s])

    @pl.when(i == 0)
    def _():
        def body(g, c):
            for j in range(SUBLANES):
                row_copy(pos_ref, slot, g, j).start()
            return c
        lax.fori_loop(0, G, body, 0)

    def drain(g, c):
        for j in range(SUBLANES):
            row_copy(pos_ref, slot, g, j).wait()
        return c

    lax.fori_loop(0, G, drain, 0)

    h = h_ref[...] + buf_ref[slot].reshape(R, D)
    hout_ref[...] = h
    for g in range(G):
        for j in range(SUBLANES):
            row_copy(pos_next_ref, 1 - slot, g, j).start()
    xhat = h * _rms_scale(h)
    uq = (xhat * qn_ref[...]).astype(BF16)
    ukv = (xhat * kvn_ref[...]).astype(BF16)
    q_ref[...] = _dot(uq, wq_ref[...]).astype(BF16)
    v_ref[...] = _dot(ukv, wv_ref[...]).astype(BF16)
    kt_ref[0] = _dot_nt(wkt_ref[...], ukv).astype(BF16)

    @pl.when(i == n - 1)
    def _():
        def body(g, c):
            for j in range(SUBLANES):
                row_copy(pos_next_ref, 1 - slot, g, j).wait()
            return c
        lax.fori_loop(0, G, body, 0)


def _combine_qkv(h, pos, ys, batch, seq, q_norm, kv_norm, w_q, w_kv, scale):
    T, D = h.shape
    sb_dim = w_q.shape[1]
    R = MOE_ROWS
    assert seq % R == 0
    n_t = T // R
    n_s = seq // R
    pos3 = pos.reshape(n_t, 1, R)
    w_qs = (w_q * scale).astype(BF16)
    w_kt = w_kv[:, :sb_dim].T.astype(BF16)
    w_v = w_kv[:, sb_dim:].astype(BF16)
    const = lambda i: (0, 0)
    return pl.pallas_call(
        _combine_qkv_kernel,
        grid=(n_t,),
        in_specs=[
            pl.BlockSpec((1, 1, R), lambda i: (i, 0, 0), memory_space=pltpu.SMEM),
            pl.BlockSpec((1, 1, R), lambda i: (jnp.minimum(i + 1, n_t - 1), 0, 0), memory_space=pltpu.SMEM),
            pl.BlockSpec((R, D), lambda i: (i, 0)),
            pl.BlockSpec(memory_space=pl.ANY),
            pl.BlockSpec((1, D), const),
            pl.BlockSpec((1, D), const),
            pl.BlockSpec((D, sb_dim), const),
            pl.BlockSpec((sb_dim, D), const),
            pl.BlockSpec((D, sb_dim), const),
        ],
        out_specs=[
            pl.BlockSpec((R, D), lambda i: (i, 0)),
            pl.BlockSpec((R, sb_dim), lambda i: (i, 0)),
            pl.BlockSpec((1, sb_dim, R), lambda i: (i // n_s, 0, i % n_s)),
            pl.BlockSpec((R, sb_dim), lambda i: (i, 0)),
        ],
        out_shape=[
            jax.ShapeDtypeStruct((T, D), F32),
            jax.ShapeDtypeStruct((T, sb_dim), BF16),
            jax.ShapeDtypeStruct((batch, sb_dim, seq), BF16),
            jax.ShapeDtypeStruct((T, sb_dim), BF16),
        ],
        scratch_shapes=[pltpu.VMEM((2, R // SUBLANES, SUBLANES, D), F32), pltpu.SemaphoreType.DMA((2,))],
        compiler_params=pltpu.CompilerParams(dimension_semantics=("arbitrary",), vmem_limit_bytes=VMEM_LIMIT),
        name="combine_qkv",
    )(pos3, pos3, h, ys, q_norm.reshape(1, D), kv_norm.reshape(1, D), w_qs, w_kt, w_v)


def _sb_attention_kernel(q_ref, kt_ref, v_ref, o_ref, *, head_dim):
    S = q_ref.shape[0]
    T = ATT_TILE
    n_q = S // T
    n_heads = LANES // head_dim
    lane_q = lax.broadcasted_iota(jnp.int32, (T, LANES), 1)
    trow = lax.broadcasted_iota(jnp.int32, (T, T), 0)
    scol = lax.broadcasted_iota(jnp.int32, (T, T), 1)
    strictly_before = scol < trow
    suffix = jnp.where(trow >= scol, 1.0, 0.0).astype(BF16)
    in_head = [(lane_q >= hd * head_dim) & (lane_q < (hd + 1) * head_dim) for hd in range(n_heads)]

    ksq = jnp.square(kt_ref[0].astype(F32))
    k_max = [jnp.sqrt(jnp.max(jnp.sum(ksq[hd * head_dim:(hd + 1) * head_dim], axis=0, keepdims=True),
                              axis=1, keepdims=True)) for hd in range(n_heads)]

    def tile(q_h, k0, acc, run, mask):
        z = _dot(q_h, kt_ref[0, :, pl.ds(k0, T)])
        sp = jnp.maximum(z, 0.0) + jnp.log2(1.0 + jnp.exp2(-jnp.abs(z)))
        if mask is not None:
            sp = jnp.where(mask, sp, 0.0)
        within = _dot(sp.astype(BF16), suffix)
        a = jnp.exp2(z - within - run)
        if mask is not None:
            a = jnp.where(mask, a, 0.0)
        acc = acc + _dot(a.astype(BF16), v_ref[pl.ds(k0, T), :])
        return acc, run + within[:, 0:1]

    def all_zero_from_here(runs, z_bound):
        slack = runs[0] - z_bound[0]
        for r, zb in zip(runs[1:], z_bound[1:]):
            slack = jnp.minimum(slack, r - zb)
        return (jnp.min(slack) > ATT_ZERO_MARGIN).astype(jnp.int32)

    def block_head(qi, with_left):
        q0 = qi * T if isinstance(qi, int) else pl.multiple_of(qi * T, T)
        q_pair = q_ref[pl.ds(q0, T), :]
        qsq = jnp.square(q_pair.astype(F32))
        q_hs = [jnp.where(m, q_pair, jnp.zeros_like(q_pair)) for m in in_head]
        z_bound = [jnp.sqrt(jnp.sum(jnp.where(m, qsq, 0.0), axis=-1, keepdims=True)) * km
                   for m, km in zip(in_head, k_max)]
        st = [tile(q_h, q0, jnp.zeros((T, LANES), F32), jnp.zeros((T, 1), F32), strictly_before) for q_h in q_hs]
        if with_left:
            st = [tile(q_h, pl.multiple_of(q0 - T, T), a, r, None) for q_h, (a, r) in zip(q_hs, st)]
        accs = [a for a, _ in st]
        runs = [r for _, r in st]
        return q0, q_hs, z_bound, accs, runs, all_zero_from_here(runs, z_bound)

    def block_tail(qi, with_left, q0, q_hs, z_bound, accs, runs, done):
        def cond(c):
            return (c[0] >= 0) & (c[1] == 0)

        def body(c):
            j, _, accs, runs = c
            k0 = pl.multiple_of(j * T, T)
            st = [tile(q_h, k0, a, r, None) for q_h, a, r in zip(q_hs, accs, runs)]
            accs = [a for a, _ in st]
            runs = [r for _, r in st]
            return j - 1, all_zero_from_here(runs, z_bound), accs, runs

        if with_left:
            _, _, accs, _ = lax.while_loop(cond, body, (qi - 2, done, accs, runs))
        result = accs[0]
        for m, a in zip(in_head[1:], accs[1:]):
            result = jnp.where(m, a, result)
        o_ref[pl.ds(q0, T), :] = result.astype(o_ref.dtype)

    def q_blocks(qis, with_left):
        heads = [block_head(qi, with_left) for qi in qis]
        for qi, hd in zip(qis, heads):
            block_tail(qi, with_left, *hd)

    q_blocks([0], False)

    def block_pair(p, carry):
        q_blocks([2 * p + 1, 2 * p + 2], True)
        return carry

    lax.fori_loop(0, (n_q - 1) // 2, block_pair, 0)
    if (n_q - 1) % 2:
        q_blocks([n_q - 1], True)


def _sb_attention(q, kt, v, batch, seq, head_dim):
    T, sb_dim = q.shape
    assert LANES % head_dim == 0 and seq % ATT_TILE == 0
    n_p = sb_dim // LANES
    return pl.pallas_call(
        functools.partial(_sb_attention_kernel, head_dim=head_dim),
        grid=(batch, n_p),
        in_specs=[
            pl.BlockSpec((seq, LANES), lambda b, p: (b, p)),
            pl.BlockSpec((1, LANES, seq), lambda b, p: (b, p, 0)),
            pl.BlockSpec((seq, LANES), lambda b, p: (b, p)),
        ],
        out_specs=pl.BlockSpec((seq, LANES), lambda b, p: (b, p)),
        out_shape=jax.ShapeDtypeStruct((T, sb_dim), BF16),
        compiler_params=pltpu.CompilerParams(
            dimension_semantics=("arbitrary", "arbitrary"), vmem_limit_bytes=VMEM_LIMIT),
        name="sb_attention",
    )(q, kt, v)


def _out_proj_kernel(o_ref, w_ref, h_ref, out_ref):
    out_ref[...] = h_ref[...] + _dot(o_ref[...], w_ref[...])


def _out_proj(o, w_out, h):
    T, D = h.shape
    R = PROJ_ROWS
    return pl.pallas_call(
        _out_proj_kernel,
        grid=(T // R,),
        in_specs=[
            pl.BlockSpec((R, o.shape[1]), lambda i: (i, 0)),
            pl.BlockSpec(w_out.shape, lambda i: (0, 0)),
            pl.BlockSpec((R, D), lambda i: (i, 0)),
        ],
        out_specs=pl.BlockSpec((R, D), lambda i: (i, 0)),
        out_shape=jax.ShapeDtypeStruct((T, D), F32),
        compiler_params=pltpu.CompilerParams(dimension_semantics=("arbitrary",), vmem_limit_bytes=VMEM_LIMIT),
        name="out_proj",
    )(o, w_out.astype(BF16), h)


def kernel(x, a_norm, a_w_in, a_w_gk2, a_b_gk2, a_o_norm, a_w_out, kv_norm, w_kv, b_norm, b_w_q, b_w_out,
           m_norm, m_w_group, m_b_group, m_w_expert, m_b_expert, m_w_gate, m_w_up, m_w_down, final_norm):
    B, S, D = x.shape
    assert a_norm.shape[0] == 1 and b_norm.shape[0] == 1 and m_norm.shape[0] == 2
    head_dim = b_w_q.shape[2] // SB_HEADS
    h = x.reshape(B * S, D)

    def moe_sorted(h, layer):
        return _moe_sorted(h, m_norm[layer], m_w_group[layer], m_b_group[layer], m_w_expert[layer],
                           m_b_expert[layer], m_w_gate[layer], m_w_up[layer], m_w_down[layer])

    h = _gla_layer(h, B, S, a_norm[0], a_w_in[0], a_w_gk2[0], a_b_gk2[0], a_o_norm[0], a_w_out[0])
    pos, ys = moe_sorted(h, 0)
    h, q, kt, v = _combine_qkv(h, pos, ys, B, S, b_norm[0], kv_norm, b_w_q[0], w_kv,
                               math.log2(math.e) / math.sqrt(head_dim))
    o = _sb_attention(q, kt, v, B, S, head_dim)
    h = _out_proj(o, b_w_out[0], h)
    pos, ys = moe_sorted(h, 1)
    h = _combine_norm(h, pos, ys, final_norm)
    return h.reshape(B, S, D)
```

```python
import functools
import math

import jax
import jax.numpy as jnp
from jax import lax
from jax.experimental import pallas as pl
from jax.experimental.pallas import tpu as pltpu

RMS_EPS = 1e-6

GLA_HEADS = 4
GLA_CHUNK = 64
CHUNK_SHIFT = GLA_CHUNK.bit_length() - 1
GATE_NORMALIZER = 16.0
SB_HEADS = 16
N_GROUPS = 4
EXPERTS_PER_GROUP = 4
N_EXPERTS = N_GROUPS * EXPERTS_PER_GROUP
PAIRS_PER_GROUP = EXPERTS_PER_GROUP * (EXPERTS_PER_GROUP - 1) // 2
N_CLASSES = N_GROUPS * PAIRS_PER_GROUP

LANES = 128
SUBLANES = 8
ROUTE_ROWS = 32
EXPERT_ROW0 = 8
VMEM_LIMIT = 56 * 1024 * 1024

GLA_ROWS = 512
GLA_BLOCK = 256
PROJ_ROWS = 512
ATT_TILE = 256
ATT_ZERO_MARGIN = 160.0
ATT_EXP2_CAP = 100.0
MOE_ROWS = 512
MOE_TM = 256

BF16 = jnp.bfloat16
F32 = jnp.float32


def _dot(a, b):
    return jnp.dot(a, b, preferred_element_type=F32)


def _dot_nt(a, b):
    return lax.dot_general(a, b, (((1,), (1,)), ((), ())), preferred_element_type=F32)


def _split_dot(m01, x, left):
    hi = x.astype(BF16)
    lo = (x - hi.astype(F32)).astype(BF16)
    if left:
        return _dot(m01, hi) + _dot(m01, lo)
    return _dot(hi, m01) + _dot(lo, m01)


def _rms_scale(x):
    return lax.rsqrt(jnp.mean(x * x, axis=-1, keepdims=True) + RMS_EPS)


def _log_sigmoid(x):
    return jnp.minimum(x, 0.0) - jnp.log(1.0 + jnp.exp(-jnp.abs(x)))


def _softplus(x):
    return jnp.maximum(x, 0.0) + jnp.log(1.0 + jnp.exp(-jnp.abs(x)))


def _sigmoid(x):
    return 1.0 / (1.0 + jnp.exp(-x))


def _gla_block(h, states, nrm_ref, wrow_ref, wkt_ref, wlrt_ref, wgk2_ref, bgk2_ref, wgk2t_ref, bgk2c_ref,
               onorm_ref, wout_ref, *, dk, dv, qk_dim, v_dim, log_scale):
    R = h.shape[0]
    n_chunks = R // GLA_CHUNK
    u = (h * _rms_scale(h) * nrm_ref[...]).astype(BF16)

    proj = _dot(u, wrow_ref[...])
    q = proj[:, :qk_dim]
    v = proj[:, qk_dim:qk_dim + v_dim].astype(BF16)
    g = proj[:, qk_dim + v_dim:qk_dim + 2 * v_dim]
    lr = proj[:, qk_dim + 2 * v_dim:].astype(BF16)
    kt = _dot_nt(wkt_ref[...], u)
    lrt = _dot_nt(wlrt_ref[...], u).astype(BF16)

    gk = _log_sigmoid(_dot(lr, wgk2_ref[...]) + bgk2_ref[...]) * (1.0 / GATE_NORMALIZER)
    gkt = _log_sigmoid(_dot(wgk2t_ref[...], lrt) + bgk2c_ref[...]) * (1.0 / GATE_NORMALIZER)

    row = lax.broadcasted_iota(jnp.int32, (R, R), 0)
    col = lax.broadcasted_iota(jnp.int32, (R, R), 1)
    same_chunk = (row >> CHUNK_SHIFT) == (col >> CHUNK_SHIFT)
    causal = same_chunk & (col <= row)
    lbd = jnp.where(causal, 1.0, 0.0).astype(BF16)
    ubd = jnp.where(same_chunk & (row <= col), 1.0, 0.0).astype(BF16)
    after = jnp.where(same_chunk & (row > col), 1.0, 0.0).astype(BF16)

    b = _split_dot(lbd, gk, left=True)
    bt = _split_dot(ubd, gkt, left=False)
    tail_t = _split_dot(after, gkt, left=False)

    q_dec = (q * jnp.exp(b + log_scale)).astype(BF16)
    k_inv_t = (kt * jnp.exp(-bt)).astype(BF16)
    k_end_t = (kt * jnp.exp(tail_t)).astype(BF16)
    chunk_decay_t = jnp.exp(bt + tail_t)

    lane_chunk = lax.broadcasted_iota(jnp.int32, (dk, R), 1) >> CHUNK_SHIFT

    acc = h
    new_states = []
    for hd in range(GLA_HEADS):
        ks = slice(hd * dk, (hd + 1) * dk)
        vs = slice(hd * dv, (hd + 1) * dv)
        qd_h = q_dec[:, ks]
        v_h = v[:, vs]
        att = _dot(qd_h, k_inv_t[ks, :])
        att = jnp.where(causal, att, 0.0).astype(BF16)
        o_h = _dot(att, v_h)
        kend_h = k_end_t[ks, :]
        state = states[hd]
        inter = []
        for c in range(n_chunks):
            rows = slice(c * GLA_CHUNK, (c + 1) * GLA_CHUNK)
            inter.append(_dot(qd_h[rows], state.astype(BF16)))
            kend_c = jnp.where(lane_chunk == c, kend_h, jnp.zeros_like(kend_h))
            decay = chunk_decay_t[ks, c * GLA_CHUNK:c * GLA_CHUNK + 1]
            state = decay * state + _dot(kend_c, v_h)
        new_states.append(state)
        o_h = o_h + jnp.concatenate(inter, axis=0)
        o_h = o_h * _rms_scale(o_h) * onorm_ref[...]
        g_h = g[:, vs]
        o_h = o_h * (g_h * _sigmoid(g_h))
        acc = acc + _dot(o_h.astype(BF16), wout_ref[vs, :])
    return acc, new_states


def _gla_kernel(h_ref, *refs, **dims):
    *w_refs, out_ref, state_ref = refs

    @pl.when(pl.program_id(1) == 0)
    def _():
        state_ref[...] = jnp.zeros_like(state_ref)

    states = [state_ref[hd] for hd in range(GLA_HEADS)]
    for blk in range(h_ref.shape[0] // GLA_BLOCK):
        rows = slice(blk * GLA_BLOCK, (blk + 1) * GLA_BLOCK)
        out, states = _gla_block(h_ref[rows, :], states, *w_refs, **dims)
        out_ref[rows, :] = out
    for hd in range(GLA_HEADS):
        state_ref[hd] = states[hd]


def _gla_layer(h, batch, seq, a_norm, w_in, w_gk2, b_gk2, o_norm, w_out):
    T, D = h.shape
    rank, qk_dim = w_gk2.shape
    v_dim = w_out.shape[0]
    dk = qk_dim // GLA_HEADS
    dv = v_dim // GLA_HEADS
    R = GLA_ROWS
    assert seq % R == 0 and R % GLA_CHUNK == 0 and rank <= LANES
    assert w_in.shape[1] == 2 * qk_dim + 2 * v_dim + rank
    n_s = seq // R

    o_q, o_k, o_v, o_lr, o_g = 0, qk_dim, 2 * qk_dim, 2 * qk_dim + v_dim, 2 * qk_dim + v_dim + rank
    w_lr = jnp.pad(w_in[:, o_lr:o_lr + rank], ((0, 0), (0, LANES - rank)))
    w_row = jnp.concatenate([w_in[:, o_q:o_k], w_in[:, o_v:o_lr], w_in[:, o_g:], w_lr], axis=1).astype(BF16)
    w_kt = w_in[:, o_k:o_v].T.astype(BF16)
    w_lrt = w_lr.T.astype(BF16)
    w_gk2p = jnp.pad(w_gk2, ((0, LANES - rank), (0, 0))).astype(BF16)
    w_gk2t = w_gk2p.T
    const = lambda b, s: (0, 0)
    kern = functools.partial(_gla_kernel, dk=dk, dv=dv, qk_dim=qk_dim, v_dim=v_dim,
                             log_scale=math.log(dk ** -0.5))
    return pl.pallas_call(
        kern,
        grid=(batch, n_s),
        in_specs=[
            pl.BlockSpec((R, D), lambda b, s: (b * n_s + s, 0)),
            pl.BlockSpec((1, D), const),
            pl.BlockSpec(w_row.shape, const),
            pl.BlockSpec(w_kt.shape, const),
            pl.BlockSpec(w_lrt.shape, const),
            pl.BlockSpec(w_gk2p.shape, const),
            pl.BlockSpec((1, qk_dim), const),
            pl.BlockSpec(w_gk2t.shape, const),
            pl.BlockSpec((qk_dim, 1), const),
            pl.BlockSpec((1, dv), const),
            pl.BlockSpec((v_dim, D), const),
        ],
        out_specs=pl.BlockSpec((R, D), lambda b, s: (b * n_s + s, 0)),
        out_shape=jax.ShapeDtypeStruct((T, D), F32),
        scratch_shapes=[pltpu.VMEM((GLA_HEADS, dk, dv), F32)],
        compiler_params=pltpu.CompilerParams(
            dimension_semantics=("arbitrary", "arbitrary"), vmem_limit_bytes=VMEM_LIMIT),
        name="gla_layer",
    )(h, a_norm.reshape(1, D), w_row, w_kt, w_lrt, w_gk2p, b_gk2.reshape(1, qk_dim), w_gk2t,
      b_gk2.reshape(qk_dim, 1), o_norm.reshape(1, dv), w_out.astype(BF16))


def _router_kernel(h_ref, nrm_ref, wt_ref, bt_ref, upper_ref, idx_ref, wgt_ref, cnt_ref, carry_ref):
    R = h_ref.shape[0]

    @pl.when(pl.program_id(0) == 0)
    def _():
        carry_ref[...] = jnp.zeros_like(carry_ref)

    h = h_ref[...]
    t = h * _rms_scale(h) * nrm_ref[...]
    t_hi = t.astype(BF16)
    t_lo = (t - t_hi.astype(F32)).astype(BF16)
    p = _dot_nt(wt_ref[...], t_hi)
    logits = p[:ROUTE_ROWS] + p[ROUTE_ROWS:] + _dot_nt(wt_ref[:ROUTE_ROWS], t_lo) + bt_ref[:, 0:1]
    row = lax.broadcasted_iota(jnp.int32, (ROUTE_ROWS, R), 0)
    neg = jnp.float32(-jnp.inf)

    def first_argmax(vals):
        m = jnp.max(vals, axis=0, keepdims=True)
        i = jnp.min(jnp.where(vals == m, row, ROUTE_ROWS), axis=0, keepdims=True)
        return m, i

    gl = jnp.where(row < N_GROUPS, logits, neg)
    g_max, g_idx = first_argmax(gl)
    g_w = 1.0 / jnp.sum(jnp.exp(gl - g_max), axis=0, keepdims=True)

    base = EXPERT_ROW0 + g_idx * EXPERTS_PER_GROUP
    el = jnp.where((row >= base) & (row < base + EXPERTS_PER_GROUP), logits, neg)
    m1, i1 = first_argmax(el)
    m2, i2 = first_argmax(jnp.where(row == i1, neg, el))
    r = jnp.exp(m2 - m1)
    w1 = g_w / (1.0 + r)
    w2 = g_w * r / (1.0 + r)
    l1 = i1 - base
    l2 = i2 - base
    first_is_lo = l1 < l2
    lo = jnp.minimum(l1, l2)
    hi = jnp.maximum(l1, l2)
    cls = g_idx * PAIRS_PER_GROUP + ((lo * (2 * EXPERTS_PER_GROUP - 1 - lo)) >> 1) + (hi - lo - 1)
    w_lo = jnp.where(first_is_lo, w1, w2)
    w_hi = jnp.where(first_is_lo, w2, w1)

    onehot = jnp.where(row == cls, 1.0, 0.0)
    before = _dot(onehot.astype(BF16), upper_ref[...]) + carry_ref[:, 0:1]
    rank = jnp.sum(jnp.where(row == cls, before, 0.0), axis=0, keepdims=True)
    total = carry_ref[:, 0:1] + jnp.sum(onehot, axis=1, keepdims=True)
    carry_ref[...] = jnp.broadcast_to(total, carry_ref.shape)
    cnt_ref[...] = jnp.broadcast_to(total, cnt_ref.shape).astype(jnp.int32)

    row8 = lax.broadcasted_iota(jnp.int32, (SUBLANES, R), 0)
    idx_ref[0] = jnp.where(row8 == 0, cls, jnp.where(row8 == 1, rank.astype(jnp.int32), 0))
    w_rows = jnp.where(row == 0, w_lo, jnp.where(row == 1, w_hi, 0.0))
    w_rows = jnp.concatenate([w_rows, jnp.zeros((LANES - ROUTE_ROWS, R), F32)], axis=0)
    wgt_ref[...] = w_rows.T


def _router(h, m_norm, w_group, b_group, w_expert, b_expert):
    T, D = h.shape
    R = MOE_ROWS
    assert T % R == 0 and N_CLASSES <= ROUTE_ROWS
    pad_g = EXPERT_ROW0 - N_GROUPS
    pad_e = ROUTE_ROWS - EXPERT_ROW0 - N_EXPERTS
    w_t = jnp.pad(jnp.concatenate([w_group.T, jnp.zeros((pad_g, D), F32), w_expert.T], axis=0), ((0, pad_e), (0, 0)))
    w_t_hi = w_t.astype(BF16)
    w_t = jnp.concatenate([w_t_hi, (w_t - w_t_hi.astype(F32)).astype(BF16)], axis=0)
    b_t = jnp.pad(jnp.concatenate([b_group, jnp.zeros((pad_g,), F32), b_expert]), (0, pad_e))
    b_t = jnp.broadcast_to(b_t[:, None], (ROUTE_ROWS, LANES))
    idx = jnp.arange(R)
    upper = (idx[:, None] < idx[None, :]).astype(BF16)
    const = lambda i: (0, 0)
    return pl.pallas_call(
        _router_kernel,
        grid=(T // R,),
        in_specs=[
            pl.BlockSpec((R, D), lambda i: (i, 0)),
            pl.BlockSpec((1, D), const),
            pl.BlockSpec((2 * ROUTE_ROWS, D), const),
            pl.BlockSpec((ROUTE_ROWS, LANES), const),
            pl.BlockSpec((R, R), const),
        ],
        out_specs=[
            pl.BlockSpec((1, SUBLANES, R), lambda i: (i, 0, 0)),
            pl.BlockSpec((R, LANES), lambda i: (i, 0)),
            pl.BlockSpec((ROUTE_ROWS, LANES), const),
        ],
        out_shape=[
            jax.ShapeDtypeStruct((T // R, SUBLANES, R), jnp.int32),
            jax.ShapeDtypeStruct((T, LANES), F32),
            jax.ShapeDtypeStruct((ROUTE_ROWS, LANES), jnp.int32),
        ],
        scratch_shapes=[pltpu.VMEM((ROUTE_ROWS, LANES), F32)],
        compiler_params=pltpu.CompilerParams(dimension_semantics=("arbitrary",), vmem_limit_bytes=VMEM_LIMIT),
        name="moe_router",
    )(h, m_norm.reshape(1, D), w_t, b_t, upper)


def _dispatch_kernel(pos_ref, pos_prev_ref, fill_ref, h_ref, wgt_ref, xs_ref, row_ref, zero_ref, sem, fill_sem):
    i = pl.program_id(0)
    n = pl.num_programs(0)
    R, D = h_ref.shape
    G = R // SUBLANES
    slot = i % 2

    @pl.when(i == 0)
    def _():
        zero_ref[...] = jnp.zeros_like(zero_ref)
        tile_rows = zero_ref.shape[0]

        def fill_copy(k):
            first = pl.multiple_of(jnp.maximum(fill_ref[0, 0, k], 0), tile_rows)
            return pltpu.make_async_copy(zero_ref, xs_ref.at[pl.ds(first, tile_rows)], fill_sem)

        def start(k, c):
            @pl.when(fill_ref[0, 0, k] >= 0)
            def _():
                fill_copy(k).start()
            return c

        def wait(k, c):
            @pl.when(fill_ref[0, 0, k] >= 0)
            def _():
                fill_copy(k).wait()
            return c

        lax.fori_loop(0, fill_ref.shape[2], start, 0)
        lax.fori_loop(0, fill_ref.shape[2], wait, 0)

    row_ref[slot, :, :, :D] = h_ref[...].reshape(G, SUBLANES, D)
    row_ref[slot, :, :, D:] = wgt_ref[...].reshape(G, SUBLANES, LANES)

    def row_copy(p_ref, s, g, j):
        dst = p_ref[0, 0, g * SUBLANES + j]
        return pltpu.make_async_copy(row_ref.at[s, g, pl.ds(j, 1)], xs_ref.at[pl.ds(dst, 1)], sem.at[s])

    def issue(g, c):
        for j in range(SUBLANES):
            row_copy(pos_ref, slot, g, j).start()
        return c

    lax.fori_loop(0, G, issue, 0)

    def drain(p_ref, s):
        def body(g, c):
            for j in range(SUBLANES):
                row_copy(p_ref, s, g, j).wait()
            return c
        lax.fori_loop(0, G, body, 0)

    @pl.when(i > 0)
    def _():
        drain(pos_prev_ref, 1 - slot)

    @pl.when(i == n - 1)
    def _():
        drain(pos_ref, slot)


def _dispatch(h, wgt, pos, fill, n_rows):
    T, D = h.shape
    R = MOE_ROWS
    n_t = T // R
    pos3 = pos.reshape(n_t, 1, R)
    fill3 = fill.reshape(1, 1, -1)
    return pl.pallas_call(
        _dispatch_kernel,
        grid=(n_t,),
        in_specs=[
            pl.BlockSpec((1, 1, R), lambda i: (i, 0, 0), memory_space=pltpu.SMEM),
            pl.BlockSpec((1, 1, R), lambda i: (jnp.maximum(i - 1, 0), 0, 0), memory_space=pltpu.SMEM),
            pl.BlockSpec(fill3.shape, lambda i: (0, 0, 0), memory_space=pltpu.SMEM),
            pl.BlockSpec((R, D), lambda i: (i, 0)),
            pl.BlockSpec((R, LANES), lambda i: (i, 0)),
        ],
        out_specs=pl.BlockSpec(memory_space=pl.ANY),
        out_shape=jax.ShapeDtypeStruct((n_rows, D + LANES), F32),
        scratch_shapes=[pltpu.VMEM((2, R // SUBLANES, SUBLANES, D + LANES), F32),
                        pltpu.VMEM((MOE_TM, D + LANES), F32),
                        pltpu.SemaphoreType.DMA((2,)), pltpu.SemaphoreType.DMA],
        compiler_params=pltpu.CompilerParams(dimension_semantics=("arbitrary",), vmem_limit_bytes=VMEM_LIMIT),
        name="moe_dispatch",
    )(pos3, pos3, fill3, h, wgt)


def _experts_kernel(used_ref, lo_ref, hi_ref, xs_ref, nrm_ref, g_lo_ref, u_lo_ref, d_lo_ref, g_hi_ref, u_hi_ref,
                    d_hi_ref, ys_ref, wgu_ref, wd_ref, *, d_model, d_expert):
    i = pl.program_id(0)
    active = i < used_ref[0]
    prev = jnp.maximum(i - 1, 0)

    def refresh(k, e_ref, g_ref, u_ref, d_ref):
        @pl.when(active & ((i == 0) | (e_ref[i] != e_ref[prev])))
        def _():
            wgu_ref[k, :, :d_expert] = g_ref[0, 0].astype(BF16)
            wgu_ref[k, :, d_expert:] = u_ref[0, 0].astype(BF16)
            wd_ref[k] = d_ref[0, 0].astype(BF16)

    refresh(0, lo_ref, g_lo_ref, u_lo_ref, d_lo_ref)
    refresh(1, hi_ref, g_hi_ref, u_hi_ref, d_hi_ref)

    @pl.when(active)
    def _():
        x = xs_ref[:, :d_model]
        t = (x * _rms_scale(x) * nrm_ref[...]).astype(BF16)
        y = None
        for k in range(2):
            gu = _dot(t, wgu_ref[k])
            gate = gu[:, :d_expert]
            hdn = (gate * _sigmoid(gate) * gu[:, d_expert:]).astype(BF16)
            y_k = xs_ref[:, d_model + k:d_model + k + 1] * _dot(hdn, wd_ref[k])
            y = y_k if y is None else y + y_k
        ys_ref[...] = y

    @pl.when(jnp.logical_not(active))
    def _():
        ys_ref[...] = jnp.zeros_like(ys_ref)


def _experts(xs, n_used, tile_lo, tile_hi, m_norm, layer, w_gate, w_up, w_down):
    n_rows = xs.shape[0]
    D = m_norm.shape[0]
    d_expert = w_gate.shape[-1]
    TM = MOE_TM
    n_tiles = n_rows // TM
    in_w = lambda which: (lambda i, nu, lo, hi: (layer, which(lo, hi)[i], 0, 0))
    pick_lo = lambda lo, hi: lo
    pick_hi = lambda lo, hi: hi
    grid_spec = pltpu.PrefetchScalarGridSpec(
        num_scalar_prefetch=3,
        grid=(n_tiles,),
        in_specs=[
            pl.BlockSpec((TM, D + LANES), lambda i, nu, lo, hi: (i, 0)),
            pl.BlockSpec((1, D), lambda i, nu, lo, hi: (0, 0)),
            pl.BlockSpec((1, 1, D, d_expert), in_w(pick_lo)),
            pl.BlockSpec((1, 1, D, d_expert), in_w(pick_lo)),
            pl.BlockSpec((1, 1, d_expert, D), in_w(pick_lo)),
            pl.BlockSpec((1, 1, D, d_expert), in_w(pick_hi)),
            pl.BlockSpec((1, 1, D, d_expert), in_w(pick_hi)),
            pl.BlockSpec((1, 1, d_expert, D), in_w(pick_hi)),
        ],
        out_specs=pl.BlockSpec((TM, D), lambda i, nu, lo, hi: (i, 0)),
        scratch_shapes=[pltpu.VMEM((2, D, 2 * d_expert), BF16), pltpu.VMEM((2, d_expert, D), BF16)],
    )
    return pl.pallas_call(
        functools.partial(_experts_kernel, d_model=D, d_expert=d_expert),
        grid_spec=grid_spec,
        out_shape=jax.ShapeDtypeStruct((n_rows, D), F32),
        compiler_params=pltpu.CompilerParams(dimension_semantics=("arbitrary",), vmem_limit_bytes=VMEM_LIMIT),
        name="moe_experts",
    )(n_used, tile_lo, tile_hi, xs, m_norm.reshape(1, D), w_gate, w_up, w_down, w_gate, w_up, w_down)


def _combine_norm_kernel(pos_ref, pos_next_ref, h_ref, nrm_ref, ys_ref, out_ref, buf_ref, sem):
    i = pl.program_id(0)
    n = pl.num_programs(0)
    R, D = h_ref.shape
    G = R // SUBLANES
    slot = i % 2

    def row_copy(p_ref, s, g, j):
        src = p_ref[0, 0, g * SUBLANES + j]
        return pltpu.make_async_copy(ys_ref.at[pl.ds(src, 1)], buf_ref.at[s, g, pl.ds(j, 1)], sem.at[s])

    def issue(p_ref, s):
        def body(g, c):
            for j in range(SUBLANES):
                row_copy(p_ref, s, g, j).start()
            return c
        lax.fori_loop(0, G, body, 0)

    @pl.when(i == 0)
    def _():
        issue(pos_ref, slot)

    @pl.when(i < n - 1)
    def _():
        issue(pos_next_ref, 1 - slot)

    def drain(g, c):
        for j in range(SUBLANES):
            row_copy(pos_ref, slot, g, j).wait()
        return c

    lax.fori_loop(0, G, drain, 0)

    out = h_ref[...] + buf_ref[slot].reshape(R, D)
    out_ref[...] = out * _rms_scale(out) * nrm_ref[...]


def _combine_norm(h, pos, ys, norm):
    T, D = h.shape
    R = MOE_ROWS
    n_t = T // R
    pos3 = pos.reshape(n_t, 1, R)
    return pl.pallas_call(
        _combine_norm_kernel,
        grid=(n_t,),
        in_specs=[
            pl.BlockSpec((1, 1, R), lambda i: (i, 0, 0), memory_space=pltpu.SMEM),
            pl.BlockSpec((1, 1, R), lambda i: (jnp.minimum(i + 1, n_t - 1), 0, 0), memory_space=pltpu.SMEM),
            pl.BlockSpec((R, D), lambda i: (i, 0)),
            pl.BlockSpec((1, D), lambda i: (0, 0)),
            pl.BlockSpec(memory_space=pl.ANY),
        ],
        out_specs=pl.BlockSpec((R, D), lambda i: (i, 0)),
        out_shape=jax.ShapeDtypeStruct((T, D), F32),
        scratch_shapes=[pltpu.VMEM((2, R // SUBLANES, SUBLANES, D), F32), pltpu.SemaphoreType.DMA((2,))],
        compiler_params=pltpu.CompilerParams(dimension_semantics=("arbitrary",), vmem_limit_bytes=VMEM_LIMIT),
        name="moe_combine",
    )(pos3, pos3, h, norm.reshape(1, D), ys)


def _class_experts():
    lo, hi = [], []
    for g in range(N_GROUPS):
        for a in range(EXPERTS_PER_GROUP):
            for b in range(a + 1, EXPERTS_PER_GROUP):
                lo.append(g * EXPERTS_PER_GROUP + a)
                hi.append(g * EXPERTS_PER_GROUP + b)
    return jnp.asarray(lo, jnp.int32), jnp.asarray(hi, jnp.int32)


def _moe_sorted(h, layer, m_norm, w_group, b_group, w_expert, b_expert, w_gate, w_up, w_down):
    T, D = h.shape
    TM = MOE_TM
    idx, wgt, cnt = _router(h, m_norm, w_group, b_group, w_expert, b_expert)

    counts = cnt[:N_CLASSES, 0]
    padded = ((counts + TM - 1) // TM) * TM
    ends = jnp.cumsum(padded)
    starts = ends - padded
    n_rows = T + N_CLASSES * TM
    n_tiles = n_rows // TM
    tile_start = jnp.arange(n_tiles, dtype=jnp.int32) * TM
    tile_class = jnp.minimum(
        jnp.sum((ends[None, :] <= tile_start[:, None]).astype(jnp.int32), axis=1), N_CLASSES - 1)
    class_lo, class_hi = _class_experts()
    n_used = (ends[-1] // TM).astype(jnp.int32).reshape(1)
    cls, pos = idx[:, 0, :], idx[:, 1, :]
    for c in range(N_CLASSES):
        pos = pos + jnp.where(cls == c, starts[c], 0)
    pos = pos.astype(jnp.int32)

    partial_tail = jnp.where(counts % TM != 0, ends - TM, -1)
    unused = n_used[0] + jnp.arange(N_CLASSES, dtype=jnp.int32)
    unused = jnp.where(unused < n_tiles, unused * TM, -1)
    fill = jnp.concatenate([partial_tail, unused]).astype(jnp.int32)

    xs = _dispatch(h, wgt, pos, fill, n_rows)
    ys = _experts(xs, n_used, class_lo[tile_class], class_hi[tile_class], m_norm, layer, w_gate, w_up, w_down)
    return pos, ys


def _combine_qkv_kernel(pos_ref, pos_next_ref, h_ref, ys_ref, qn_ref, kvn_ref, wq_ref, wkt_ref, wv_ref,
                        hout_ref, q_ref, kt_ref, v_ref, buf_ref, sem):
    i = pl.program_id(0)
    n = pl.num_programs(0)
    R, D = h_ref.shape
    G = R // SUBLANES
    slot = i % 2

    def row_copy(p_ref, s, g, j):
        src = p_ref[0, 0, g * SUBLANES + j]
        return pltpu.make_async_copy(ys_ref.at[pl.ds(src, 1)], buf_ref.at[s, g, pl.ds(j, 1)], sem.at[s])

    @pl.when(i == 0)
    def _():
        def body(g, c):
            for j in range(SUBLANES):
                row_copy(pos_ref, slot, g, j).start()
            return c
        lax.fori_loop(0, G, body, 0)

    def drain(g, c):
        for j in range(SUBLANES):
            row_copy(pos_ref, slot, g, j).wait()
        return c

    lax.fori_loop(0, G, drain, 0)

    h = h_ref[...] + buf_ref[slot].reshape(R, D)
    hout_ref[...] = h
    for g in range(G):
        for j in range(SUBLANES):
            row_copy(pos_next_ref, 1 - slot, g, j).start()
    xhat = h * _rms_scale(h)
    uq = (xhat * qn_ref[...]).astype(BF16)
    ukv = (xhat * kvn_ref[...]).astype(BF16)
    q_ref[...] = _dot(uq, wq_ref[...]).astype(BF16)
    v_ref[...] = _dot(ukv, wv_ref[...]).astype(BF16)
    kt_ref[0] = _dot_nt(wkt_ref[...], ukv).astype(BF16)

    @pl.when(i == n - 1)
    def _():
        def body(g, c):
            for j in range(SUBLANES):
                row_copy(pos_next_ref, 1 - slot, g, j).wait()
            return c
        lax.fori_loop(0, G, body, 0)


def _combine_qkv(h, pos, ys, batch, seq, q_norm, kv_norm, w_q, w_kv, scale):
    T, D = h.shape
    sb_dim = w_q.shape[1]
    R = MOE_ROWS
    assert seq % R == 0
    n_t = T // R
    n_s = seq // R
    pos3 = pos.reshape(n_t, 1, R)
    w_qs = (w_q * scale).astype(BF16)
    w_kt = w_kv[:, :sb_dim].T.astype(BF16)
    w_v = w_kv[:, sb_dim:].astype(BF16)
    const = lambda i: (0, 0)
    return pl.pallas_call(
        _combine_qkv_kernel,
        grid=(n_t,),
        in_specs=[
            pl.BlockSpec((1, 1, R), lambda i: (i, 0, 0), memory_space=pltpu.SMEM),
            pl.BlockSpec((1, 1, R), lambda i: (jnp.minimum(i + 1, n_t - 1), 0, 0), memory_space=pltpu.SMEM),
            pl.BlockSpec((R, D), lambda i: (i, 0)),
            pl.BlockSpec(memory_space=pl.ANY),
            pl.BlockSpec((1, D), const),
            pl.BlockSpec((1, D), const),
            pl.BlockSpec((D, sb_dim), const),
            pl.BlockSpec((sb_dim, D), const),
            pl.BlockSpec((D, sb_dim), const),
        ],
        out_specs=[
            pl.BlockSpec((R, D), lambda i: (i, 0)),
            pl.BlockSpec((R, sb_dim), lambda i: (i, 0)),
            pl.BlockSpec((1, sb_dim, R), lambda i: (i // n_s, 0, i % n_s)),
            pl.BlockSpec((R, sb_dim), lambda i: (i, 0)),
        ],
        out_shape=[
            jax.ShapeDtypeStruct((T, D), F32),
            jax.ShapeDtypeStruct((T, sb_dim), BF16),
            jax.ShapeDtypeStruct((batch, sb_dim, seq), BF16),
            jax.ShapeDtypeStruct((T, sb_dim), BF16),
        ],
        scratch_shapes=[pltpu.VMEM((2, R // SUBLANES, SUBLANES, D), F32), pltpu.SemaphoreType.DMA((2,))],
        compiler_params=pltpu.CompilerParams(dimension_semantics=("arbitrary",), vmem_limit_bytes=VMEM_LIMIT),
        name="combine_qkv",
    )(pos3, pos3, h, ys, q_norm.reshape(1, D), kv_norm.reshape(1, D), w_qs, w_kt, w_v)


def _sb_attention_kernel(q_ref, kt_ref, v_ref, o_ref, *, head_dim):
    S = q_ref.shape[0]
    T = ATT_TILE
    n_q = S // T
    n_heads = LANES // head_dim
    lane_q = lax.broadcasted_iota(jnp.int32, (T, LANES), 1)
    trow = lax.broadcasted_iota(jnp.int32, (T, T), 0)
    scol = lax.broadcasted_iota(jnp.int32, (T, T), 1)
    strictly_before = scol < trow
    suffix = jnp.where(trow >= scol, 1.0, 0.0).astype(BF16)
    in_head = [(lane_q >= hd * head_dim) & (lane_q < (hd + 1) * head_dim) for hd in range(n_heads)]

    ksq = jnp.square(kt_ref[0].astype(F32))
    k_max = [jnp.sqrt(jnp.max(jnp.sum(ksq[hd * head_dim:(hd + 1) * head_dim], axis=0, keepdims=True),
                              axis=1, keepdims=True)) for hd in range(n_heads)]

    def tile(q_h, k0, acc, run, mask):
        z = _dot(q_h, kt_ref[0, :, pl.ds(k0, T)])
        sp = jnp.maximum(z, jnp.log2(1.0 + jnp.exp2(jnp.minimum(z, ATT_EXP2_CAP))))
        if mask is not None:
            sp = jnp.where(mask, sp, 0.0)
        within = _dot(sp.astype(BF16), suffix)
        a = jnp.exp2(z - within - run)
        if mask is not None:
            a = jnp.where(mask, a, 0.0)
        acc = acc + _dot(a.astype(BF16), v_ref[pl.ds(k0, T), :])
        return acc, run + within[:, 0:1]

    def all_zero_from_here(runs, z_bound):
        slack = runs[0] - z_bound[0]
        for r, zb in zip(runs[1:], z_bound[1:]):
            slack = jnp.minimum(slack, r - zb)
        return (jnp.min(slack) > ATT_ZERO_MARGIN).astype(jnp.int32)

    def block_head(qi, with_left):
        q0 = qi * T if isinstance(qi, int) else pl.multiple_of(qi * T, T)
        q_pair = q_ref[pl.ds(q0, T), :]
        qsq = jnp.square(q_pair.astype(F32))
        q_hs = [jnp.where(m, q_pair, jnp.zeros_like(q_pair)) for m in in_head]
        z_bound = [jnp.sqrt(jnp.sum(jnp.where(m, qsq, 0.0), axis=-1, keepdims=True)) * km
                   for m, km in zip(in_head, k_max)]
        st = [tile(q_h, q0, jnp.zeros((T, LANES), F32), jnp.zeros((T, 1), F32), strictly_before) for q_h in q_hs]
        if with_left:
            st = [tile(q_h, pl.multiple_of(q0 - T, T), a, r, None) for q_h, (a, r) in zip(q_hs, st)]
        accs = [a for a, _ in st]
        runs = [r for _, r in st]
        return q0, q_hs, z_bound, accs, runs, all_zero_from_here(runs, z_bound)

    def block_tail(qi, with_left, q0, q_hs, z_bound, accs, runs, done):
        def cond(c):
            return (c[0] >= 0) & (c[1] == 0)

        def body(c):
            j, _, accs, runs = c
            k0 = pl.multiple_of(j * T, T)
            st = [tile(q_h, k0, a, r, None) for q_h, a, r in zip(q_hs, accs, runs)]
            accs = [a for a, _ in st]
            runs = [r for _, r in st]
            return j - 1, all_zero_from_here(runs, z_bound), accs, runs

        if with_left:
            _, _, accs, _ = lax.while_loop(cond, body, (qi - 2, done, accs, runs))
        result = accs[0]
        for m, a in zip(in_head[1:], accs[1:]):
            result = jnp.where(m, a, result)
        o_ref[pl.ds(q0, T), :] = result.astype(o_ref.dtype)

    def q_blocks(qis, with_left):
        heads = [block_head(qi, with_left) for qi in qis]
        for qi, hd in zip(qis, heads):
            block_tail(qi, with_left, *hd)

    q_blocks([0], False)

    def block_pair(p, carry):
        q_blocks([2 * p + 1, 2 * p + 2], True)
        return carry

    lax.fori_loop(0, (n_q - 1) // 2, block_pair, 0)
    if (n_q - 1) % 2:
        q_blocks([n_q - 1], True)


def _sb_attention(q, kt, v, batch, seq, head_dim):
    T, sb_dim = q.shape
    assert LANES % head_dim == 0 and seq % ATT_TILE == 0
    n_p = sb_dim // LANES
    return pl.pallas_call(
        functools.partial(_sb_attention_kernel, head_dim=head_dim),
        grid=(batch, n_p),
        in_specs=[
            pl.BlockSpec((seq, LANES), lambda b, p: (b, p)),
            pl.BlockSpec((1, LANES, seq), lambda b, p: (b, p, 0)),
            pl.BlockSpec((seq, LANES), lambda b, p: (b, p)),
        ],
        out_specs=pl.BlockSpec((seq, LANES), lambda b, p: (b, p)),
        out_shape=jax.ShapeDtypeStruct((T, sb_dim), BF16),
        compiler_params=pltpu.CompilerParams(
            dimension_semantics=("arbitrary", "arbitrary"), vmem_limit_bytes=VMEM_LIMIT),
        name="sb_attention",
    )(q, kt, v)


def _out_proj_kernel(o_ref, w_ref, h_ref, out_ref):
    out_ref[...] = h_ref[...] + _dot(o_ref[...], w_ref[...])


def _out_proj(o, w_out, h):
    T, D = h.shape
    R = PROJ_ROWS
    return pl.pallas_call(
        _out_proj_kernel,
        grid=(T // R,),
        in_specs=[
            pl.BlockSpec((R, o.shape[1]), lambda i: (i, 0)),
            pl.BlockSpec(w_out.shape, lambda i: (0, 0)),
            pl.BlockSpec((R, D), lambda i: (i, 0)),
        ],
        out_specs=pl.BlockSpec((R, D), lambda i: (i, 0)),
        out_shape=jax.ShapeDtypeStruct((T, D), F32),
        compiler_params=pltpu.CompilerParams(dimension_semantics=("arbitrary",), vmem_limit_bytes=VMEM_LIMIT),
        name="out_proj",
    )(o, w_out.astype(BF16), h)


def kernel(x, a_norm, a_w_in, a_w_gk2, a_b_gk2, a_o_norm, a_w_out, kv_norm, w_kv, b_norm, b_w_q, b_w_out,
           m_norm, m_w_group, m_b_group, m_w_expert, m_b_expert, m_w_gate, m_w_up, m_w_down, final_norm):
    B, S, D = x.shape
    assert a_norm.shape[0] == 1 and b_norm.shape[0] == 1 and m_norm.shape[0] == 2
    head_dim = b_w_q.shape[2] // SB_HEADS
    h = x.reshape(B * S, D)

    def moe_sorted(h, layer):
        return _moe_sorted(h, layer, m_norm[layer], m_w_group[layer], m_b_group[layer], m_w_expert[layer],
                           m_b_expert[layer], m_w_gate, m_w_up, m_w_down)

    h = _gla_layer(h, B, S, a_norm[0], a_w_in[0], a_w_gk2[0], a_b_gk2[0], a_o_norm[0], a_w_out[0])
    pos, ys = moe_sorted(h, 0)
    h, q, kt, v = _combine_qkv(h, pos, ys, B, S, b_norm[0], kv_norm, b_w_q[0], w_kv,
                               math.log2(math.e) / math.sqrt(head_dim))
    o = _sb_attention(q, kt, v, B, S, head_dim)
    h = _out_proj(o, b_w_out[0], h)
    pos, ys = moe_sorted(h, 1)
    h = _combine_norm(h, pos, ys, final_norm)
    return h.reshape(B, S, D)
```

```python
import functools
import math

import jax
import jax.numpy as jnp
from jax import lax
from jax.experimental import pallas as pl
from jax.experimental.pallas import tpu as pltpu

RMS_EPS = 1e-6

GLA_HEADS = 4
GLA_CHUNK = 64
CHUNK_SHIFT = GLA_CHUNK.bit_length() - 1
GATE_NORMALIZER = 16.0
SB_HEADS = 16
N_GROUPS = 4
EXPERTS_PER_GROUP = 4
N_EXPERTS = N_GROUPS * EXPERTS_PER_GROUP
PAIRS_PER_GROUP = EXPERTS_PER_GROUP * (EXPERTS_PER_GROUP - 1) // 2
N_CLASSES = N_GROUPS * PAIRS_PER_GROUP

LANES = 128
SUBLANES = 8
ROUTE_ROWS = 32
EXPERT_ROW0 = 8
VMEM_LIMIT = 56 * 1024 * 1024

GLA_ROWS = 512
GLA_BLOCK = 256
PROJ_ROWS = 512
ATT_TILE = 256
ATT_ZERO_MARGIN = 160.0
ATT_EXP2_CAP = 100.0
MOE_ROWS = 512
MOE_TM = 256

BF16 = jnp.bfloat16
F32 = jnp.float32


def _dot(a, b):
    return jnp.dot(a, b, preferred_element_type=F32)


def _dot_nt(a, b):
    return lax.dot_general(a, b, (((1,), (1,)), ((), ())), preferred_element_type=F32)


def _split_dot(m01, x, left):
    hi = x.astype(BF16)
    lo = (x - hi.astype(F32)).astype(BF16)
    if left:
        return _dot(m01, hi) + _dot(m01, lo)
    return _dot(hi, m01) + _dot(lo, m01)


def _rms_scale(x):
    return lax.rsqrt(jnp.mean(x * x, axis=-1, keepdims=True) + RMS_EPS)


def _log_sigmoid(x):
    return jnp.minimum(x, 0.0) - jnp.log(1.0 + jnp.exp(-jnp.abs(x)))


def _softplus(x):
    return jnp.maximum(x, 0.0) + jnp.log(1.0 + jnp.exp(-jnp.abs(x)))


def _sigmoid(x):
    return 1.0 / (1.0 + jnp.exp(-x))


def _gla_block(h, states, nrm_ref, wrow_ref, wkt_ref, wgk2_ref, bgk2_ref, onorm_ref, wout_ref,
               *, dk, dv, qk_dim, v_dim, log_scale):
    R = h.shape[0]
    n_chunks = R // GLA_CHUNK
    u = (h * _rms_scale(h) * nrm_ref[...]).astype(BF16)

    proj = _dot(u, wrow_ref[...])
    q = proj[:, :qk_dim]
    v = proj[:, qk_dim:qk_dim + v_dim].astype(BF16)
    g = proj[:, qk_dim + v_dim:qk_dim + 2 * v_dim]
    lr = proj[:, qk_dim + 2 * v_dim:].astype(BF16)
    kt = _dot_nt(wkt_ref[...], u)

    gk = _log_sigmoid(_dot(lr, wgk2_ref[...]) + bgk2_ref[...]) * (1.0 / GATE_NORMALIZER)
    gkt = gk.T

    row = lax.broadcasted_iota(jnp.int32, (R, R), 0)
    col = lax.broadcasted_iota(jnp.int32, (R, R), 1)
    same_chunk = (row >> CHUNK_SHIFT) == (col >> CHUNK_SHIFT)
    causal = same_chunk & (col <= row)
    lbd = jnp.where(causal, 1.0, 0.0).astype(BF16)
    ubd = jnp.where(same_chunk & (row <= col), 1.0, 0.0).astype(BF16)
    after = jnp.where(same_chunk & (row > col), 1.0, 0.0).astype(BF16)

    b = _split_dot(lbd, gk, left=True)
    bt = _split_dot(ubd, gkt, left=False)
    tail_t = _split_dot(after, gkt, left=False)

    q_dec = (q * jnp.exp(b + log_scale)).astype(BF16)
    k_inv_t = (kt * jnp.exp(-bt)).astype(BF16)
    k_end_t = (kt * jnp.exp(tail_t)).astype(BF16)
    chunk_decay_t = jnp.exp(bt + tail_t)

    lane_chunk = lax.broadcasted_iota(jnp.int32, (dk, R), 1) >> CHUNK_SHIFT

    acc = h
    new_states = []
    for hd in range(GLA_HEADS):
        ks = slice(hd * dk, (hd + 1) * dk)
        vs = slice(hd * dv, (hd + 1) * dv)
        qd_h = q_dec[:, ks]
        v_h = v[:, vs]
        att = _dot(qd_h, k_inv_t[ks, :])
        att = jnp.where(causal, att, 0.0).astype(BF16)
        o_h = _dot(att, v_h)
        kend_h = k_end_t[ks, :]
        state = states[hd]
        inter = []
        for c in range(n_chunks):
            rows = slice(c * GLA_CHUNK, (c + 1) * GLA_CHUNK)
            inter.append(_dot(qd_h[rows], state.astype(BF16)))
            kend_c = jnp.where(lane_chunk == c, kend_h, jnp.zeros_like(kend_h))
            decay = chunk_decay_t[ks, c * GLA_CHUNK:c * GLA_CHUNK + 1]
            state = decay * state + _dot(kend_c, v_h)
        new_states.append(state)
        o_h = o_h + jnp.concatenate(inter, axis=0)
        o_h = o_h * _rms_scale(o_h) * onorm_ref[...]
        g_h = g[:, vs]
        o_h = o_h * (g_h * _sigmoid(g_h))
        acc = acc + _dot(o_h.astype(BF16), wout_ref[vs, :])
    return acc, new_states


def _gla_kernel(h_ref, *refs, **dims):
    *w_refs, out_ref, state_ref = refs

    @pl.when(pl.program_id(1) == 0)
    def _():
        state_ref[...] = jnp.zeros_like(state_ref)

    states = [state_ref[hd] for hd in range(GLA_HEADS)]
    for blk in range(h_ref.shape[0] // GLA_BLOCK):
        rows = slice(blk * GLA_BLOCK, (blk + 1) * GLA_BLOCK)
        out, states = _gla_block(h_ref[rows, :], states, *w_refs, **dims)
        out_ref[rows, :] = out
    for hd in range(GLA_HEADS):
        state_ref[hd] = states[hd]


def _gla_layer(h, batch, seq, a_norm, w_in, w_gk2, b_gk2, o_norm, w_out):
    T, D = h.shape
    rank, qk_dim = w_gk2.shape
    v_dim = w_out.shape[0]
    dk = qk_dim // GLA_HEADS
    dv = v_dim // GLA_HEADS
    R = GLA_ROWS
    assert seq % R == 0 and R % GLA_CHUNK == 0 and rank <= LANES
    assert w_in.shape[1] == 2 * qk_dim + 2 * v_dim + rank
    n_s = seq // R

    o_q, o_k, o_v, o_lr, o_g = 0, qk_dim, 2 * qk_dim, 2 * qk_dim + v_dim, 2 * qk_dim + v_dim + rank
    w_lr = jnp.pad(w_in[:, o_lr:o_lr + rank], ((0, 0), (0, LANES - rank)))
    w_row = jnp.concatenate([w_in[:, o_q:o_k], w_in[:, o_v:o_lr], w_in[:, o_g:], w_lr], axis=1).astype(BF16)
    w_kt = w_in[:, o_k:o_v].T.astype(BF16)
    w_gk2p = jnp.pad(w_gk2, ((0, LANES - rank), (0, 0))).astype(BF16)
    const = lambda b, s: (0, 0)
    kern = functools.partial(_gla_kernel, dk=dk, dv=dv, qk_dim=qk_dim, v_dim=v_dim,
                             log_scale=math.log(dk ** -0.5))
    return pl.pallas_call(
        kern,
        grid=(batch, n_s),
        in_specs=[
            pl.BlockSpec((R, D), lambda b, s: (b * n_s + s, 0)),
            pl.BlockSpec((1, D), const),
            pl.BlockSpec(w_row.shape, const),
            pl.BlockSpec(w_kt.shape, const),
            pl.BlockSpec(w_gk2p.shape, const),
            pl.BlockSpec((1, qk_dim), const),
            pl.BlockSpec((1, dv), const),
            pl.BlockSpec((v_dim, D), const),
        ],
        out_specs=pl.BlockSpec((R, D), lambda b, s: (b * n_s + s, 0)),
        out_shape=jax.ShapeDtypeStruct((T, D), F32),
        scratch_shapes=[pltpu.VMEM((GLA_HEADS, dk, dv), F32)],
        compiler_params=pltpu.CompilerParams(
            dimension_semantics=("arbitrary", "arbitrary"), vmem_limit_bytes=VMEM_LIMIT),
        name="gla_layer",
    )(h, a_norm.reshape(1, D), w_row, w_kt, w_gk2p, b_gk2.reshape(1, qk_dim), o_norm.reshape(1, dv),
      w_out.astype(BF16))


def _router_kernel(h_ref, nrm_ref, wt_ref, bt_ref, upper_ref, idx_ref, wgt_ref, cnt_ref, carry_ref):
    R = h_ref.shape[0]

    @pl.when(pl.program_id(0) == 0)
    def _():
        carry_ref[...] = jnp.zeros_like(carry_ref)

    h = h_ref[...]
    t = h * _rms_scale(h) * nrm_ref[...]
    t_hi = t.astype(BF16)
    t_lo = (t - t_hi.astype(F32)).astype(BF16)
    p = _dot_nt(wt_ref[...], t_hi)
    logits = p[:ROUTE_ROWS] + p[ROUTE_ROWS:] + _dot_nt(wt_ref[:ROUTE_ROWS], t_lo) + bt_ref[:, 0:1]
    row = lax.broadcasted_iota(jnp.int32, (ROUTE_ROWS, R), 0)
    neg = jnp.float32(-jnp.inf)

    def first_argmax(vals):
        m = jnp.max(vals, axis=0, keepdims=True)
        i = jnp.min(jnp.where(vals == m, row, ROUTE_ROWS), axis=0, keepdims=True)
        return m, i

    gl = jnp.where(row < N_GROUPS, logits, neg)
    g_max, g_idx = first_argmax(gl)
    g_w = 1.0 / jnp.sum(jnp.exp(gl - g_max), axis=0, keepdims=True)

    base = EXPERT_ROW0 + g_idx * EXPERTS_PER_GROUP
    el = jnp.where((row >= base) & (row < base + EXPERTS_PER_GROUP), logits, neg)
    m1, i1 = first_argmax(el)
    m2, i2 = first_argmax(jnp.where(row == i1, neg, el))
    r = jnp.exp(m2 - m1)
    w1 = g_w / (1.0 + r)
    w2 = g_w * r / (1.0 + r)
    l1 = i1 - base
    l2 = i2 - base
    first_is_lo = l1 < l2
    lo = jnp.minimum(l1, l2)
    hi = jnp.maximum(l1, l2)
    cls = g_idx * PAIRS_PER_GROUP + ((lo * (2 * EXPERTS_PER_GROUP - 1 - lo)) >> 1) + (hi - lo - 1)
    w_lo = jnp.where(first_is_lo, w1, w2)
    w_hi = jnp.where(first_is_lo, w2, w1)

    onehot = jnp.where(row == cls, 1.0, 0.0)
    before = _dot(onehot.astype(BF16), upper_ref[...]) + carry_ref[:, 0:1]
    rank = jnp.sum(jnp.where(row == cls, before, 0.0), axis=0, keepdims=True)
    total = carry_ref[:, 0:1] + jnp.sum(onehot, axis=1, keepdims=True)
    carry_ref[...] = jnp.broadcast_to(total, carry_ref.shape)
    cnt_ref[...] = jnp.broadcast_to(total, cnt_ref.shape).astype(jnp.int32)

    row8 = lax.broadcasted_iota(jnp.int32, (SUBLANES, R), 0)
    idx_ref[0] = jnp.where(row8 == 0, cls, jnp.where(row8 == 1, rank.astype(jnp.int32), 0))
    w_rows = jnp.where(row == 0, w_lo, jnp.where(row == 1, w_hi, 0.0))
    w_rows = jnp.concatenate([w_rows, jnp.zeros((LANES - ROUTE_ROWS, R), F32)], axis=0)
    wgt_ref[...] = w_rows.T


def _router(h, m_norm, w_group, b_group, w_expert, b_expert):
    T, D = h.shape
    R = MOE_ROWS
    assert T % R == 0 and N_CLASSES <= ROUTE_ROWS
    pad_g = EXPERT_ROW0 - N_GROUPS
    pad_e = ROUTE_ROWS - EXPERT_ROW0 - N_EXPERTS
    w_t = jnp.pad(jnp.concatenate([w_group.T, jnp.zeros((pad_g, D), F32), w_expert.T], axis=0), ((0, pad_e), (0, 0)))
    w_t_hi = w_t.astype(BF16)
    w_t = jnp.concatenate([w_t_hi, (w_t - w_t_hi.astype(F32)).astype(BF16)], axis=0)
    b_t = jnp.pad(jnp.concatenate([b_group, jnp.zeros((pad_g,), F32), b_expert]), (0, pad_e))
    b_t = jnp.broadcast_to(b_t[:, None], (ROUTE_ROWS, LANES))
    idx = jnp.arange(R)
    upper = (idx[:, None] < idx[None, :]).astype(BF16)
    const = lambda i: (0, 0)
    return pl.pallas_call(
        _router_kernel,
        grid=(T // R,),
        in_specs=[
            pl.BlockSpec((R, D), lambda i: (i, 0)),
            pl.BlockSpec((1, D), const),
            pl.BlockSpec((2 * ROUTE_ROWS, D), const),
            pl.BlockSpec((ROUTE_ROWS, LANES), const),
            pl.BlockSpec((R, R), const),
        ],
        out_specs=[
            pl.BlockSpec((1, SUBLANES, R), lambda i: (i, 0, 0)),
            pl.BlockSpec((R, LANES), lambda i: (i, 0)),
            pl.BlockSpec((ROUTE_ROWS, LANES), const),
        ],
        out_shape=[
            jax.ShapeDtypeStruct((T // R, SUBLANES, R), jnp.int32),
            jax.ShapeDtypeStruct((T, LANES), F32),
            jax.ShapeDtypeStruct((ROUTE_ROWS, LANES), jnp.int32),
        ],
        scratch_shapes=[pltpu.VMEM((ROUTE_ROWS, LANES), F32)],
        compiler_params=pltpu.CompilerParams(dimension_semantics=("arbitrary",), vmem_limit_bytes=VMEM_LIMIT),
        name="moe_router",
    )(h, m_norm.reshape(1, D), w_t, b_t, upper)


def _dispatch_kernel(pos_ref, pos_prev_ref, fill_ref, h_ref, wgt_ref, xs_ref, row_ref, zero_ref, sem, fill_sem):
    i = pl.program_id(0)
    n = pl.num_programs(0)
    R, D = h_ref.shape
    G = R // SUBLANES
    slot = i % 2

    @pl.when(i == 0)
    def _():
        zero_ref[...] = jnp.zeros_like(zero_ref)
        tile_rows = zero_ref.shape[0]

        def fill_copy(k):
            first = pl.multiple_of(jnp.maximum(fill_ref[0, 0, k], 0), tile_rows)
            return pltpu.make_async_copy(zero_ref, xs_ref.at[pl.ds(first, tile_rows)], fill_sem)

        def start(k, c):
            @pl.when(fill_ref[0, 0, k] >= 0)
            def _():
                fill_copy(k).start()
            return c

        def wait(k, c):
            @pl.when(fill_ref[0, 0, k] >= 0)
            def _():
                fill_copy(k).wait()
            return c

        lax.fori_loop(0, fill_ref.shape[2], start, 0)
        lax.fori_loop(0, fill_ref.shape[2], wait, 0)

    row_ref[slot, :, :, :D] = h_ref[...].reshape(G, SUBLANES, D)
    row_ref[slot, :, :, D:] = wgt_ref[...].reshape(G, SUBLANES, LANES)

    def row_copy(p_ref, s, g, j):
        dst = p_ref[0, 0, g * SUBLANES + j]
        return pltpu.make_async_copy(row_ref.at[s, g, pl.ds(j, 1)], xs_ref.at[pl.ds(dst, 1)], sem.at[s])

    for g in range(G):
        for j in range(SUBLANES):
            row_copy(pos_ref, slot, g, j).start()

    def drain(p_ref, s):
        def body(g, c):
            for j in range(SUBLANES):
                row_copy(p_ref, s, g, j).wait()
            return c
        lax.fori_loop(0, G, body, 0)

    @pl.when(i > 0)
    def _():
        drain(pos_prev_ref, 1 - slot)

    @pl.when(i == n - 1)
    def _():
        drain(pos_ref, slot)


def _dispatch(h, wgt, pos, fill, n_rows):
    T, D = h.shape
    R = MOE_ROWS
    n_t = T // R
    pos3 = pos.reshape(n_t, 1, R)
    fill3 = fill.reshape(1, 1, -1)
    return pl.pallas_call(
        _dispatch_kernel,
        grid=(n_t,),
        in_specs=[
            pl.BlockSpec((1, 1, R), lambda i: (i, 0, 0), memory_space=pltpu.SMEM),
            pl.BlockSpec((1, 1, R), lambda i: (jnp.maximum(i - 1, 0), 0, 0), memory_space=pltpu.SMEM),
            pl.BlockSpec(fill3.shape, lambda i: (0, 0, 0), memory_space=pltpu.SMEM),
            pl.BlockSpec((R, D), lambda i: (i, 0)),
            pl.BlockSpec((R, LANES), lambda i: (i, 0)),
        ],
        out_specs=pl.BlockSpec(memory_space=pl.ANY),
        out_shape=jax.ShapeDtypeStruct((n_rows, D + LANES), F32),
        scratch_shapes=[pltpu.VMEM((2, R // SUBLANES, SUBLANES, D + LANES), F32),
                        pltpu.VMEM((MOE_TM, D + LANES), F32),
                        pltpu.SemaphoreType.DMA((2,)), pltpu.SemaphoreType.DMA],
        compiler_params=pltpu.CompilerParams(dimension_semantics=("arbitrary",), vmem_limit_bytes=VMEM_LIMIT),
        name="moe_dispatch",
    )(pos3, pos3, fill3, h, wgt)


def _experts_kernel(used_ref, lo_ref, hi_ref, xs_ref, nrm_ref, g_lo_ref, u_lo_ref, d_lo_ref, g_hi_ref, u_hi_ref,
                    d_hi_ref, ys_ref, wgu_ref, wd_ref, *, d_model, d_expert):
    i = pl.program_id(0)
    active = i < used_ref[0]
    prev = jnp.maximum(i - 1, 0)

    def refresh(k, e_ref, g_ref, u_ref, d_ref):
        @pl.when(active & ((i == 0) | (e_ref[i] != e_ref[prev])))
        def _():
            wgu_ref[k, :, :d_expert] = g_ref[0, 0].astype(BF16)
            wgu_ref[k, :, d_expert:] = u_ref[0, 0].astype(BF16)
            wd_ref[k] = d_ref[0, 0].astype(BF16)

    refresh(0, lo_ref, g_lo_ref, u_lo_ref, d_lo_ref)
    refresh(1, hi_ref, g_hi_ref, u_hi_ref, d_hi_ref)

    @pl.when(active)
    def _():
        x = xs_ref[:, :d_model]
        t = (x * _rms_scale(x) * nrm_ref[...]).astype(BF16)
        y = None
        for k in range(2):
            gu = _dot(t, wgu_ref[k])
            gate = gu[:, :d_expert]
            hdn = (gate * _sigmoid(gate) * gu[:, d_expert:]).astype(BF16)
            y_k = xs_ref[:, d_model + k:d_model + k + 1] * _dot(hdn, wd_ref[k])
            y = y_k if y is None else y + y_k
        ys_ref[...] = y

    @pl.when(jnp.logical_not(active))
    def _():
        ys_ref[...] = jnp.zeros_like(ys_ref)


def _experts(xs, n_used, tile_lo, tile_hi, m_norm, layer, w_gate, w_up, w_down):
    n_rows = xs.shape[0]
    D = m_norm.shape[0]
    d_expert = w_gate.shape[-1]
    TM = MOE_TM
    n_tiles = n_rows // TM
    in_w = lambda which: (lambda i, nu, lo, hi: (layer, which(lo, hi)[i], 0, 0))
    pick_lo = lambda lo, hi: lo
    pick_hi = lambda lo, hi: hi
    grid_spec = pltpu.PrefetchScalarGridSpec(
        num_scalar_prefetch=3,
        grid=(n_tiles,),
        in_specs=[
            pl.BlockSpec((TM, D + LANES), lambda i, nu, lo, hi: (i, 0)),
            pl.BlockSpec((1, D), lambda i, nu, lo, hi: (0, 0)),
            pl.BlockSpec((1, 1, D, d_expert), in_w(pick_lo)),
            pl.BlockSpec((1, 1, D, d_expert), in_w(pick_lo)),
            pl.BlockSpec((1, 1, d_expert, D), in_w(pick_lo)),
            pl.BlockSpec((1, 1, D, d_expert), in_w(pick_hi)),
            pl.BlockSpec((1, 1, D, d_expert), in_w(pick_hi)),
            pl.BlockSpec((1, 1, d_expert, D), in_w(pick_hi)),
        ],
        out_specs=pl.BlockSpec((TM, D), lambda i, nu, lo, hi: (i, 0)),
        scratch_shapes=[pltpu.VMEM((2, D, 2 * d_expert), BF16), pltpu.VMEM((2, d_expert, D), BF16)],
    )
    return pl.pallas_call(
        functools.partial(_experts_kernel, d_model=D, d_expert=d_expert),
        grid_spec=grid_spec,
        out_shape=jax.ShapeDtypeStruct((n_rows, D), F32),
        compiler_params=pltpu.CompilerParams(dimension_semantics=("arbitrary",), vmem_limit_bytes=VMEM_LIMIT),
        name="moe_experts",
    )(n_used, tile_lo, tile_hi, xs, m_norm.reshape(1, D), w_gate, w_up, w_down, w_gate, w_up, w_down)


def _combine_norm_kernel(pos_ref, pos_next_ref, h_ref, nrm_ref, ys_ref, out_ref, buf_ref, sem):
    i = pl.program_id(0)
    n = pl.num_programs(0)
    R, D = h_ref.shape
    G = R // SUBLANES
    slot = i % 2

    def row_copy(p_ref, s, g, j):
        src = p_ref[0, 0, g * SUBLANES + j]
        return pltpu.make_async_copy(ys_ref.at[pl.ds(src, 1)], buf_ref.at[s, g, pl.ds(j, 1)], sem.at[s])

    def issue(p_ref, s):
        def body(g, c):
            for j in range(SUBLANES):
                row_copy(p_ref, s, g, j).start()
            return c
        lax.fori_loop(0, G, body, 0)

    @pl.when(i == 0)
    def _():
        issue(pos_ref, slot)

    @pl.when(i < n - 1)
    def _():
        for g in range(G):
            for j in range(SUBLANES):
                row_copy(pos_next_ref, 1 - slot, g, j).start()

    def drain(g, c):
        for j in range(SUBLANES):
            row_copy(pos_ref, slot, g, j).wait()
        return c

    lax.fori_loop(0, G, drain, 0)

    out = h_ref[...] + buf_ref[slot].reshape(R, D)
    out_ref[...] = out * _rms_scale(out) * nrm_ref[...]


def _combine_norm(h, pos, ys, norm):
    T, D = h.shape
    R = MOE_ROWS
    n_t = T // R
    pos3 = pos.reshape(n_t, 1, R)
    return pl.pallas_call(
        _combine_norm_kernel,
        grid=(n_t,),
        in_specs=[
            pl.BlockSpec((1, 1, R), lambda i: (i, 0, 0), memory_space=pltpu.SMEM),
            pl.BlockSpec((1, 1, R), lambda i: (jnp.minimum(i + 1, n_t - 1), 0, 0), memory_space=pltpu.SMEM),
            pl.BlockSpec((R, D), lambda i: (i, 0)),
            pl.BlockSpec((1, D), lambda i: (0, 0)),
            pl.BlockSpec(memory_space=pl.ANY),
        ],
        out_specs=pl.BlockSpec((R, D), lambda i: (i, 0)),
        out_shape=jax.ShapeDtypeStruct((T, D), F32),
        scratch_shapes=[pltpu.VMEM((2, R // SUBLANES, SUBLANES, D), F32), pltpu.SemaphoreType.DMA((2,))],
        compiler_params=pltpu.CompilerParams(dimension_semantics=("arbitrary",), vmem_limit_bytes=VMEM_LIMIT),
        name="moe_combine",
    )(pos3, pos3, h, norm.reshape(1, D), ys)


def _class_experts():
    lo, hi = [], []
    for g in range(N_GROUPS):
        for a in range(EXPERTS_PER_GROUP):
            for b in range(a + 1, EXPERTS_PER_GROUP):
                lo.append(g * EXPERTS_PER_GROUP + a)
                hi.append(g * EXPERTS_PER_GROUP + b)
    return jnp.asarray(lo, jnp.int32), jnp.asarray(hi, jnp.int32)


def _moe_sorted(h, layer, m_norm, w_group, b_group, w_expert, b_expert, w_gate, w_up, w_down):
    T, D = h.shape
    TM = MOE_TM
    idx, wgt, cnt = _router(h, m_norm, w_group, b_group, w_expert, b_expert)

    counts = cnt[:N_CLASSES, 0]
    padded = ((counts + TM - 1) // TM) * TM
    ends = jnp.cumsum(padded)
    starts = ends - padded
    n_rows = T + N_CLASSES * TM
    n_tiles = n_rows // TM
    tile_start = jnp.arange(n_tiles, dtype=jnp.int32) * TM
    tile_class = jnp.minimum(
        jnp.sum((ends[None, :] <= tile_start[:, None]).astype(jnp.int32), axis=1), N_CLASSES - 1)
    class_lo, class_hi = _class_experts()
    n_used = (ends[-1] // TM).astype(jnp.int32).reshape(1)
    cls, pos = idx[:, 0, :], idx[:, 1, :]
    for c in range(N_CLASSES):
        pos = pos + jnp.where(cls == c, starts[c], 0)
    pos = pos.astype(jnp.int32)

    partial_tail = jnp.where(counts % TM != 0, ends - TM, -1)
    unused = n_used[0] + jnp.arange(N_CLASSES, dtype=jnp.int32)
    unused = jnp.where(unused < n_tiles, unused * TM, -1)
    fill = jnp.concatenate([partial_tail, unused]).astype(jnp.int32)

    xs = _dispatch(h, wgt, pos, fill, n_rows)
    ys = _experts(xs, n_used, class_lo[tile_class], class_hi[tile_class], m_norm, layer, w_gate, w_up, w_down)
    return pos, ys


def _combine_qkv_kernel(pos_ref, pos_next_ref, h_ref, ys_ref, qn_ref, kvn_ref, wq_ref, wkt_ref, wv_ref,
                        hout_ref, q_ref, kt_ref, v_ref, buf_a, buf_b, sem):
    i = pl.program_id(0)
    n = pl.num_programs(0)
    R, D = h_ref.shape
    G = R // SUBLANES

    def row_copy(p_ref, buf, s, g, j):
        src = p_ref[0, 0, g * SUBLANES + j]
        return pltpu.make_async_copy(ys_ref.at[pl.ds(src, 1)], buf.at[g, pl.ds(j, 1)], sem.at[s])

    @pl.when(i == 0)
    def _():
        def body(g, c):
            for j in range(SUBLANES):
                row_copy(pos_ref, buf_a, 0, g, j).start()
            return c
        lax.fori_loop(0, G, body, 0)

    def step(cur, cur_s, nxt, nxt_s):
        def drain(g, c):
            for j in range(SUBLANES):
                row_copy(pos_ref, cur, cur_s, g, j).wait()
            return c
        lax.fori_loop(0, G, drain, 0)

        for g in range(G):
            for j in range(SUBLANES):
                row_copy(pos_next_ref, nxt, nxt_s, g, j).start()
        h = h_ref[...] + cur[...].reshape(R, D)
        hout_ref[...] = h
        xhat = h * _rms_scale(h)
        uq = (xhat * qn_ref[...]).astype(BF16)
        ukv = (xhat * kvn_ref[...]).astype(BF16)
        q_ref[...] = _dot(uq, wq_ref[...]).astype(BF16)
        v_ref[...] = _dot(ukv, wv_ref[...]).astype(BF16)
        kt_ref[0] = _dot_nt(wkt_ref[...], ukv).astype(BF16)

        @pl.when(i == n - 1)
        def _():
            def body(g, c):
                for j in range(SUBLANES):
                    row_copy(pos_next_ref, nxt, nxt_s, g, j).wait()
                return c
            lax.fori_loop(0, G, body, 0)

    @pl.when(i % 2 == 0)
    def _():
        step(buf_a, 0, buf_b, 1)

    @pl.when(i % 2 == 1)
    def _():
        step(buf_b, 1, buf_a, 0)


def _combine_qkv(h, pos, ys, batch, seq, q_norm, kv_norm, w_q, w_kv, scale):
    T, D = h.shape
    sb_dim = w_q.shape[1]
    R = MOE_ROWS
    assert seq % R == 0
    n_t = T // R
    n_s = seq // R
    pos3 = pos.reshape(n_t, 1, R)
    w_qs = (w_q * scale).astype(BF16)
    w_kt = w_kv[:, :sb_dim].T.astype(BF16)
    w_v = w_kv[:, sb_dim:].astype(BF16)
    const = lambda i: (0, 0)
    return pl.pallas_call(
        _combine_qkv_kernel,
        grid=(n_t,),
        in_specs=[
            pl.BlockSpec((1, 1, R), lambda i: (i, 0, 0), memory_space=pltpu.SMEM),
            pl.BlockSpec((1, 1, R), lambda i: (jnp.minimum(i + 1, n_t - 1), 0, 0), memory_space=pltpu.SMEM),
            pl.BlockSpec((R, D), lambda i: (i, 0)),
            pl.BlockSpec(memory_space=pl.ANY),
            pl.BlockSpec((1, D), const),
            pl.BlockSpec((1, D), const),
            pl.BlockSpec((D, sb_dim), const),
            pl.BlockSpec((sb_dim, D), const),
            pl.BlockSpec((D, sb_dim), const),
        ],
        out_specs=[
            pl.BlockSpec((R, D), lambda i: (i, 0)),
            pl.BlockSpec((R, sb_dim), lambda i: (i, 0)),
            pl.BlockSpec((1, sb_dim, R), lambda i: (i // n_s, 0, i % n_s)),
            pl.BlockSpec((R, sb_dim), lambda i: (i, 0)),
        ],
        out_shape=[
            jax.ShapeDtypeStruct((T, D), F32),
            jax.ShapeDtypeStruct((T, sb_dim), BF16),
            jax.ShapeDtypeStruct((batch, sb_dim, seq), BF16),
            jax.ShapeDtypeStruct((T, sb_dim), BF16),
        ],
        scratch_shapes=[pltpu.VMEM((R // SUBLANES, SUBLANES, D), F32), pltpu.VMEM((R // SUBLANES, SUBLANES, D), F32),
                        pltpu.SemaphoreType.DMA((2,))],
        compiler_params=pltpu.CompilerParams(dimension_semantics=("arbitrary",), vmem_limit_bytes=VMEM_LIMIT),
        name="combine_qkv",
    )(pos3, pos3, h, ys, q_norm.reshape(1, D), kv_norm.reshape(1, D), w_qs, w_kt, w_v)


def _sb_attention_kernel(q_ref, kt_ref, v_ref, o_ref, *, head_dim):
    S = q_ref.shape[0]
    T = ATT_TILE
    n_q = S // T
    n_heads = LANES // head_dim
    lane_q = lax.broadcasted_iota(jnp.int32, (T, LANES), 1)
    trow = lax.broadcasted_iota(jnp.int32, (T, T), 0)
    scol = lax.broadcasted_iota(jnp.int32, (T, T), 1)
    strictly_before = scol < trow
    suffix = jnp.where(trow >= scol, 1.0, 0.0).astype(BF16)
    in_head = [(lane_q >= hd * head_dim) & (lane_q < (hd + 1) * head_dim) for hd in range(n_heads)]

    ksq = jnp.square(kt_ref[0].astype(F32))
    k_max = [jnp.sqrt(jnp.max(jnp.sum(ksq[hd * head_dim:(hd + 1) * head_dim], axis=0, keepdims=True),
                              axis=1, keepdims=True)) for hd in range(n_heads)]

    def tile(q_h, k0, acc, run, mask):
        z = _dot(q_h, kt_ref[0, :, pl.ds(k0, T)])
        sp = jnp.maximum(z, jnp.log2(1.0 + jnp.exp2(jnp.minimum(z, ATT_EXP2_CAP))))
        if mask is not None:
            sp = jnp.where(mask, sp, 0.0)
        within = _dot(sp.astype(BF16), suffix)
        a = jnp.exp2(z - within - run)
        if mask is not None:
            a = jnp.where(mask, a, 0.0)
        acc = acc + _dot(a.astype(BF16), v_ref[pl.ds(k0, T), :])
        return acc, run + within[:, 0:1]

    def all_zero_from_here(runs, z_bound):
        slack = runs[0] - z_bound[0]
        for r, zb in zip(runs[1:], z_bound[1:]):
            slack = jnp.minimum(slack, r - zb)
        return (jnp.min(slack) > ATT_ZERO_MARGIN).astype(jnp.int32)

    def block_head(qi, with_left):
        q0 = qi * T if isinstance(qi, int) else pl.multiple_of(qi * T, T)
        q_pair = q_ref[pl.ds(q0, T), :]
        qsq = jnp.square(q_pair.astype(F32))
        q_hs = [jnp.where(m, q_pair, jnp.zeros_like(q_pair)) for m in in_head]
        z_bound = [jnp.sqrt(jnp.sum(jnp.where(m, qsq, 0.0), axis=-1, keepdims=True)) * km
                   for m, km in zip(in_head, k_max)]
        st = [tile(q_h, q0, jnp.zeros((T, LANES), F32), jnp.zeros((T, 1), F32), strictly_before) for q_h in q_hs]
        if with_left:
            st = [tile(q_h, pl.multiple_of(q0 - T, T), a, r, None) for q_h, (a, r) in zip(q_hs, st)]
        accs = [a for a, _ in st]
        runs = [r for _, r in st]
        return q0, q_hs, z_bound, accs, runs, all_zero_from_here(runs, z_bound)

    def block_tail(qi, with_left, q0, q_hs, z_bound, accs, runs, done):
        def cond(c):
            return (c[0] >= 0) & (c[1] == 0)

        def body(c):
            j, _, accs, runs = c
            k0 = pl.multiple_of(j * T, T)
            st = [tile(q_h, k0, a, r, None) for q_h, a, r in zip(q_hs, accs, runs)]
            accs = [a for a, _ in st]
            runs = [r for _, r in st]
            return j - 1, all_zero_from_here(runs, z_bound), accs, runs

        if with_left:
            _, _, accs, _ = lax.while_loop(cond, body, (qi - 2, done, accs, runs))
        result = accs[0]
        for m, a in zip(in_head[1:], accs[1:]):
            result = jnp.where(m, a, result)
        o_ref[pl.ds(q0, T), :] = result.astype(o_ref.dtype)

    def q_blocks(qis, with_left):
        heads = [block_head(qi, with_left) for qi in qis]
        for qi, hd in zip(qis, heads):
            block_tail(qi, with_left, *hd)

    q_blocks([0], False)

    def block_pair(p, carry):
        q_blocks([2 * p + 1, 2 * p + 2], True)
        return carry

    lax.fori_loop(0, (n_q - 1) // 2, block_pair, 0)
    if (n_q - 1) % 2:
        q_blocks([n_q - 1], True)


def _sb_attention(q, kt, v, batch, seq, head_dim):
    T, sb_dim = q.shape
    assert LANES % head_dim == 0 and seq % ATT_TILE == 0
    n_p = sb_dim // LANES
    return pl.pallas_call(
        functools.partial(_sb_attention_kernel, head_dim=head_dim),
        grid=(batch, n_p),
        in_specs=[
            pl.BlockSpec((seq, LANES), lambda b, p: (b, p)),
            pl.BlockSpec((1, LANES, seq), lambda b, p: (b, p, 0)),
            pl.BlockSpec((seq, LANES), lambda b, p: (b, p)),
        ],
        out_specs=pl.BlockSpec((seq, LANES), lambda b, p: (b, p)),
        out_shape=jax.ShapeDtypeStruct((T, sb_dim), BF16),
        compiler_params=pltpu.CompilerParams(
            dimension_semantics=("arbitrary", "arbitrary"), vmem_limit_bytes=VMEM_LIMIT),
        name="sb_attention",
    )(q, kt, v)


def _out_proj_kernel(o_ref, w_ref, h_ref, out_ref):
    out_ref[...] = h_ref[...] + _dot(o_ref[...], w_ref[...])


def _out_proj(o, w_out, h):
    T, D = h.shape
    R = PROJ_ROWS
    return pl.pallas_call(
        _out_proj_kernel,
        grid=(T // R,),
        in_specs=[
            pl.BlockSpec((R, o.shape[1]), lambda i: (i, 0)),
            pl.BlockSpec(w_out.shape, lambda i: (0, 0)),
            pl.BlockSpec((R, D), lambda i: (i, 0)),
        ],
        out_specs=pl.BlockSpec((R, D), lambda i: (i, 0)),
        out_shape=jax.ShapeDtypeStruct((T, D), F32),
        compiler_params=pltpu.CompilerParams(dimension_semantics=("arbitrary",), vmem_limit_bytes=VMEM_LIMIT),
        name="out_proj",
    )(o, w_out.astype(BF16), h)


def kernel(x, a_norm, a_w_in, a_w_gk2, a_b_gk2, a_o_norm, a_w_out, kv_norm, w_kv, b_norm, b_w_q, b_w_out,
           m_norm, m_w_group, m_b_group, m_w_expert, m_b_expert, m_w_gate, m_w_up, m_w_down, final_norm):
    B, S, D = x.shape
    assert a_norm.shape[0] == 1 and b_norm.shape[0] == 1 and m_norm.shape[0] == 2
    head_dim = b_w_q.shape[2] // SB_HEADS
    h = x.reshape(B * S, D)

    def moe_sorted(h, layer):
        return _moe_sorted(h, layer, m_norm[layer], m_w_group[layer], m_b_group[layer], m_w_expert[layer],
                           m_b_expert[layer], m_w_gate, m_w_up, m_w_down)

    h = _gla_layer(h, B, S, a_norm[0], a_w_in[0], a_w_gk2[0], a_b_gk2[0], a_o_norm[0], a_w_out[0])
    pos, ys = moe_sorted(h, 0)
    h, q, kt, v = _combine_qkv(h, pos, ys, B, S, b_norm[0], kv_norm, b_w_q[0], w_kv,
                               math.log2(math.e) / math.sqrt(head_dim))
    o = _sb_attention(q, kt, v, B, S, head_dim)
    h = _out_proj(o, b_w_out[0], h)
    pos, ys = moe_sorted(h, 1)
    h = _combine_norm(h, pos, ys, final_norm)
    return h.reshape(B, S, D)
```

```python
import functools
import math

import jax
import jax.numpy as jnp
from jax import lax
from jax.experimental import pallas as pl
from jax.experimental.pallas import tpu as pltpu

RMS_EPS = 1e-6

GLA_HEADS = 4
GLA_CHUNK = 64
CHUNK_SHIFT = GLA_CHUNK.bit_length() - 1
GATE_NORMALIZER = 16.0
SB_HEADS = 16
N_GROUPS = 4
EXPERTS_PER_GROUP = 4
N_EXPERTS = N_GROUPS * EXPERTS_PER_GROUP
PAIRS_PER_GROUP = EXPERTS_PER_GROUP * (EXPERTS_PER_GROUP - 1) // 2
N_CLASSES = N_GROUPS * PAIRS_PER_GROUP

LANES = 128
SUBLANES = 8
ROUTE_ROWS = 32
EXPERT_ROW0 = 8
VMEM_LIMIT = 56 * 1024 * 1024

GLA_ROWS = 512
GLA_BLOCK = 256
PROJ_ROWS = 512
ATT_TILE = 256
ATT_ZERO_MARGIN = 160.0
ATT_EXP2_CAP = 100.0
MOE_ROWS = 512
MOE_TM = 512

BF16 = jnp.bfloat16
F32 = jnp.float32


def _dot(a, b):
    return jnp.dot(a, b, preferred_element_type=F32)


def _dot_nt(a, b):
    return lax.dot_general(a, b, (((1,), (1,)), ((), ())), preferred_element_type=F32)


def _split_dot(x, m01):
    hi = x.astype(BF16)
    lo = (x - hi.astype(F32)).astype(BF16)
    return _dot(hi, m01) + _dot(lo, m01)


def _rms_scale(x):
    return lax.rsqrt(jnp.mean(x * x, axis=-1, keepdims=True) + RMS_EPS)


def _log_sigmoid(x):
    return jnp.minimum(x, 0.0) - jnp.log(1.0 + jnp.exp(-jnp.abs(x)))


def _softplus(x):
    return jnp.maximum(x, 0.0) + jnp.log(1.0 + jnp.exp(-jnp.abs(x)))


def _sigmoid(x):
    return 1.0 / (1.0 + jnp.exp(-x))


def _gla_block(h, states, nrm_ref, wrow_ref, wkt_ref, wgk2_ref, bgk2_ref, onorm_ref, wout_ref,
               *, dk, dv, qk_dim, v_dim, log_scale):
    R = h.shape[0]
    n_chunks = R // GLA_CHUNK
    u = (h * _rms_scale(h) * nrm_ref[...]).astype(BF16)

    proj = _dot(u, wrow_ref[...])
    q = proj[:, :qk_dim]
    v = proj[:, qk_dim:qk_dim + v_dim].astype(BF16)
    g = proj[:, qk_dim + v_dim:qk_dim + 2 * v_dim]
    lr = proj[:, qk_dim + 2 * v_dim:].astype(BF16)
    kt = _dot_nt(wkt_ref[...], u)

    gk = _log_sigmoid(_dot(lr, wgk2_ref[...]) + bgk2_ref[...]) * (1.0 / GATE_NORMALIZER)
    gkt = gk.T

    row = lax.broadcasted_iota(jnp.int32, (R, R), 0)
    col = lax.broadcasted_iota(jnp.int32, (R, R), 1)
    same_chunk = (row >> CHUNK_SHIFT) == (col >> CHUNK_SHIFT)
    causal = same_chunk & (col <= row)
    upto = jnp.where(same_chunk & (row <= col), 1.0, 0.0).astype(BF16)
    after = jnp.where(same_chunk & (row > col), 1.0, 0.0).astype(BF16)

    bt = _split_dot(gkt, upto)
    tail_t = _split_dot(gkt, after)
    b = bt.T

    q_dec = (q * jnp.exp(b + log_scale)).astype(BF16)
    k_inv_t = (kt * jnp.exp(-bt)).astype(BF16)
    k_end_t = (kt * jnp.exp(tail_t)).astype(BF16)
    chunk_decay_t = jnp.exp(bt + tail_t)

    lane_chunk = lax.broadcasted_iota(jnp.int32, (dk, R), 1) >> CHUNK_SHIFT

    acc = h
    new_states = []
    for hd in range(GLA_HEADS):
        ks = slice(hd * dk, (hd + 1) * dk)
        vs = slice(hd * dv, (hd + 1) * dv)
        qd_h = q_dec[:, ks]
        v_h = v[:, vs]
        att = _dot(qd_h, k_inv_t[ks, :])
        att = jnp.where(causal, att, 0.0).astype(BF16)
        kend_h = k_end_t[ks, :]
        kend_c = [jnp.where(lane_chunk == c, kend_h, jnp.zeros_like(kend_h)) for c in range(n_chunks)]
        ov = _dot(jnp.concatenate([att] + kend_c, axis=0), v_h)
        o_h = ov[:R]
        state = states[hd]
        inter = []
        for c in range(n_chunks):
            rows = slice(c * GLA_CHUNK, (c + 1) * GLA_CHUNK)
            inter.append(_dot(qd_h[rows], state.astype(BF16)))
            decay = chunk_decay_t[ks, c * GLA_CHUNK:c * GLA_CHUNK + 1]
            state = decay * state + ov[R + c * dk:R + (c + 1) * dk]
        new_states.append(state)
        o_h = o_h + jnp.concatenate(inter, axis=0)
        o_h = o_h * _rms_scale(o_h) * onorm_ref[...]
        g_h = g[:, vs]
        o_h = o_h * (g_h * _sigmoid(g_h))
        acc = acc + _dot(o_h.astype(BF16), wout_ref[vs, :])
    return acc, new_states


def _gla_kernel(h_ref, *refs, **dims):
    *w_refs, out_ref, state_ref = refs

    @pl.when(pl.program_id(1) == 0)
    def _():
        state_ref[...] = jnp.zeros_like(state_ref)

    states = [state_ref[hd] for hd in range(GLA_HEADS)]
    for blk in range(h_ref.shape[0] // GLA_BLOCK):
        rows = slice(blk * GLA_BLOCK, (blk + 1) * GLA_BLOCK)
        out, states = _gla_block(h_ref[rows, :], states, *w_refs, **dims)
        out_ref[rows, :] = out
    for hd in range(GLA_HEADS):
        state_ref[hd] = states[hd]


def _gla_layer(h, batch, seq, a_norm, w_in, w_gk2, b_gk2, o_norm, w_out):
    T, D = h.shape
    rank, qk_dim = w_gk2.shape
    v_dim = w_out.shape[0]
    dk = qk_dim // GLA_HEADS
    dv = v_dim // GLA_HEADS
    R = GLA_ROWS
    assert seq % R == 0 and R % GLA_CHUNK == 0 and rank <= LANES
    assert w_in.shape[1] == 2 * qk_dim + 2 * v_dim + rank
    n_s = seq // R

    o_q, o_k, o_v, o_lr, o_g = 0, qk_dim, 2 * qk_dim, 2 * qk_dim + v_dim, 2 * qk_dim + v_dim + rank
    w_lr = jnp.pad(w_in[:, o_lr:o_lr + rank], ((0, 0), (0, LANES - rank)))
    w_row = jnp.concatenate([w_in[:, o_q:o_k], w_in[:, o_v:o_lr], w_in[:, o_g:], w_lr], axis=1).astype(BF16)
    w_kt = w_in[:, o_k:o_v].T.astype(BF16)
    w_gk2p = jnp.pad(w_gk2, ((0, LANES - rank), (0, 0))).astype(BF16)
    const = lambda b, s: (0, 0)
    kern = functools.partial(_gla_kernel, dk=dk, dv=dv, qk_dim=qk_dim, v_dim=v_dim,
                             log_scale=math.log(dk ** -0.5))
    return pl.pallas_call(
        kern,
        grid=(batch, n_s),
        in_specs=[
            pl.BlockSpec((R, D), lambda b, s: (b * n_s + s, 0)),
            pl.BlockSpec((1, D), const),
            pl.BlockSpec(w_row.shape, const),
            pl.BlockSpec(w_kt.shape, const),
            pl.BlockSpec(w_gk2p.shape, const),
            pl.BlockSpec((1, qk_dim), const),
            pl.BlockSpec((1, dv), const),
            pl.BlockSpec((v_dim, D), const),
        ],
        out_specs=pl.BlockSpec((R, D), lambda b, s: (b * n_s + s, 0)),
        out_shape=jax.ShapeDtypeStruct((T, D), F32),
        scratch_shapes=[pltpu.VMEM((GLA_HEADS, dk, dv), F32)],
        compiler_params=pltpu.CompilerParams(
            dimension_semantics=("arbitrary", "arbitrary"), vmem_limit_bytes=VMEM_LIMIT),
        name="gla_layer",
    )(h, a_norm.reshape(1, D), w_row, w_kt, w_gk2p, b_gk2.reshape(1, qk_dim), o_norm.reshape(1, dv),
      w_out.astype(BF16))


def _router_kernel(h_ref, nrm_ref, wt_ref, bt_ref, upper_ref, idx_ref, wgt_ref, cnt_ref, carry_ref):
    R = h_ref.shape[0]

    @pl.when(pl.program_id(0) == 0)
    def _():
        carry_ref[...] = jnp.zeros_like(carry_ref)

    h = h_ref[...]
    t = h * _rms_scale(h) * nrm_ref[...]
    t_hi = t.astype(BF16)
    t_lo = (t - t_hi.astype(F32)).astype(BF16)
    p = _dot_nt(wt_ref[...], t_hi)
    logits = p[:ROUTE_ROWS] + p[ROUTE_ROWS:] + _dot_nt(wt_ref[:ROUTE_ROWS], t_lo) + bt_ref[:, 0:1]
    row = lax.broadcasted_iota(jnp.int32, (ROUTE_ROWS, R), 0)
    neg = jnp.float32(-jnp.inf)

    def first_argmax(vals):
        m = jnp.max(vals, axis=0, keepdims=True)
        i = jnp.min(jnp.where(vals == m, row, ROUTE_ROWS), axis=0, keepdims=True)
        return m, i

    gl = jnp.where(row < N_GROUPS, logits, neg)
    g_max, g_idx = first_argmax(gl)
    g_w = 1.0 / jnp.sum(jnp.exp(gl - g_max), axis=0, keepdims=True)

    base = EXPERT_ROW0 + g_idx * EXPERTS_PER_GROUP
    el = jnp.where((row >= base) & (row < base + EXPERTS_PER_GROUP), logits, neg)
    m1, i1 = first_argmax(el)
    m2, i2 = first_argmax(jnp.where(row == i1, neg, el))
    r = jnp.exp(m2 - m1)
    w1 = g_w / (1.0 + r)
    w2 = g_w * r / (1.0 + r)
    l1 = i1 - base
    l2 = i2 - base
    first_is_lo = l1 < l2
    lo = jnp.minimum(l1, l2)
    hi = jnp.maximum(l1, l2)
    cls = g_idx * PAIRS_PER_GROUP + ((lo * (2 * EXPERTS_PER_GROUP - 1 - lo)) >> 1) + (hi - lo - 1)
    w_lo = jnp.where(first_is_lo, w1, w2)
    w_hi = jnp.where(first_is_lo, w2, w1)

    onehot = jnp.where(row == cls, 1.0, 0.0)
    before = _dot(onehot.astype(BF16), upper_ref[...]) + carry_ref[:, 0:1]
    rank = jnp.sum(jnp.where(row == cls, before, 0.0), axis=0, keepdims=True)
    total = carry_ref[:, 0:1] + jnp.sum(onehot, axis=1, keepdims=True)
    carry_ref[...] = jnp.broadcast_to(total, carry_ref.shape)
    cnt_ref[...] = jnp.broadcast_to(total, cnt_ref.shape).astype(jnp.int32)

    row8 = lax.broadcasted_iota(jnp.int32, (SUBLANES, R), 0)
    idx_ref[0] = jnp.where(row8 == 0, cls, jnp.where(row8 == 1, rank.astype(jnp.int32), 0))
    w_rows = jnp.where(row == 0, w_lo, jnp.where(row == 1, w_hi, 0.0))
    w_rows = jnp.concatenate([w_rows, jnp.zeros((LANES - ROUTE_ROWS, R), F32)], axis=0)
    wgt_ref[...] = w_rows.T


def _router(h, m_norm, w_group, b_group, w_expert, b_expert):
    T, D = h.shape
    R = MOE_ROWS
    assert T % R == 0 and N_CLASSES <= ROUTE_ROWS
    pad_g = EXPERT_ROW0 - N_GROUPS
    pad_e = ROUTE_ROWS - EXPERT_ROW0 - N_EXPERTS
    w_t = jnp.pad(jnp.concatenate([w_group.T, jnp.zeros((pad_g, D), F32), w_expert.T], axis=0), ((0, pad_e), (0, 0)))
    w_t_hi = w_t.astype(BF16)
    w_t = jnp.concatenate([w_t_hi, (w_t - w_t_hi.astype(F32)).astype(BF16)], axis=0)
    b_t = jnp.pad(jnp.concatenate([b_group, jnp.zeros((pad_g,), F32), b_expert]), (0, pad_e))
    b_t = jnp.broadcast_to(b_t[:, None], (ROUTE_ROWS, LANES))
    idx = jnp.arange(R)
    upper = (idx[:, None] < idx[None, :]).astype(BF16)
    const = lambda i: (0, 0)
    return pl.pallas_call(
        _router_kernel,
        grid=(T // R,),
        in_specs=[
            pl.BlockSpec((R, D), lambda i: (i, 0)),
            pl.BlockSpec((1, D), const),
            pl.BlockSpec((2 * ROUTE_ROWS, D), const),
            pl.BlockSpec((ROUTE_ROWS, LANES), const),
            pl.BlockSpec((R, R), const),
        ],
        out_specs=[
            pl.BlockSpec((1, SUBLANES, R), lambda i: (i, 0, 0)),
            pl.BlockSpec((R, LANES), lambda i: (i, 0)),
            pl.BlockSpec((ROUTE_ROWS, LANES), const),
        ],
        out_shape=[
            jax.ShapeDtypeStruct((T // R, SUBLANES, R), jnp.int32),
            jax.ShapeDtypeStruct((T, LANES), F32),
            jax.ShapeDtypeStruct((ROUTE_ROWS, LANES), jnp.int32),
        ],
        scratch_shapes=[pltpu.VMEM((ROUTE_ROWS, LANES), F32)],
        compiler_params=pltpu.CompilerParams(dimension_semantics=("arbitrary",), vmem_limit_bytes=VMEM_LIMIT),
        name="moe_router",
    )(h, m_norm.reshape(1, D), w_t, b_t, upper)


def _dispatch_kernel(pos_ref, pos_prev_ref, fill_ref, h_ref, wgt_ref, xs_ref, row_ref, zero_ref, sem, fill_sem):
    i = pl.program_id(0)
    n = pl.num_programs(0)
    R, D = h_ref.shape
    G = R // SUBLANES
    slot = i % 2

    @pl.when(i == 0)
    def _():
        zero_ref[...] = jnp.zeros_like(zero_ref)
        tile_rows = zero_ref.shape[0]

        def fill_copy(k):
            first = pl.multiple_of(jnp.maximum(fill_ref[0, 0, k], 0), tile_rows)
            return pltpu.make_async_copy(zero_ref, xs_ref.at[pl.ds(first, tile_rows)], fill_sem)

        def start(k, c):
            @pl.when(fill_ref[0, 0, k] >= 0)
            def _():
                fill_copy(k).start()
            return c

        def wait(k, c):
            @pl.when(fill_ref[0, 0, k] >= 0)
            def _():
                fill_copy(k).wait()
            return c

        lax.fori_loop(0, fill_ref.shape[2], start, 0)
        lax.fori_loop(0, fill_ref.shape[2], wait, 0)

    row_ref[slot, :, :, :D] = h_ref[...].reshape(G, SUBLANES, D)
    row_ref[slot, :, :, D:] = wgt_ref[...].reshape(G, SUBLANES, LANES)

    def row_copy(p_ref, s, g, j):
        dst = p_ref[0, 0, g * SUBLANES + j]
        return pltpu.make_async_copy(row_ref.at[s, g, pl.ds(j, 1)], xs_ref.at[pl.ds(dst, 1)], sem.at[s])

    for g in range(G):
        for j in range(SUBLANES):
            row_copy(pos_ref, slot, g, j).start()

    def drain(p_ref, s):
        def body(g, c):
            for j in range(SUBLANES):
                row_copy(p_ref, s, g, j).wait()
            return c
        lax.fori_loop(0, G, body, 0)

    @pl.when(i > 0)
    def _():
        drain(pos_prev_ref, 1 - slot)

    @pl.when(i == n - 1)
    def _():
        drain(pos_ref, slot)


def _dispatch(h, wgt, pos, fill, n_rows):
    T, D = h.shape
    R = MOE_ROWS
    n_t = T // R
    pos3 = pos.reshape(n_t, 1, R)
    fill3 = fill.reshape(1, 1, -1)
    return pl.pallas_call(
        _dispatch_kernel,
        grid=(n_t,),
        in_specs=[
            pl.BlockSpec((1, 1, R), lambda i: (i, 0, 0), memory_space=pltpu.SMEM),
            pl.BlockSpec((1, 1, R), lambda i: (jnp.maximum(i - 1, 0), 0, 0), memory_space=pltpu.SMEM),
            pl.BlockSpec(fill3.shape, lambda i: (0, 0, 0), memory_space=pltpu.SMEM),
            pl.BlockSpec((R, D), lambda i: (i, 0)),
            pl.BlockSpec((R, LANES), lambda i: (i, 0)),
        ],
        out_specs=pl.BlockSpec(memory_space=pl.ANY),
        out_shape=jax.ShapeDtypeStruct((n_rows, D + LANES), F32),
        scratch_shapes=[pltpu.VMEM((2, R // SUBLANES, SUBLANES, D + LANES), F32),
                        pltpu.VMEM((MOE_TM, D + LANES), F32),
                        pltpu.SemaphoreType.DMA((2,)), pltpu.SemaphoreType.DMA],
        compiler_params=pltpu.CompilerParams(dimension_semantics=("arbitrary",), vmem_limit_bytes=VMEM_LIMIT),
        name="moe_dispatch",
    )(pos3, pos3, fill3, h, wgt)


def _experts_kernel(used_ref, lo_ref, hi_ref, xs_ref, nrm_ref, g_lo_ref, u_lo_ref, d_lo_ref, g_hi_ref, u_hi_ref,
                    d_hi_ref, ys_ref, wgu_ref, wd_ref, *, d_model, d_expert):
    i = pl.program_id(0)
    active = i < used_ref[0]
    prev = jnp.maximum(i - 1, 0)

    def refresh(k, e_ref, g_ref, u_ref, d_ref):
        @pl.when(active & ((i == 0) | (e_ref[i] != e_ref[prev])))
        def _():
            wgu_ref[k, :, :d_expert] = g_ref[0, 0].astype(BF16)
            wgu_ref[k, :, d_expert:] = u_ref[0, 0].astype(BF16)
            wd_ref[k] = d_ref[0, 0].astype(BF16)

    refresh(0, lo_ref, g_lo_ref, u_lo_ref, d_lo_ref)
    refresh(1, hi_ref, g_hi_ref, u_hi_ref, d_hi_ref)

    @pl.when(active)
    def _():
        x = xs_ref[:, :d_model]
        t = (x * _rms_scale(x) * nrm_ref[...]).astype(BF16)
        y = None
        for k in range(2):
            gu = _dot(t, wgu_ref[k])
            gate = gu[:, :d_expert]
            hdn = (gate * _sigmoid(gate) * gu[:, d_expert:]).astype(BF16)
            y_k = xs_ref[:, d_model + k:d_model + k + 1] * _dot(hdn, wd_ref[k])
            y = y_k if y is None else y + y_k
        ys_ref[...] = y

    @pl.when(jnp.logical_not(active))
    def _():
        ys_ref[...] = jnp.zeros_like(ys_ref)


def _experts(xs, n_used, tile_lo, tile_hi, m_norm, layer, w_gate, w_up, w_down):
    n_rows = xs.shape[0]
    D = m_norm.shape[0]
    d_expert = w_gate.shape[-1]
    TM = MOE_TM
    n_tiles = n_rows // TM
    in_w = lambda which: (lambda i, nu, lo, hi: (layer, which(lo, hi)[i], 0, 0))
    pick_lo = lambda lo, hi: lo
    pick_hi = lambda lo, hi: hi
    grid_spec = pltpu.PrefetchScalarGridSpec(
        num_scalar_prefetch=3,
        grid=(n_tiles,),
        in_specs=[
            pl.BlockSpec((TM, D + LANES), lambda i, nu, lo, hi: (i, 0)),
            pl.BlockSpec((1, D), lambda i, nu, lo, hi: (0, 0)),
            pl.BlockSpec((1, 1, D, d_expert), in_w(pick_lo)),
            pl.BlockSpec((1, 1, D, d_expert), in_w(pick_lo)),
            pl.BlockSpec((1, 1, d_expert, D), in_w(pick_lo)),
            pl.BlockSpec((1, 1, D, d_expert), in_w(pick_hi)),
            pl.BlockSpec((1, 1, D, d_expert), in_w(pick_hi)),
            pl.BlockSpec((1, 1, d_expert, D), in_w(pick_hi)),
        ],
        out_specs=pl.BlockSpec((TM, D), lambda i, nu, lo, hi: (i, 0)),
        scratch_shapes=[pltpu.VMEM((2, D, 2 * d_expert), BF16), pltpu.VMEM((2, d_expert, D), BF16)],
    )
    return pl.pallas_call(
        functools.partial(_experts_kernel, d_model=D, d_expert=d_expert),
        grid_spec=grid_spec,
        out_shape=jax.ShapeDtypeStruct((n_rows, D), F32),
        compiler_params=pltpu.CompilerParams(dimension_semantics=("arbitrary",), vmem_limit_bytes=VMEM_LIMIT),
        name="moe_experts",
    )(n_used, tile_lo, tile_hi, xs, m_norm.reshape(1, D), w_gate, w_up, w_down, w_gate, w_up, w_down)


def _combine_norm_kernel(pos_ref, pos_next_ref, h_ref, nrm_ref, ys_ref, out_ref, buf_ref, sem):
    i = pl.program_id(0)
    n = pl.num_programs(0)
    R, D = h_ref.shape
    G = R // SUBLANES
    slot = i % 2

    def row_copy(p_ref, s, g, j):
        src = p_ref[0, 0, g * SUBLANES + j]
        return pltpu.make_async_copy(ys_ref.at[pl.ds(src, 1)], buf_ref.at[s, g, pl.ds(j, 1)], sem.at[s])

    def issue(p_ref, s):
        def body(g, c):
            for j in range(SUBLANES):
                row_copy(p_ref, s, g, j).start()
            return c
        lax.fori_loop(0, G, body, 0)

    @pl.when(i == 0)
    def _():
        issue(pos_ref, slot)

    @pl.when(i < n - 1)
    def _():
        for g in range(G):
            for j in range(SUBLANES):
                row_copy(pos_next_ref, 1 - slot, g, j).start()

    def drain(g, c):
        for j in range(SUBLANES):
            row_copy(pos_ref, slot, g, j).wait()
        return c

    lax.fori_loop(0, G, drain, 0)

    out = h_ref[...] + buf_ref[slot].reshape(R, D)
    out_ref[...] = out * _rms_scale(out) * nrm_ref[...]


def _combine_norm(h, pos, ys, norm):
    T, D = h.shape
    R = MOE_ROWS
    n_t = T // R
    pos3 = pos.reshape(n_t, 1, R)
    return pl.pallas_call(
        _combine_norm_kernel,
        grid=(n_t,),
        in_specs=[
            pl.BlockSpec((1, 1, R), lambda i: (i, 0, 0), memory_space=pltpu.SMEM),
            pl.BlockSpec((1, 1, R), lambda i: (jnp.minimum(i + 1, n_t - 1), 0, 0), memory_space=pltpu.SMEM),
            pl.BlockSpec((R, D), lambda i: (i, 0)),
            pl.BlockSpec((1, D), lambda i: (0, 0)),
            pl.BlockSpec(memory_space=pl.ANY),
        ],
        out_specs=pl.BlockSpec((R, D), lambda i: (i, 0)),
        out_shape=jax.ShapeDtypeStruct((T, D), F32),
        scratch_shapes=[pltpu.VMEM((2, R // SUBLANES, SUBLANES, D), F32), pltpu.SemaphoreType.DMA((2,))],
        compiler_params=pltpu.CompilerParams(dimension_semantics=("arbitrary",), vmem_limit_bytes=VMEM_LIMIT),
        name="moe_combine",
    )(pos3, pos3, h, norm.reshape(1, D), ys)


def _class_experts():
    lo, hi = [], []
    for g in range(N_GROUPS):
        for a in range(EXPERTS_PER_GROUP):
            for b in range(a + 1, EXPERTS_PER_GROUP):
                lo.append(g * EXPERTS_PER_GROUP + a)
                hi.append(g * EXPERTS_PER_GROUP + b)
    return jnp.asarray(lo, jnp.int32), jnp.asarray(hi, jnp.int32)


def _moe_sorted(h, layer, m_norm, w_group, b_group, w_expert, b_expert, w_gate, w_up, w_down):
    T, D = h.shape
    TM = MOE_TM
    idx, wgt, cnt = _router(h, m_norm, w_group, b_group, w_expert, b_expert)

    counts = cnt[:N_CLASSES, 0]
    padded = ((counts + TM - 1) // TM) * TM
    ends = jnp.cumsum(padded)
    starts = ends - padded
    n_rows = T + N_CLASSES * TM
    n_tiles = n_rows // TM
    tile_start = jnp.arange(n_tiles, dtype=jnp.int32) * TM
    tile_class = jnp.minimum(
        jnp.sum((ends[None, :] <= tile_start[:, None]).astype(jnp.int32), axis=1), N_CLASSES - 1)
    class_lo, class_hi = _class_experts()
    n_used = (ends[-1] // TM).astype(jnp.int32).reshape(1)
    cls, pos = idx[:, 0, :], idx[:, 1, :]
    for c in range(N_CLASSES):
        pos = pos + jnp.where(cls == c, starts[c], 0)
    pos = pos.astype(jnp.int32)

    partial_tail = jnp.where(counts % TM != 0, ends - TM, -1)
    unused = n_used[0] + jnp.arange(N_CLASSES, dtype=jnp.int32)
    unused = jnp.where(unused < n_tiles, unused * TM, -1)
    fill = jnp.concatenate([partial_tail, unused]).astype(jnp.int32)

    xs = _dispatch(h, wgt, pos, fill, n_rows)
    ys = _experts(xs, n_used, class_lo[tile_class], class_hi[tile_class], m_norm, layer, w_gate, w_up, w_down)
    return pos, ys


def _combine_qkv_kernel(pos_ref, pos_next_ref, h_ref, ys_ref, qn_ref, kvn_ref, wq_ref, wkt_ref, wv_ref,
                        hout_ref, q_ref, kt_ref, v_ref, buf_a, buf_b, sem):
    i = pl.program_id(0)
    n = pl.num_programs(0)
    R, D = h_ref.shape
    G = R // SUBLANES

    def row_copy(p_ref, buf, s, g, j):
        src = p_ref[0, 0, g * SUBLANES + j]
        return pltpu.make_async_copy(ys_ref.at[pl.ds(src, 1)], buf.at[g, pl.ds(j, 1)], sem.at[s])

    @pl.when(i == 0)
    def _():
        def body(g, c):
            for j in range(SUBLANES):
                row_copy(pos_ref, buf_a, 0, g, j).start()
            return c
        lax.fori_loop(0, G, body, 0)

    def step(cur, cur_s, nxt, nxt_s):
        def drain(g, c):
            for j in range(SUBLANES):
                row_copy(pos_ref, cur, cur_s, g, j).wait()
            return c
        lax.fori_loop(0, G, drain, 0)

        for g in range(G):
            for j in range(SUBLANES):
                row_copy(pos_next_ref, nxt, nxt_s, g, j).start()
        h = h_ref[...] + cur[...].reshape(R, D)
        hout_ref[...] = h
        xhat = h * _rms_scale(h)
        uq = (xhat * qn_ref[...]).astype(BF16)
        ukv = (xhat * kvn_ref[...]).astype(BF16)
        q_ref[...] = _dot(uq, wq_ref[...]).astype(BF16)
        v_ref[...] = _dot(ukv, wv_ref[...]).astype(BF16)
        kt_ref[0] = _dot_nt(wkt_ref[...], ukv).astype(BF16)

        @pl.when(i == n - 1)
        def _():
            def body(g, c):
                for j in range(SUBLANES):
                    row_copy(pos_next_ref, nxt, nxt_s, g, j).wait()
                return c
            lax.fori_loop(0, G, body, 0)

    @pl.when(i % 2 == 0)
    def _():
        step(buf_a, 0, buf_b, 1)

    @pl.when(i % 2 == 1)
    def _():
        step(buf_b, 1, buf_a, 0)


def _combine_qkv(h, pos, ys, batch, seq, q_norm, kv_norm, w_q, w_kv, scale):
    T, D = h.shape
    sb_dim = w_q.shape[1]
    R = MOE_ROWS
    assert seq % R == 0
    n_t = T // R
    n_s = seq // R
    pos3 = pos.reshape(n_t, 1, R)
    w_qs = (w_q * scale).astype(BF16)
    w_kt = w_kv[:, :sb_dim].T.astype(BF16)
    w_v = w_kv[:, sb_dim:].astype(BF16)
    const = lambda i: (0, 0)
    return pl.pallas_call(
        _combine_qkv_kernel,
        grid=(n_t,),
        in_specs=[
            pl.BlockSpec((1, 1, R), lambda i: (i, 0, 0), memory_space=pltpu.SMEM),
            pl.BlockSpec((1, 1, R), lambda i: (jnp.minimum(i + 1, n_t - 1), 0, 0), memory_space=pltpu.SMEM),
            pl.BlockSpec((R, D), lambda i: (i, 0)),
            pl.BlockSpec(memory_space=pl.ANY),
            pl.BlockSpec((1, D), const),
            pl.BlockSpec((1, D), const),
            pl.BlockSpec((D, sb_dim), const),
            pl.BlockSpec((sb_dim, D), const),
            pl.BlockSpec((D, sb_dim), const),
        ],
        out_specs=[
            pl.BlockSpec((R, D), lambda i: (i, 0)),
            pl.BlockSpec((R, sb_dim), lambda i: (i, 0)),
            pl.BlockSpec((1, sb_dim, R), lambda i: (i // n_s, 0, i % n_s)),
            pl.BlockSpec((R, sb_dim), lambda i: (i, 0)),
        ],
        out_shape=[
            jax.ShapeDtypeStruct((T, D), F32),
            jax.ShapeDtypeStruct((T, sb_dim), BF16),
            jax.ShapeDtypeStruct((batch, sb_dim, seq), BF16),
            jax.ShapeDtypeStruct((T, sb_dim), BF16),
        ],
        scratch_shapes=[pltpu.VMEM((R // SUBLANES, SUBLANES, D), F32), pltpu.VMEM((R // SUBLANES, SUBLANES, D), F32),
                        pltpu.SemaphoreType.DMA((2,))],
        compiler_params=pltpu.CompilerParams(dimension_semantics=("arbitrary",), vmem_limit_bytes=VMEM_LIMIT),
        name="combine_qkv",
    )(pos3, pos3, h, ys, q_norm.reshape(1, D), kv_norm.reshape(1, D), w_qs, w_kt, w_v)


def _sb_attention_kernel(q_ref, kt_ref, v_ref, o_ref, *, head_dim):
    S = q_ref.shape[0]
    T = ATT_TILE
    n_q = S // T
    n_heads = LANES // head_dim
    lane_q = lax.broadcasted_iota(jnp.int32, (T, LANES), 1)
    trow = lax.broadcasted_iota(jnp.int32, (T, T), 0)
    scol = lax.broadcasted_iota(jnp.int32, (T, T), 1)
    strictly_before = scol < trow
    suffix = jnp.where(trow >= scol, 1.0, 0.0).astype(BF16)
    in_head = [(lane_q >= hd * head_dim) & (lane_q < (hd + 1) * head_dim) for hd in range(n_heads)]

    ksq = jnp.square(kt_ref[0].astype(F32))
    k_max = [jnp.sqrt(jnp.max(jnp.sum(ksq[hd * head_dim:(hd + 1) * head_dim], axis=0, keepdims=True),
                              axis=1, keepdims=True)) for hd in range(n_heads)]

    def softplus2(z):
        return jnp.maximum(z, jnp.log2(1.0 + jnp.exp2(jnp.minimum(z, ATT_EXP2_CAP))))

    def tile(q_h, k0, acc, run, mask):
        z = _dot(q_h, kt_ref[0, :, pl.ds(k0, T)])
        sp = softplus2(z)
        if mask is not None:
            sp = jnp.where(mask, sp, 0.0)
        within = _dot(sp.astype(BF16), suffix)
        a = jnp.exp2(z - within - run)
        if mask is not None:
            a = jnp.where(mask, a, 0.0)
        acc = acc + _dot(a.astype(BF16), v_ref[pl.ds(k0, T), :])
        return acc, run + within[:, 0:1]

    def all_zero_from_here(runs, z_bound):
        slack = runs[0] - z_bound[0]
        for r, zb in zip(runs[1:], z_bound[1:]):
            slack = jnp.minimum(slack, r - zb)
        return (jnp.min(slack) > ATT_ZERO_MARGIN).astype(jnp.int32)

    def block_tail(qi, with_left, q0, q_hs, z_bound, accs, runs, done):
        def cond(c):
            return (c[0] >= 0) & (c[1] == 0)

        def body(c):
            j, _, accs, runs = c
            k0 = pl.multiple_of(j * T, T)
            st = [tile(q_h, k0, a, r, None) for q_h, a, r in zip(q_hs, accs, runs)]
            accs = [a for a, _ in st]
            runs = [r for _, r in st]
            return j - 1, all_zero_from_here(runs, z_bound), accs, runs

        if with_left:
            _, _, accs, _ = lax.while_loop(cond, body, (qi - 2, done, accs, runs))
        result = accs[0]
        for m, a in zip(in_head[1:], accs[1:]):
            result = jnp.where(m, a, result)
        o_ref[pl.ds(q0, T), :] = result.astype(o_ref.dtype)

    def pair_head(qa, a_has_left):
        n_h = len(in_head)
        U = n_h * T
        static = isinstance(qa, int)
        q0 = [qa * T, qa * T + T] if static else [pl.multiple_of(qa * T, T), pl.multiple_of(qa * T + T, T)]
        q_hs, z_bound = [], []
        for blk in range(2):
            q_pair = q_ref[pl.ds(q0[blk], T), :]
            qsq = jnp.square(q_pair.astype(F32))
            q_hs.append([jnp.where(m, q_pair, jnp.zeros_like(q_pair)) for m in in_head])
            z_bound.append([jnp.sqrt(jnp.sum(jnp.where(m, qsq, 0.0), axis=-1, keepdims=True)) * km
                            for m, km in zip(in_head, k_max)])
        diag_mask = jnp.concatenate([strictly_before] * n_h, axis=0)

        def scores(lhs, k0):
            return _dot(jnp.concatenate(lhs, axis=0), kt_ref[0, :, pl.ds(k0, T)])

        z_mid = scores(q_hs[0] + q_hs[1], q0[0])
        z_bd = scores(q_hs[1], q0[1])
        sp_mid = softplus2(z_mid)
        sp_mid = jnp.concatenate([jnp.where(diag_mask, sp_mid[:U], 0.0), sp_mid[U:]], axis=0)
        sp_bd = jnp.where(diag_mask, softplus2(z_bd), 0.0)
        sps = [sp_mid, sp_bd]
        if a_has_left:
            k_al = pl.multiple_of(qa * T - T, T)
            z_al = scores(q_hs[0], k_al)
            sps.append(softplus2(z_al))
        within = _dot(jnp.concatenate(sps, axis=0).astype(BF16), suffix)
        w_ad, w_bl, w_bd = within[:U], within[U:2 * U], within[2 * U:3 * U]
        run_a = w_ad[:, 0:1]
        run_b = w_bd[:, 0:1]
        a_ad = jnp.where(diag_mask, jnp.exp2(z_mid[:U] - w_ad), 0.0)
        a_bd = jnp.where(diag_mask, jnp.exp2(z_bd - w_bd), 0.0)
        a_bl = jnp.exp2(z_mid[U:] - w_bl - run_b)
        o_mid = _dot(jnp.concatenate([a_ad, a_bl], axis=0).astype(BF16), v_ref[pl.ds(q0[0], T), :])
        acc_a = o_mid[:U]
        acc_b = o_mid[U:] + _dot(a_bd.astype(BF16), v_ref[pl.ds(q0[1], T), :])
        run_b = run_b + w_bl[:, 0:1]
        if a_has_left:
            w_al = within[3 * U:]
            a_al = jnp.exp2(z_al - w_al - run_a)
            acc_a = acc_a + _dot(a_al.astype(BF16), v_ref[pl.ds(k_al, T), :])
            run_a = run_a + w_al[:, 0:1]
        out = []
        for blk, (acc, run) in enumerate(((acc_a, run_a), (acc_b, run_b))):
            accs = [acc[hd * T:(hd + 1) * T] for hd in range(n_h)]
            runs = [run[hd * T:(hd + 1) * T] for hd in range(n_h)]
            out.append((q0[blk], q_hs[blk], z_bound[blk], accs, runs, all_zero_from_here(runs, z_bound[blk])))
        return out

    def block_pair(p, a_has_left):
        qa = 2 * p
        heads = pair_head(qa, a_has_left)
        block_tail(qa, a_has_left, *heads[0])
        block_tail(qa + 1, True, *heads[1])

    block_pair(0, False)

    def later_pair(p, carry):
        block_pair(p, True)
        return carry

    lax.fori_loop(1, n_q // 2, later_pair, 0)


def _sb_attention(q, kt, v, batch, seq, head_dim):
    T, sb_dim = q.shape
    assert LANES % head_dim == 0 and seq % (2 * ATT_TILE) == 0
    n_p = sb_dim // LANES
    return pl.pallas_call(
        functools.partial(_sb_attention_kernel, head_dim=head_dim),
        grid=(batch, n_p),
        in_specs=[
            pl.BlockSpec((seq, LANES), lambda b, p: (b, p)),
            pl.BlockSpec((1, LANES, seq), lambda b, p: (b, p, 0)),
            pl.BlockSpec((seq, LANES), lambda b, p: (b, p)),
        ],
        out_specs=pl.BlockSpec((seq, LANES), lambda b, p: (b, p)),
        out_shape=jax.ShapeDtypeStruct((T, sb_dim), BF16),
        compiler_params=pltpu.CompilerParams(
            dimension_semantics=("arbitrary", "arbitrary"), vmem_limit_bytes=VMEM_LIMIT),
        name="sb_attention",
    )(q, kt, v)


def _out_proj_kernel(o_ref, w_ref, h_ref, out_ref):
    out_ref[...] = h_ref[...] + _dot(o_ref[...], w_ref[...])


def _out_proj(o, w_out, h):
    T, D = h.shape
    R = PROJ_ROWS
    return pl.pallas_call(
        _out_proj_kernel,
        grid=(T // R,),
        in_specs=[
            pl.BlockSpec((R, o.shape[1]), lambda i: (i, 0)),
            pl.BlockSpec(w_out.shape, lambda i: (0, 0)),
            pl.BlockSpec((R, D), lambda i: (i, 0)),
        ],
        out_specs=pl.BlockSpec((R, D), lambda i: (i, 0)),
        out_shape=jax.ShapeDtypeStruct((T, D), F32),
        compiler_params=pltpu.CompilerParams(dimension_semantics=("arbitrary",), vmem_limit_bytes=VMEM_LIMIT),
        name="out_proj",
    )(o, w_out.astype(BF16), h)


def kernel(x, a_norm, a_w_in, a_w_gk2, a_b_gk2, a_o_norm, a_w_out, kv_norm, w_kv, b_norm, b_w_q, b_w_out,
           m_norm, m_w_group, m_b_group, m_w_expert, m_b_expert, m_w_gate, m_w_up, m_w_down, final_norm):
    B, S, D = x.shape
    assert a_norm.shape[0] == 1 and b_norm.shape[0] == 1 and m_norm.shape[0] == 2
    head_dim = b_w_q.shape[2] // SB_HEADS
    h = x.reshape(B * S, D)

    def moe_sorted(h, layer):
        return _moe_sorted(h, layer, m_norm[layer], m_w_group[layer], m_b_group[layer], m_w_expert[layer],
                           m_b_expert[layer], m_w_gate, m_w_up, m_w_down)

    h = _gla_layer(h, B, S, a_norm[0], a_w_in[0], a_w_gk2[0], a_b_gk2[0], a_o_norm[0], a_w_out[0])
    pos, ys = moe_sorted(h, 0)
    h, q, kt, v = _combine_qkv(h, pos, ys, B, S, b_norm[0], kv_norm, b_w_q[0], w_kv,
                               math.log2(math.e) / math.sqrt(head_dim))
    o = _sb_attention(q, kt, v, B, S, head_dim)
    h = _out_proj(o, b_w_out[0], h)
    pos, ys = moe_sorted(h, 1)
    h = _combine_norm(h, pos, ys, final_norm)
    return h.reshape(B, S, D)
```

```python
import functools
import math

import jax
import jax.numpy as jnp
from jax import lax
from jax.experimental import pallas as pl
from jax.experimental.pallas import tpu as pltpu

RMS_EPS = 1e-6

GLA_HEADS = 4
GLA_CHUNK = 64
CHUNK_SHIFT = GLA_CHUNK.bit_length() - 1
GATE_NORMALIZER = 16.0
SB_HEADS = 16
N_GROUPS = 4
EXPERTS_PER_GROUP = 4
N_EXPERTS = N_GROUPS * EXPERTS_PER_GROUP
PAIRS_PER_GROUP = EXPERTS_PER_GROUP * (EXPERTS_PER_GROUP - 1) // 2
N_CLASSES = N_GROUPS * PAIRS_PER_GROUP

LANES = 128
SUBLANES = 8
ROUTE_ROWS = 32
EXPERT_ROW0 = 8
VMEM_LIMIT = 56 * 1024 * 1024

GLA_ROWS = 512
GLA_BLOCK = 256
PROJ_ROWS = 512
ATT_TILE = 256
ATT_GROUP = 8
ATT_ZERO_MARGIN = 160.0
ATT_EXP2_CAP = 100.0
MOE_ROWS = 512
MOE_TM = 512

BF16 = jnp.bfloat16
F32 = jnp.float32


def _dot(a, b):
    return jnp.dot(a, b, preferred_element_type=F32)


def _dot_nt(a, b):
    return lax.dot_general(a, b, (((1,), (1,)), ((), ())), preferred_element_type=F32)


def _split_dot(x, m01):
    hi = x.astype(BF16)
    lo = (x - hi.astype(F32)).astype(BF16)
    return _dot(hi, m01) + _dot(lo, m01)


def _rms_scale(x):
    return lax.rsqrt(jnp.mean(x * x, axis=-1, keepdims=True) + RMS_EPS)


def _log_sigmoid(x):
    return jnp.minimum(x, 0.0) - jnp.log(1.0 + jnp.exp(-jnp.abs(x)))


def _sigmoid(x):
    return 1.0 / (1.0 + jnp.exp(-x))


def _gla_block(h, states, nrm_ref, wrow_ref, wkt_ref, wgk2_ref, bgk2_ref, onorm_ref, wout_ref,
               *, dk, dv, qk_dim, v_dim, log_scale):
    R = h.shape[0]
    n_chunks = R // GLA_CHUNK
    u = (h * _rms_scale(h) * nrm_ref[...]).astype(BF16)

    proj = _dot(u, wrow_ref[...])
    q = proj[:, :qk_dim]
    v = proj[:, qk_dim:qk_dim + v_dim].astype(BF16)
    g = proj[:, qk_dim + v_dim:qk_dim + 2 * v_dim]
    lr = proj[:, qk_dim + 2 * v_dim:].astype(BF16)
    kt = _dot_nt(wkt_ref[...], u)

    gk = _log_sigmoid(_dot(lr, wgk2_ref[...]) + bgk2_ref[...]) * (1.0 / GATE_NORMALIZER)
    gkt = gk.T

    row = lax.broadcasted_iota(jnp.int32, (R, R), 0)
    col = lax.broadcasted_iota(jnp.int32, (R, R), 1)
    same_chunk = (row >> CHUNK_SHIFT) == (col >> CHUNK_SHIFT)
    causal = same_chunk & (col <= row)
    upto = jnp.where(same_chunk & (row <= col), 1.0, 0.0).astype(BF16)
    after = jnp.where(same_chunk & (row > col), 1.0, 0.0).astype(BF16)

    bt = _split_dot(gkt, upto)
    tail_t = _split_dot(gkt, after)
    b = bt.T

    q_dec = (q * jnp.exp(b + log_scale)).astype(BF16)
    k_inv_t = (kt * jnp.exp(-bt)).astype(BF16)
    k_end_t = (kt * jnp.exp(tail_t)).astype(BF16)
    chunk_decay_t = jnp.exp(bt + tail_t)

    lane_chunk = lax.broadcasted_iota(jnp.int32, (dk, R), 1) >> CHUNK_SHIFT

    acc = h
    new_states = []
    for hd in range(GLA_HEADS):
        ks = slice(hd * dk, (hd + 1) * dk)
        vs = slice(hd * dv, (hd + 1) * dv)
        qd_h = q_dec[:, ks]
        v_h = v[:, vs]
        att = _dot(qd_h, k_inv_t[ks, :])
        att = jnp.where(causal, att, 0.0).astype(BF16)
        kend_h = k_end_t[ks, :]
        kend_c = [jnp.where(lane_chunk == c, kend_h, jnp.zeros_like(kend_h)) for c in range(n_chunks)]
        ov = _dot(jnp.concatenate([att] + kend_c, axis=0), v_h)
        o_h = ov[:R]
        state = states[hd]
        inter = []
        for c in range(n_chunks):
            rows = slice(c * GLA_CHUNK, (c + 1) * GLA_CHUNK)
            inter.append(_dot(qd_h[rows], state.astype(BF16)))
            decay = chunk_decay_t[ks, c * GLA_CHUNK:c * GLA_CHUNK + 1]
            state = decay * state + ov[R + c * dk:R + (c + 1) * dk]
        new_states.append(state)
        o_h = o_h + jnp.concatenate(inter, axis=0)
        o_h = o_h * _rms_scale(o_h) * onorm_ref[...]
        g_h = g[:, vs]
        o_h = o_h * (g_h * _sigmoid(g_h))
        acc = acc + _dot(o_h.astype(BF16), wout_ref[vs, :])
    return acc, new_states


def _gla_kernel(h_ref, *refs, **dims):
    *w_refs, out_ref, state_ref = refs

    @pl.when(pl.program_id(1) == 0)
    def _():
        state_ref[...] = jnp.zeros_like(state_ref)

    states = [state_ref[hd] for hd in range(GLA_HEADS)]
    for blk in range(h_ref.shape[0] // GLA_BLOCK):
        rows = slice(blk * GLA_BLOCK, (blk + 1) * GLA_BLOCK)
        out, states = _gla_block(h_ref[rows, :], states, *w_refs, **dims)
        out_ref[rows, :] = out
    for hd in range(GLA_HEADS):
        state_ref[hd] = states[hd]


def _gla_layer(h, batch, seq, a_norm, w_in, w_gk2, b_gk2, o_norm, w_out):
    T, D = h.shape
    rank, qk_dim = w_gk2.shape
    v_dim = w_out.shape[0]
    dk = qk_dim // GLA_HEADS
    dv = v_dim // GLA_HEADS
    R = GLA_ROWS
    assert seq % R == 0 and R % GLA_CHUNK == 0 and rank <= LANES
    assert w_in.shape[1] == 2 * qk_dim + 2 * v_dim + rank
    n_s = seq // R

    o_q, o_k, o_v, o_lr, o_g = 0, qk_dim, 2 * qk_dim, 2 * qk_dim + v_dim, 2 * qk_dim + v_dim + rank
    w_lr = jnp.pad(w_in[:, o_lr:o_lr + rank], ((0, 0), (0, LANES - rank)))
    w_row = jnp.concatenate([w_in[:, o_q:o_k], w_in[:, o_v:o_lr], w_in[:, o_g:], w_lr], axis=1).astype(BF16)
    w_kt = w_in[:, o_k:o_v].T.astype(BF16)
    w_gk2p = jnp.pad(w_gk2, ((0, LANES - rank), (0, 0))).astype(BF16)
    const = lambda b, s: (0, 0)
    kern = functools.partial(_gla_kernel, dk=dk, dv=dv, qk_dim=qk_dim, v_dim=v_dim,
                             log_scale=math.log(dk ** -0.5))
    return pl.pallas_call(
        kern,
        grid=(batch, n_s),
        in_specs=[
            pl.BlockSpec((R, D), lambda b, s: (b * n_s + s, 0)),
            pl.BlockSpec((1, D), const),
            pl.BlockSpec(w_row.shape, const),
            pl.BlockSpec(w_kt.shape, const),
            pl.BlockSpec(w_gk2p.shape, const),
            pl.BlockSpec((1, qk_dim), const),
            pl.BlockSpec((1, dv), const),
            pl.BlockSpec((v_dim, D), const),
        ],
        out_specs=pl.BlockSpec((R, D), lambda b, s: (b * n_s + s, 0)),
        out_shape=jax.ShapeDtypeStruct((T, D), F32),
        scratch_shapes=[pltpu.VMEM((GLA_HEADS, dk, dv), F32)],
        compiler_params=pltpu.CompilerParams(
            dimension_semantics=("arbitrary", "arbitrary"), vmem_limit_bytes=VMEM_LIMIT),
        name="gla_layer",
    )(h, a_norm.reshape(1, D), w_row, w_kt, w_gk2p, b_gk2.reshape(1, qk_dim), o_norm.reshape(1, dv),
      w_out.astype(BF16))


def _router_kernel(h_ref, nrm_ref, wt_ref, bt_ref, upper_ref, idx_ref, wgt_ref, cnt_ref, carry_ref):
    R = h_ref.shape[0]

    @pl.when(pl.program_id(0) == 0)
    def _():
        carry_ref[...] = jnp.zeros_like(carry_ref)

    h = h_ref[...]
    t = h * _rms_scale(h) * nrm_ref[...]
    t_hi = t.astype(BF16)
    t_lo = (t - t_hi.astype(F32)).astype(BF16)
    p = _dot_nt(wt_ref[...], t_hi)
    logits = p[:ROUTE_ROWS] + p[ROUTE_ROWS:] + _dot_nt(wt_ref[:ROUTE_ROWS], t_lo) + bt_ref[:, 0:1]
    row = lax.broadcasted_iota(jnp.int32, (ROUTE_ROWS, R), 0)
    neg = jnp.float32(-jnp.inf)

    def first_argmax(vals):
        m = jnp.max(vals, axis=0, keepdims=True)
        i = jnp.min(jnp.where(vals == m, row, ROUTE_ROWS), axis=0, keepdims=True)
        return m, i

    gl = jnp.where(row < N_GROUPS, logits, neg)
    g_max, g_idx = first_argmax(gl)
    g_w = 1.0 / jnp.sum(jnp.exp(gl - g_max), axis=0, keepdims=True)

    base = EXPERT_ROW0 + g_idx * EXPERTS_PER_GROUP
    el = jnp.where((row >= base) & (row < base + EXPERTS_PER_GROUP), logits, neg)
    m1, i1 = first_argmax(el)
    m2, i2 = first_argmax(jnp.where(row == i1, neg, el))
    r = jnp.exp(m2 - m1)
    w1 = g_w / (1.0 + r)
    w2 = g_w * r / (1.0 + r)
    l1 = i1 - base
    l2 = i2 - base
    first_is_lo = l1 < l2
    lo = jnp.minimum(l1, l2)
    hi = jnp.maximum(l1, l2)
    cls = g_idx * PAIRS_PER_GROUP + ((lo * (2 * EXPERTS_PER_GROUP - 1 - lo)) >> 1) + (hi - lo - 1)
    w_lo = jnp.where(first_is_lo, w1, w2)
    w_hi = jnp.where(first_is_lo, w2, w1)

    onehot = jnp.where(row == cls, 1.0, 0.0)
    before = _dot(onehot.astype(BF16), upper_ref[...]) + carry_ref[:, 0:1]
    rank = jnp.sum(jnp.where(row == cls, before, 0.0), axis=0, keepdims=True)
    total = carry_ref[:, 0:1] + jnp.sum(onehot, axis=1, keepdims=True)
    carry_ref[...] = jnp.broadcast_to(total, carry_ref.shape)
    cnt_ref[...] = jnp.broadcast_to(total, cnt_ref.shape).astype(jnp.int32)

    row8 = lax.broadcasted_iota(jnp.int32, (SUBLANES, R), 0)
    idx_ref[0] = jnp.where(row8 == 0, cls, jnp.where(row8 == 1, rank.astype(jnp.int32), 0))
    w_rows = jnp.where(row == 0, w_lo, jnp.where(row == 1, w_hi, 0.0))
    w_rows = jnp.concatenate([w_rows, jnp.zeros((LANES - ROUTE_ROWS, R), F32)], axis=0)
    wgt_ref[...] = w_rows.T


def _router(h, m_norm, w_group, b_group, w_expert, b_expert):
    T, D = h.shape
    R = MOE_ROWS
    assert T % R == 0 and N_CLASSES <= ROUTE_ROWS
    pad_g = EXPERT_ROW0 - N_GROUPS
    pad_e = ROUTE_ROWS - EXPERT_ROW0 - N_EXPERTS
    w_t = jnp.pad(jnp.concatenate([w_group.T, jnp.zeros((pad_g, D), F32), w_expert.T], axis=0), ((0, pad_e), (0, 0)))
    w_t_hi = w_t.astype(BF16)
    w_t = jnp.concatenate([w_t_hi, (w_t - w_t_hi.astype(F32)).astype(BF16)], axis=0)
    b_t = jnp.pad(jnp.concatenate([b_group, jnp.zeros((pad_g,), F32), b_expert]), (0, pad_e))
    b_t = jnp.broadcast_to(b_t[:, None], (ROUTE_ROWS, LANES))
    idx = jnp.arange(R)
    upper = (idx[:, None] < idx[None, :]).astype(BF16)
    const = lambda i: (0, 0)
    return pl.pallas_call(
        _router_kernel,
        grid=(T // R,),
        in_specs=[
            pl.BlockSpec((R, D), lambda i: (i, 0)),
            pl.BlockSpec((1, D), const),
            pl.BlockSpec((2 * ROUTE_ROWS, D), const),
            pl.BlockSpec((ROUTE_ROWS, LANES), const),
            pl.BlockSpec((R, R), const),
        ],
        out_specs=[
            pl.BlockSpec((1, SUBLANES, R), lambda i: (i, 0, 0)),
            pl.BlockSpec((R, LANES), lambda i: (i, 0)),
            pl.BlockSpec((ROUTE_ROWS, LANES), const),
        ],
        out_shape=[
            jax.ShapeDtypeStruct((T // R, SUBLANES, R), jnp.int32),
            jax.ShapeDtypeStruct((T, LANES), F32),
            jax.ShapeDtypeStruct((ROUTE_ROWS, LANES), jnp.int32),
        ],
        scratch_shapes=[pltpu.VMEM((ROUTE_ROWS, LANES), F32)],
        compiler_params=pltpu.CompilerParams(dimension_semantics=("arbitrary",), vmem_limit_bytes=VMEM_LIMIT),
        name="moe_router",
    )(h, m_norm.reshape(1, D), w_t, b_t, upper)


def _dispatch_kernel(pos_ref, pos_prev_ref, fill_ref, h_ref, wgt_ref, xs_ref, row_ref, zero_ref, sem, fill_sem):
    i = pl.program_id(0)
    n = pl.num_programs(0)
    R, D = h_ref.shape
    G = R // SUBLANES
    slot = i % 2

    @pl.when(i == 0)
    def _():
        zero_ref[...] = jnp.zeros_like(zero_ref)
        tile_rows = zero_ref.shape[0]

        def fill_copy(k):
            first = pl.multiple_of(jnp.maximum(fill_ref[0, 0, k], 0), tile_rows)
            return pltpu.make_async_copy(zero_ref, xs_ref.at[pl.ds(first, tile_rows)], fill_sem)

        def start(k, c):
            @pl.when(fill_ref[0, 0, k] >= 0)
            def _():
                fill_copy(k).start()
            return c

        def wait(k, c):
            @pl.when(fill_ref[0, 0, k] >= 0)
            def _():
                fill_copy(k).wait()
            return c

        lax.fori_loop(0, fill_ref.shape[2], start, 0)
        lax.fori_loop(0, fill_ref.shape[2], wait, 0)

    row_ref[slot, :, :, :D] = h_ref[...].reshape(G, SUBLANES, D)
    row_ref[slot, :, :, D:] = wgt_ref[...].reshape(G, SUBLANES, LANES)

    def row_copy(p_ref, s, g, j):
        dst = p_ref[0, 0, g * SUBLANES + j]
        return pltpu.make_async_copy(row_ref.at[s, g, pl.ds(j, 1)], xs_ref.at[pl.ds(dst, 1)], sem.at[s])

    for g in range(G):
        for j in range(SUBLANES):
            row_copy(pos_ref, slot, g, j).start()

    def drain(p_ref, s):
        def body(g, c):
            for j in range(SUBLANES):
                row_copy(p_ref, s, g, j).wait()
            return c
        lax.fori_loop(0, G, body, 0)

    @pl.when(i > 0)
    def _():
        drain(pos_prev_ref, 1 - slot)

    @pl.when(i == n - 1)
    def _():
        drain(pos_ref, slot)


def _dispatch(h, wgt, pos, fill, n_rows):
    T, D = h.shape
    R = MOE_ROWS
    n_t = T // R
    pos3 = pos.reshape(n_t, 1, R)
    fill3 = fill.reshape(1, 1, -1)
    return pl.pallas_call(
        _dispatch_kernel,
        grid=(n_t,),
        in_specs=[
            pl.BlockSpec((1, 1, R), lambda i: (i, 0, 0), memory_space=pltpu.SMEM),
            pl.BlockSpec((1, 1, R), lambda i: (jnp.maximum(i - 1, 0), 0, 0), memory_space=pltpu.SMEM),
            pl.BlockSpec(fill3.shape, lambda i: (0, 0, 0), memory_space=pltpu.SMEM),
            pl.BlockSpec((R, D), lambda i: (i, 0)),
            pl.BlockSpec((R, LANES), lambda i: (i, 0)),
        ],
        out_specs=pl.BlockSpec(memory_space=pl.ANY),
        out_shape=jax.ShapeDtypeStruct((n_rows, D + LANES), F32),
        scratch_shapes=[pltpu.VMEM((2, R // SUBLANES, SUBLANES, D + LANES), F32),
                        pltpu.VMEM((MOE_TM, D + LANES), F32),
                        pltpu.SemaphoreType.DMA((2,)), pltpu.SemaphoreType.DMA],
        compiler_params=pltpu.CompilerParams(dimension_semantics=("arbitrary",), vmem_limit_bytes=VMEM_LIMIT),
        name="moe_dispatch",
    )(pos3, pos3, fill3, h, wgt)


def _experts_kernel(used_ref, lo_ref, hi_ref, xs_ref, nrm_ref, g_lo_ref, u_lo_ref, d_lo_ref, g_hi_ref, u_hi_ref,
                    d_hi_ref, ys_ref, wgu_ref, wd_ref, *, d_model, d_expert):
    i = pl.program_id(0)
    active = i < used_ref[0]
    prev = jnp.maximum(i - 1, 0)

    def refresh(k, e_ref, g_ref, u_ref, d_ref):
        @pl.when(active & ((i == 0) | (e_ref[i] != e_ref[prev])))
        def _():
            wgu_ref[k, :, :d_expert] = g_ref[0, 0].astype(BF16)
            wgu_ref[k, :, d_expert:] = u_ref[0, 0].astype(BF16)
            wd_ref[k] = d_ref[0, 0].astype(BF16)

    refresh(0, lo_ref, g_lo_ref, u_lo_ref, d_lo_ref)
    refresh(1, hi_ref, g_hi_ref, u_hi_ref, d_hi_ref)

    @pl.when(active)
    def _():
        x = xs_ref[:, :d_model]
        t = (x * _rms_scale(x) * nrm_ref[...]).astype(BF16)
        y = None
        for k in range(2):
            gu = _dot(t, wgu_ref[k])
            gate = gu[:, :d_expert]
            hdn = (gate * _sigmoid(gate) * gu[:, d_expert:]).astype(BF16)
            y_k = xs_ref[:, d_model + k:d_model + k + 1] * _dot(hdn, wd_ref[k])
            y = y_k if y is None else y + y_k
        ys_ref[...] = y

    @pl.when(jnp.logical_not(active))
    def _():
        ys_ref[...] = jnp.zeros_like(ys_ref)


def _experts(xs, n_used, tile_lo, tile_hi, m_norm, layer, w_gate, w_up, w_down):
    n_rows = xs.shape[0]
    D = m_norm.shape[0]
    d_expert = w_gate.shape[-1]
    TM = MOE_TM
    n_tiles = n_rows // TM
    in_w = lambda which: (lambda i, nu, lo, hi: (layer, which(lo, hi)[i], 0, 0))
    pick_lo = lambda lo, hi: lo
    pick_hi = lambda lo, hi: hi
    grid_spec = pltpu.PrefetchScalarGridSpec(
        num_scalar_prefetch=3,
        grid=(n_tiles,),
        in_specs=[
            pl.BlockSpec((TM, D + LANES), lambda i, nu, lo, hi: (i, 0)),
            pl.BlockSpec((1, D), lambda i, nu, lo, hi: (0, 0)),
            pl.BlockSpec((1, 1, D, d_expert), in_w(pick_lo)),
            pl.BlockSpec((1, 1, D, d_expert), in_w(pick_lo)),
            pl.BlockSpec((1, 1, d_expert, D), in_w(pick_lo)),
            pl.BlockSpec((1, 1, D, d_expert), in_w(pick_hi)),
            pl.BlockSpec((1, 1, D, d_expert), in_w(pick_hi)),
            pl.BlockSpec((1, 1, d_expert, D), in_w(pick_hi)),
        ],
        out_specs=pl.BlockSpec((TM, D), lambda i, nu, lo, hi: (i, 0)),
        scratch_shapes=[pltpu.VMEM((2, D, 2 * d_expert), BF16), pltpu.VMEM((2, d_expert, D), BF16)],
    )
    return pl.pallas_call(
        functools.partial(_experts_kernel, d_model=D, d_expert=d_expert),
        grid_spec=grid_spec,
        out_shape=jax.ShapeDtypeStruct((n_rows, D), F32),
        compiler_params=pltpu.CompilerParams(dimension_semantics=("arbitrary",), vmem_limit_bytes=VMEM_LIMIT),
        name="moe_experts",
    )(n_used, tile_lo, tile_hi, xs, m_norm.reshape(1, D), w_gate, w_up, w_down, w_gate, w_up, w_down)


def _combine_norm_kernel(pos_ref, pos_next_ref, h_ref, nrm_ref, ys_ref, out_ref, buf_ref, sem):
    i = pl.program_id(0)
    n = pl.num_programs(0)
    R, D = h_ref.shape
    G = R // SUBLANES
    slot = i % 2

    def row_copy(p_ref, s, g, j):
        src = p_ref[0, 0, g * SUBLANES + j]
        return pltpu.make_async_copy(ys_ref.at[pl.ds(src, 1)], buf_ref.at[s, g, pl.ds(j, 1)], sem.at[s])

    def issue(p_ref, s):
        def body(g, c):
            for j in range(SUBLANES):
                row_copy(p_ref, s, g, j).start()
            return c
        lax.fori_loop(0, G, body, 0)

    @pl.when(i == 0)
    def _():
        issue(pos_ref, slot)

    @pl.when(i < n - 1)
    def _():
        for g in range(G):
            for j in range(SUBLANES):
                row_copy(pos_next_ref, 1 - slot, g, j).start()

    def drain(g, c):
        for j in range(SUBLANES):
            row_copy(pos_ref, slot, g, j).wait()
        return c

    lax.fori_loop(0, G, drain, 0)

    out = h_ref[...] + buf_ref[slot].reshape(R, D)
    out_ref[...] = out * _rms_scale(out) * nrm_ref[...]


def _combine_norm(h, pos, ys, norm):
    T, D = h.shape
    R = MOE_ROWS
    n_t = T // R
    pos3 = pos.reshape(n_t, 1, R)
    return pl.pallas_call(
        _combine_norm_kernel,
        grid=(n_t,),
        in_specs=[
            pl.BlockSpec((1, 1, R), lambda i: (i, 0, 0), memory_space=pltpu.SMEM),
            pl.BlockSpec((1, 1, R), lambda i: (jnp.minimum(i + 1, n_t - 1), 0, 0), memory_space=pltpu.SMEM),
            pl.BlockSpec((R, D), lambda i: (i, 0)),
            pl.BlockSpec((1, D), lambda i: (0, 0)),
            pl.BlockSpec(memory_space=pl.ANY),
        ],
        out_specs=pl.BlockSpec((R, D), lambda i: (i, 0)),
        out_shape=jax.ShapeDtypeStruct((T, D), F32),
        scratch_shapes=[pltpu.VMEM((2, R // SUBLANES, SUBLANES, D), F32), pltpu.SemaphoreType.DMA((2,))],
        compiler_params=pltpu.CompilerParams(dimension_semantics=("arbitrary",), vmem_limit_bytes=VMEM_LIMIT),
        name="moe_combine",
    )(pos3, pos3, h, norm.reshape(1, D), ys)


def _class_experts():
    lo, hi = [], []
    for g in range(N_GROUPS):
        for a in range(EXPERTS_PER_GROUP):
            for b in range(a + 1, EXPERTS_PER_GROUP):
                lo.append(g * EXPERTS_PER_GROUP + a)
                hi.append(g * EXPERTS_PER_GROUP + b)
    return jnp.asarray(lo, jnp.int32), jnp.asarray(hi, jnp.int32)


def _moe_sorted(h, layer, m_norm, w_group, b_group, w_expert, b_expert, w_gate, w_up, w_down):
    T, D = h.shape
    TM = MOE_TM
    idx, wgt, cnt = _router(h, m_norm, w_group, b_group, w_expert, b_expert)

    counts = cnt[:N_CLASSES, 0]
    padded = ((counts + TM - 1) // TM) * TM
    ends = jnp.cumsum(padded)
    starts = ends - padded
    n_rows = T + N_CLASSES * TM
    n_tiles = n_rows // TM
    tile_start = jnp.arange(n_tiles, dtype=jnp.int32) * TM
    tile_class = jnp.minimum(
        jnp.sum((ends[None, :] <= tile_start[:, None]).astype(jnp.int32), axis=1), N_CLASSES - 1)
    class_lo, class_hi = _class_experts()
    n_used = (ends[-1] // TM).astype(jnp.int32).reshape(1)
    cls, pos = idx[:, 0, :], idx[:, 1, :]
    for c in range(N_CLASSES):
        pos = pos + jnp.where(cls == c, starts[c], 0)
    pos = pos.astype(jnp.int32)

    partial_tail = jnp.where(counts % TM != 0, ends - TM, -1)
    unused = n_used[0] + jnp.arange(N_CLASSES, dtype=jnp.int32)
    unused = jnp.where(unused < n_tiles, unused * TM, -1)
    fill = jnp.concatenate([partial_tail, unused]).astype(jnp.int32)

    xs = _dispatch(h, wgt, pos, fill, n_rows)
    ys = _experts(xs, n_used, class_lo[tile_class], class_hi[tile_class], m_norm, layer, w_gate, w_up, w_down)
    return pos, ys


def _combine_qkv_kernel(pos_ref, pos_next_ref, h_ref, ys_ref, qn_ref, kvn_ref, wq_ref, wkt_ref, wv_ref,
                        hout_ref, q_ref, kt_ref, v_ref, buf_a, buf_b, sem):
    i = pl.program_id(0)
    n = pl.num_programs(0)
    R, D = h_ref.shape
    G = R // SUBLANES

    def row_copy(p_ref, buf, s, g, j):
        src = p_ref[0, 0, g * SUBLANES + j]
        return pltpu.make_async_copy(ys_ref.at[pl.ds(src, 1)], buf.at[g, pl.ds(j, 1)], sem.at[s])

    @pl.when(i == 0)
    def _():
        def body(g, c):
            for j in range(SUBLANES):
                row_copy(pos_ref, buf_a, 0, g, j).start()
            return c
        lax.fori_loop(0, G, body, 0)

    def step(cur, cur_s, nxt, nxt_s):
        def drain(g, c):
            for j in range(SUBLANES):
                row_copy(pos_ref, cur, cur_s, g, j).wait()
            return c
        lax.fori_loop(0, G, drain, 0)

        for g in range(G):
            for j in range(SUBLANES):
                row_copy(pos_next_ref, nxt, nxt_s, g, j).start()
        h = h_ref[...] + cur[...].reshape(R, D)
        hout_ref[...] = h
        xhat = h * _rms_scale(h)
        uq = (xhat * qn_ref[...]).astype(BF16)
        ukv = (xhat * kvn_ref[...]).astype(BF16)
        q_ref[...] = _dot(uq, wq_ref[...]).astype(BF16)
        v_ref[...] = _dot(ukv, wv_ref[...]).astype(BF16)
        kt_ref[0] = _dot_nt(wkt_ref[...], ukv).astype(BF16)

        @pl.when(i == n - 1)
        def _():
            def body(g, c):
                for j in range(SUBLANES):
                    row_copy(pos_next_ref, nxt, nxt_s, g, j).wait()
                return c
            lax.fori_loop(0, G, body, 0)

    @pl.when(i % 2 == 0)
    def _():
        step(buf_a, 0, buf_b, 1)

    @pl.when(i % 2 == 1)
    def _():
        step(buf_b, 1, buf_a, 0)


def _combine_qkv(h, pos, ys, batch, seq, q_norm, kv_norm, w_q, w_kv, scale):
    T, D = h.shape
    sb_dim = w_q.shape[1]
    R = MOE_ROWS
    assert seq % R == 0
    n_t = T // R
    n_s = seq // R
    pos3 = pos.reshape(n_t, 1, R)
    w_qs = (w_q * scale).astype(BF16)
    w_kt = w_kv[:, :sb_dim].T.astype(BF16)
    w_v = w_kv[:, sb_dim:].astype(BF16)
    const = lambda i: (0, 0)
    return pl.pallas_call(
        _combine_qkv_kernel,
        grid=(n_t,),
        in_specs=[
            pl.BlockSpec((1, 1, R), lambda i: (i, 0, 0), memory_space=pltpu.SMEM),
            pl.BlockSpec((1, 1, R), lambda i: (jnp.minimum(i + 1, n_t - 1), 0, 0), memory_space=pltpu.SMEM),
            pl.BlockSpec((R, D), lambda i: (i, 0)),
            pl.BlockSpec(memory_space=pl.ANY),
            pl.BlockSpec((1, D), const),
            pl.BlockSpec((1, D), const),
            pl.BlockSpec((D, sb_dim), const),
            pl.BlockSpec((sb_dim, D), const),
            pl.BlockSpec((D, sb_dim), const),
        ],
        out_specs=[
            pl.BlockSpec((R, D), lambda i: (i, 0)),
            pl.BlockSpec((R, sb_dim), lambda i: (i, 0)),
            pl.BlockSpec((1, sb_dim, R), lambda i: (i // n_s, 0, i % n_s)),
            pl.BlockSpec((R, sb_dim), lambda i: (i, 0)),
        ],
        out_shape=[
            jax.ShapeDtypeStruct((T, D), F32),
            jax.ShapeDtypeStruct((T, sb_dim), BF16),
            jax.ShapeDtypeStruct((batch, sb_dim, seq), BF16),
            jax.ShapeDtypeStruct((T, sb_dim), BF16),
        ],
        scratch_shapes=[pltpu.VMEM((R // SUBLANES, SUBLANES, D), F32), pltpu.VMEM((R // SUBLANES, SUBLANES, D), F32),
                        pltpu.SemaphoreType.DMA((2,))],
        compiler_params=pltpu.CompilerParams(dimension_semantics=("arbitrary",), vmem_limit_bytes=VMEM_LIMIT),
        name="combine_qkv",
    )(pos3, pos3, h, ys, q_norm.reshape(1, D), kv_norm.reshape(1, D), w_qs, w_kt, w_v)


def _sb_attention_kernel(q_ref, kt_ref, v_ref, o_ref, *, head_dim):
    S = q_ref.shape[0]
    T = ATT_TILE
    n_q = S // T
    n_heads = LANES // head_dim
    lane_q = lax.broadcasted_iota(jnp.int32, (T, LANES), 1)
    trow = lax.broadcasted_iota(jnp.int32, (T, T), 0)
    scol = lax.broadcasted_iota(jnp.int32, (T, T), 1)
    strictly_before = scol < trow
    suffix = jnp.where(trow >= scol, 1.0, 0.0).astype(BF16)
    in_head = [(lane_q >= hd * head_dim) & (lane_q < (hd + 1) * head_dim) for hd in range(n_heads)]

    ksq = jnp.square(kt_ref[0].astype(F32))
    k_max = [jnp.sqrt(jnp.max(jnp.sum(ksq[hd * head_dim:(hd + 1) * head_dim], axis=0, keepdims=True),
                              axis=1, keepdims=True)) for hd in range(n_heads)]

    def softplus2(z):
        return jnp.maximum(z, jnp.log2(1.0 + jnp.exp2(jnp.minimum(z, ATT_EXP2_CAP))))

    def tile(q_h, k0, acc, run, mask):
        z = _dot(q_h, kt_ref[0, :, pl.ds(k0, T)])
        sp = softplus2(z)
        if mask is not None:
            sp = jnp.where(mask, sp, 0.0)
        within = _dot(sp.astype(BF16), suffix)
        a = jnp.exp2(z - within - run)
        if mask is not None:
            a = jnp.where(mask, a, 0.0)
        acc = acc + _dot(a.astype(BF16), v_ref[pl.ds(k0, T), :])
        return acc, run + within[:, 0:1]

    def all_zero_from_here(runs, z_bound):
        slack = runs[0] - z_bound[0]
        for r, zb in zip(runs[1:], z_bound[1:]):
            slack = jnp.minimum(slack, r - zb)
        return (jnp.min(slack) > ATT_ZERO_MARGIN).astype(jnp.int32)

    def block_tail(qi, with_left, q0, q_hs, z_bound, accs, runs, done):
        def cond(c):
            return (c[0] >= 0) & (c[1] == 0)

        def body(c):
            j, _, accs, runs = c
            k0 = pl.multiple_of(j * T, T)
            st = [tile(q_h, k0, a, r, None) for q_h, a, r in zip(q_hs, accs, runs)]
            accs = [a for a, _ in st]
            runs = [r for _, r in st]
            return j - 1, all_zero_from_here(runs, z_bound), accs, runs

        if with_left:
            _, _, accs, _ = lax.while_loop(cond, body, (qi - 2, done, accs, runs))
        result = accs[0]
        for m, a in zip(in_head[1:], accs[1:]):
            result = jnp.where(m, a, result)
        o_ref[pl.ds(q0, T), :] = result.astype(o_ref.dtype)

    def group_head(q_first, first_has_left):
        n_h = len(in_head)
        U = n_h * T
        G = ATT_GROUP
        static = isinstance(q_first, int)
        mult = (lambda x: x) if static else (lambda x: pl.multiple_of(x, T))
        q0 = [mult((q_first + g) * T) for g in range(G)]
        q_hs, z_bound = [], []
        for g in range(G):
            q_pair = q_ref[pl.ds(q0[g], T), :]
            qsq = jnp.square(q_pair.astype(F32))
            q_hs.append([jnp.where(m, q_pair, jnp.zeros_like(q_pair)) for m in in_head])
            z_bound.append([jnp.sqrt(jnp.sum(jnp.where(m, qsq, 0.0), axis=-1, keepdims=True)) * km
                            for m, km in zip(in_head, k_max)])
        diag_mask = jnp.concatenate([strictly_before] * n_h, axis=0)

        tiles = ([(mult(q_first * T - T), [("left", 0)])] if first_has_left else [])
        for j in range(G):
            tiles.append((q0[j], [("diag", j)] + ([("left", j + 1)] if j + 1 < G else [])))

        z, sp = {}, []
        order = []
        for k0, units in tiles:
            zz = _dot(jnp.concatenate([q_h for _, g in units for q_h in q_hs[g]], axis=0), kt_ref[0, :, pl.ds(k0, T)])
            for i, (kind, g) in enumerate(units):
                z_u = zz[i * U:(i + 1) * U]
                sp_u = softplus2(z_u)
                if kind == "diag":
                    sp_u = jnp.where(diag_mask, sp_u, 0.0)
                z[(kind, g)] = z_u
                sp.append(sp_u)
                order.append((kind, g))
        within = _dot(jnp.concatenate(sp, axis=0).astype(BF16), suffix)
        w = {key: within[i * U:(i + 1) * U] for i, key in enumerate(order)}

        a, run = {}, {}
        for g in range(G):
            run[g] = w[("diag", g)][:, 0:1]
            a[("diag", g)] = jnp.where(diag_mask, jnp.exp2(z[("diag", g)] - w[("diag", g)]), 0.0)
            if ("left", g) in w:
                a[("left", g)] = jnp.exp2(z[("left", g)] - w[("left", g)] - run[g])
                run[g] = run[g] + w[("left", g)][:, 0:1]
        acc = {}
        for k0, units in tiles:
            o = _dot(jnp.concatenate([a[key] for key in units], axis=0).astype(BF16), v_ref[pl.ds(k0, T), :])
            for i, (kind, g) in enumerate(units):
                o_u = o[i * U:(i + 1) * U]
                acc[g] = o_u if g not in acc else acc[g] + o_u
        split = lambda x: [x[hd * T:(hd + 1) * T] for hd in range(n_h)]
        out = []
        for g in range(G):
            runs = split(run[g])
            out.append((q0[g], q_hs[g], z_bound[g], split(acc[g]), runs, all_zero_from_here(runs, z_bound[g])))
        return out

    def block_group(p, first_has_left):
        q_first = ATT_GROUP * p
        heads = group_head(q_first, first_has_left)
        for g, hd in enumerate(heads):
            block_tail(q_first + g, first_has_left or g > 0, *hd)

    block_group(0, False)

    def later_group(p, carry):
        block_group(p, True)
        return carry

    lax.fori_loop(1, n_q // ATT_GROUP, later_group, 0)


def _sb_attention(q, kt, v, batch, seq, head_dim):
    T, sb_dim = q.shape
    assert LANES % head_dim == 0 and seq % (ATT_GROUP * ATT_TILE) == 0
    n_p = sb_dim // LANES
    return pl.pallas_call(
        functools.partial(_sb_attention_kernel, head_dim=head_dim),
        grid=(batch, n_p),
        in_specs=[
            pl.BlockSpec((seq, LANES), lambda b, p: (b, p)),
            pl.BlockSpec((1, LANES, seq), lambda b, p: (b, p, 0)),
            pl.BlockSpec((seq, LANES), lambda b, p: (b, p)),
        ],
        out_specs=pl.BlockSpec((seq, LANES), lambda b, p: (b, p)),
        out_shape=jax.ShapeDtypeStruct((T, sb_dim), BF16),
        compiler_params=pltpu.CompilerParams(
            dimension_semantics=("arbitrary", "arbitrary"), vmem_limit_bytes=VMEM_LIMIT),
        name="sb_attention",
    )(q, kt, v)


def _out_proj_kernel(o_ref, w_ref, h_ref, out_ref):
    out_ref[...] = h_ref[...] + _dot(o_ref[...], w_ref[...])


def _out_proj(o, w_out, h):
    T, D = h.shape
    R = PROJ_ROWS
    return pl.pallas_call(
        _out_proj_kernel,
        grid=(T // R,),
        in_specs=[
            pl.BlockSpec((R, o.shape[1]), lambda i: (i, 0)),
            pl.BlockSpec(w_out.shape, lambda i: (0, 0)),
            pl.BlockSpec((R, D), lambda i: (i, 0)),
        ],
        out_specs=pl.BlockSpec((R, D), lambda i: (i, 0)),
        out_shape=jax.ShapeDtypeStruct((T, D), F32),
        compiler_params=pltpu.CompilerParams(dimension_semantics=("arbitrary",), vmem_limit_bytes=VMEM_LIMIT),
        name="out_proj",
    )(o, w_out.astype(BF16), h)


def kernel(x, a_norm, a_w_in, a_w_gk2, a_b_gk2, a_o_norm, a_w_out, kv_norm, w_kv, b_norm, b_w_q, b_w_out,
           m_norm, m_w_group, m_b_group, m_w_expert, m_b_expert, m_w_gate, m_w_up, m_w_down, final_norm):
    B, S, D = x.shape
    assert a_norm.shape[0] == 1 and b_norm.shape[0] == 1 and m_norm.shape[0] == 2
    head_dim = b_w_q.shape[2] // SB_HEADS
    h = x.reshape(B * S, D)

    def moe_sorted(h, layer):
        return _moe_sorted(h, layer, m_norm[layer], m_w_group[layer], m_b_group[layer], m_w_expert[layer],
                           m_b_expert[layer], m_w_gate, m_w_up, m_w_down)

    h = _gla_layer(h, B, S, a_norm[0], a_w_in[0], a_w_gk2[0], a_b_gk2[0], a_o_norm[0], a_w_out[0])
    pos, ys = moe_sorted(h, 0)
    h, q, kt, v = _combine_qkv(h, pos, ys, B, S, b_norm[0], kv_norm, b_w_q[0], w_kv,
                               math.log2(math.e) / math.sqrt(head_dim))
    o = _sb_attention(q, kt, v, B, S, head_dim)
    h = _out_proj(o, b_w_out[0], h)
    pos, ys = moe_sorted(h, 1)
    h = _combine_norm(h, pos, ys, final_norm)
    return h.reshape(B, S, D)
```

```python
import functools
import math

import jax
import jax.numpy as jnp
from jax import lax
from jax.experimental import pallas as pl
from jax.experimental.pallas import tpu as pltpu

RMS_EPS = 1e-6

GLA_HEADS = 4
GLA_CHUNK = 64
CHUNK_SHIFT = GLA_CHUNK.bit_length() - 1
GATE_NORMALIZER = 16.0
SB_HEADS = 16
N_GROUPS = 4
EXPERTS_PER_GROUP = 4
N_EXPERTS = N_GROUPS * EXPERTS_PER_GROUP
PAIRS_PER_GROUP = EXPERTS_PER_GROUP * (EXPERTS_PER_GROUP - 1) // 2
N_CLASSES = N_GROUPS * PAIRS_PER_GROUP

LANES = 128
SUBLANES = 8
ROUTE_ROWS = 32
EXPERT_ROW0 = 8
VMEM_LIMIT = 56 * 1024 * 1024

GLA_ROWS = 1024
GLA_BLOCK = 256
PROJ_ROWS = 1024
ATT_TILE = 256
ATT_GROUP = 8
ATT_ZERO_MARGIN = 160.0
ATT_EXP2_CAP = 100.0
MOE_ROWS = 1024
COMBINE_QKV_ROWS = 512
MOE_TM = 512

BF16 = jnp.bfloat16
F32 = jnp.float32


def _dot(a, b):
    return jnp.dot(a, b, preferred_element_type=F32)


def _dot_nt(a, b):
    return lax.dot_general(a, b, (((1,), (1,)), ((), ())), preferred_element_type=F32)


def _split_dot(x, m01):
    hi = x.astype(BF16)
    lo = (x - hi.astype(F32)).astype(BF16)
    return _dot(hi, m01) + _dot(lo, m01)


def _rms_scale(x):
    return lax.rsqrt(jnp.mean(x * x, axis=-1, keepdims=True) + RMS_EPS)


def _log_sigmoid(x):
    return jnp.minimum(x, 0.0) - jnp.log(1.0 + jnp.exp(-jnp.abs(x)))


def _sigmoid(x):
    return 1.0 / (1.0 + jnp.exp(-x))


def _gla_block(h, states, nrm_ref, wrow_ref, wkt_ref, wgk2_ref, bgk2_ref, onorm_ref, wout_ref,
               *, dk, dv, qk_dim, v_dim, log_scale):
    R = h.shape[0]
    n_chunks = R // GLA_CHUNK
    u = (h * _rms_scale(h) * nrm_ref[...]).astype(BF16)

    proj = _dot(u, wrow_ref[...])
    q = proj[:, :qk_dim]
    v = proj[:, qk_dim:qk_dim + v_dim].astype(BF16)
    g = proj[:, qk_dim + v_dim:qk_dim + 2 * v_dim]
    lr = proj[:, qk_dim + 2 * v_dim:].astype(BF16)
    kt = _dot_nt(wkt_ref[...], u)

    gk = _log_sigmoid(_dot(lr, wgk2_ref[...]) + bgk2_ref[...]) * (1.0 / GATE_NORMALIZER)
    gkt = gk.T

    row = lax.broadcasted_iota(jnp.int32, (R, R), 0)
    col = lax.broadcasted_iota(jnp.int32, (R, R), 1)
    same_chunk = (row >> CHUNK_SHIFT) == (col >> CHUNK_SHIFT)
    causal = same_chunk & (col <= row)
    upto = jnp.where(same_chunk & (row <= col), 1.0, 0.0).astype(BF16)
    after = jnp.where(same_chunk & (row > col), 1.0, 0.0).astype(BF16)

    bt = _split_dot(gkt, upto)
    tail_t = _split_dot(gkt, after)
    b = bt.T

    q_dec = (q * jnp.exp(b + log_scale)).astype(BF16)
    k_inv_t = (kt * jnp.exp(-bt)).astype(BF16)
    k_end_t = (kt * jnp.exp(tail_t)).astype(BF16)
    chunk_decay_t = jnp.exp(bt + tail_t)

    lane_chunk = lax.broadcasted_iota(jnp.int32, (dk, R), 1) >> CHUNK_SHIFT

    acc = h
    new_states = []
    for hd in range(GLA_HEADS):
        ks = slice(hd * dk, (hd + 1) * dk)
        vs = slice(hd * dv, (hd + 1) * dv)
        qd_h = q_dec[:, ks]
        v_h = v[:, vs]
        att = _dot(qd_h, k_inv_t[ks, :])
        att = jnp.where(causal, att, 0.0).astype(BF16)
        kend_h = k_end_t[ks, :]
        kend_c = [jnp.where(lane_chunk == c, kend_h, jnp.zeros_like(kend_h)) for c in range(n_chunks)]
        ov = _dot(jnp.concatenate([att] + kend_c, axis=0), v_h)
        o_h = ov[:R]
        state = states[hd]
        inter = []
        for c in range(n_chunks):
            rows = slice(c * GLA_CHUNK, (c + 1) * GLA_CHUNK)
            inter.append(_dot(qd_h[rows], state.astype(BF16)))
            decay = chunk_decay_t[ks, c * GLA_CHUNK:c * GLA_CHUNK + 1]
            state = decay * state + ov[R + c * dk:R + (c + 1) * dk]
        new_states.append(state)
        o_h = o_h + jnp.concatenate(inter, axis=0)
        o_h = o_h * _rms_scale(o_h) * onorm_ref[...]
        g_h = g[:, vs]
        o_h = o_h * (g_h * _sigmoid(g_h))
        acc = acc + _dot(o_h.astype(BF16), wout_ref[vs, :])
    return acc, new_states


def _gla_kernel(h_ref, *refs, **dims):
    *w_refs, out_ref, state_ref = refs

    @pl.when(pl.program_id(1) == 0)
    def _():
        state_ref[...] = jnp.zeros_like(state_ref)

    states = [state_ref[hd] for hd in range(GLA_HEADS)]
    for blk in range(h_ref.shape[0] // GLA_BLOCK):
        rows = slice(blk * GLA_BLOCK, (blk + 1) * GLA_BLOCK)
        out, states = _gla_block(h_ref[rows, :], states, *w_refs, **dims)
        out_ref[rows, :] = out
    for hd in range(GLA_HEADS):
        state_ref[hd] = states[hd]


def _gla_layer(h, batch, seq, a_norm, w_in, w_gk2, b_gk2, o_norm, w_out):
    T, D = h.shape
    rank, qk_dim = w_gk2.shape
    v_dim = w_out.shape[0]
    dk = qk_dim // GLA_HEADS
    dv = v_dim // GLA_HEADS
    R = GLA_ROWS
    assert seq % R == 0 and R % GLA_CHUNK == 0 and rank <= LANES
    assert w_in.shape[1] == 2 * qk_dim + 2 * v_dim + rank
    n_s = seq // R

    o_q, o_k, o_v, o_lr, o_g = 0, qk_dim, 2 * qk_dim, 2 * qk_dim + v_dim, 2 * qk_dim + v_dim + rank
    w_lr = jnp.pad(w_in[:, o_lr:o_lr + rank], ((0, 0), (0, LANES - rank)))
    w_row = jnp.concatenate([w_in[:, o_q:o_k], w_in[:, o_v:o_lr], w_in[:, o_g:], w_lr], axis=1).astype(BF16)
    w_kt = w_in[:, o_k:o_v].T.astype(BF16)
    w_gk2p = jnp.pad(w_gk2, ((0, LANES - rank), (0, 0))).astype(BF16)
    const = lambda b, s: (0, 0)
    kern = functools.partial(_gla_kernel, dk=dk, dv=dv, qk_dim=qk_dim, v_dim=v_dim,
                             log_scale=math.log(dk ** -0.5))
    return pl.pallas_call(
        kern,
        grid=(batch, n_s),
        in_specs=[
            pl.BlockSpec((R, D), lambda b, s: (b * n_s + s, 0)),
            pl.BlockSpec((1, D), const),
            pl.BlockSpec(w_row.shape, const),
            pl.BlockSpec(w_kt.shape, const),
            pl.BlockSpec(w_gk2p.shape, const),
            pl.BlockSpec((1, qk_dim), const),
            pl.BlockSpec((1, dv), const),
            pl.BlockSpec((v_dim, D), const),
        ],
        out_specs=pl.BlockSpec((R, D), lambda b, s: (b * n_s + s, 0)),
        out_shape=jax.ShapeDtypeStruct((T, D), F32),
        scratch_shapes=[pltpu.VMEM((GLA_HEADS, dk, dv), F32)],
        compiler_params=pltpu.CompilerParams(
            dimension_semantics=("arbitrary", "arbitrary"), vmem_limit_bytes=VMEM_LIMIT),
        name="gla_layer",
    )(h, a_norm.reshape(1, D), w_row, w_kt, w_gk2p, b_gk2.reshape(1, qk_dim), o_norm.reshape(1, dv),
      w_out.astype(BF16))


def _router_kernel(h_ref, nrm_ref, wt_ref, bt_ref, upper_ref, idx_ref, wgt_ref, cnt_ref, carry_ref):
    R = h_ref.shape[0]

    @pl.when(pl.program_id(0) == 0)
    def _():
        carry_ref[...] = jnp.zeros_like(carry_ref)

    h = h_ref[...]
    t = h * _rms_scale(h) * nrm_ref[...]
    t_hi = t.astype(BF16)
    t_lo = (t - t_hi.astype(F32)).astype(BF16)
    p = _dot_nt(wt_ref[...], t_hi)
    logits = p[:ROUTE_ROWS] + p[ROUTE_ROWS:] + _dot_nt(wt_ref[:ROUTE_ROWS], t_lo) + bt_ref[:, 0:1]
    row = lax.broadcasted_iota(jnp.int32, (ROUTE_ROWS, R), 0)
    neg = jnp.float32(-jnp.inf)

    def first_argmax(vals):
        m = jnp.max(vals, axis=0, keepdims=True)
        i = jnp.min(jnp.where(vals == m, row, ROUTE_ROWS), axis=0, keepdims=True)
        return m, i

    gl = jnp.where(row < N_GROUPS, logits, neg)
    g_max, g_idx = first_argmax(gl)
    g_w = 1.0 / jnp.sum(jnp.exp(gl - g_max), axis=0, keepdims=True)

    base = EXPERT_ROW0 + g_idx * EXPERTS_PER_GROUP
    el = jnp.where((row >= base) & (row < base + EXPERTS_PER_GROUP), logits, neg)
    m1, i1 = first_argmax(el)
    m2, i2 = first_argmax(jnp.where(row == i1, neg, el))
    r = jnp.exp(m2 - m1)
    w1 = g_w / (1.0 + r)
    w2 = g_w * r / (1.0 + r)
    l1 = i1 - base
    l2 = i2 - base
    first_is_lo = l1 < l2
    lo = jnp.minimum(l1, l2)
    hi = jnp.maximum(l1, l2)
    cls = g_idx * PAIRS_PER_GROUP + ((lo * (2 * EXPERTS_PER_GROUP - 1 - lo)) >> 1) + (hi - lo - 1)
    w_lo = jnp.where(first_is_lo, w1, w2)
    w_hi = jnp.where(first_is_lo, w2, w1)

    onehot = jnp.where(row == cls, 1.0, 0.0)
    before = _dot(onehot.astype(BF16), upper_ref[...]) + carry_ref[:, 0:1]
    rank = jnp.sum(jnp.where(row == cls, before, 0.0), axis=0, keepdims=True)
    total = carry_ref[:, 0:1] + jnp.sum(onehot, axis=1, keepdims=True)
    carry_ref[...] = jnp.broadcast_to(total, carry_ref.shape)
    cnt_ref[...] = jnp.broadcast_to(total, cnt_ref.shape).astype(jnp.int32)

    row8 = lax.broadcasted_iota(jnp.int32, (SUBLANES, R), 0)
    idx_ref[0] = jnp.where(row8 == 0, cls, jnp.where(row8 == 1, rank.astype(jnp.int32), 0))
    w_rows = jnp.where(row == 0, w_lo, jnp.where(row == 1, w_hi, 0.0))
    w_rows = jnp.concatenate([w_rows, jnp.zeros((LANES - ROUTE_ROWS, R), F32)], axis=0)
    wgt_ref[...] = w_rows.T


def _router(h, m_norm, w_group, b_group, w_expert, b_expert):
    T, D = h.shape
    R = MOE_ROWS
    assert T % R == 0 and N_CLASSES <= ROUTE_ROWS
    pad_g = EXPERT_ROW0 - N_GROUPS
    pad_e = ROUTE_ROWS - EXPERT_ROW0 - N_EXPERTS
    w_t = jnp.pad(jnp.concatenate([w_group.T, jnp.zeros((pad_g, D), F32), w_expert.T], axis=0), ((0, pad_e), (0, 0)))
    w_t_hi = w_t.astype(BF16)
    w_t = jnp.concatenate([w_t_hi, (w_t - w_t_hi.astype(F32)).astype(BF16)], axis=0)
    b_t = jnp.pad(jnp.concatenate([b_group, jnp.zeros((pad_g,), F32), b_expert]), (0, pad_e))
    b_t = jnp.broadcast_to(b_t[:, None], (ROUTE_ROWS, LANES))
    idx = jnp.arange(R)
    upper = (idx[:, None] < idx[None, :]).astype(BF16)
    const = lambda i: (0, 0)
    return pl.pallas_call(
        _router_kernel,
        grid=(T // R,),
        in_specs=[
            pl.BlockSpec((R, D), lambda i: (i, 0)),
            pl.BlockSpec((1, D), const),
            pl.BlockSpec((2 * ROUTE_ROWS, D), const),
            pl.BlockSpec((ROUTE_ROWS, LANES), const),
            pl.BlockSpec((R, R), const),
        ],
        out_specs=[
            pl.BlockSpec((1, SUBLANES, R), lambda i: (i, 0, 0)),
            pl.BlockSpec((R, LANES), lambda i: (i, 0)),
            pl.BlockSpec((ROUTE_ROWS, LANES), const),
        ],
        out_shape=[
            jax.ShapeDtypeStruct((T // R, SUBLANES, R), jnp.int32),
            jax.ShapeDtypeStruct((T, LANES), F32),
            jax.ShapeDtypeStruct((ROUTE_ROWS, LANES), jnp.int32),
        ],
        scratch_shapes=[pltpu.VMEM((ROUTE_ROWS, LANES), F32)],
        compiler_params=pltpu.CompilerParams(dimension_semantics=("arbitrary",), vmem_limit_bytes=VMEM_LIMIT),
        name="moe_router",
    )(h, m_norm.reshape(1, D), w_t, b_t, upper)


def _dispatch_kernel(pos_ref, pos_prev_ref, fill_ref, h_ref, wgt_ref, xs_ref, row_ref, zero_ref, sem, fill_sem):
    i = pl.program_id(0)
    n = pl.num_programs(0)
    R, D = h_ref.shape
    G = R // SUBLANES
    slot = i % 2

    @pl.when(i == 0)
    def _():
        zero_ref[...] = jnp.zeros_like(zero_ref)
        tile_rows = zero_ref.shape[0]

        def fill_copy(k):
            first = pl.multiple_of(jnp.maximum(fill_ref[0, 0, k], 0), tile_rows)
            return pltpu.make_async_copy(zero_ref, xs_ref.at[pl.ds(first, tile_rows)], fill_sem)

        def start(k, c):
            @pl.when(fill_ref[0, 0, k] >= 0)
            def _():
                fill_copy(k).start()
            return c

        def wait(k, c):
            @pl.when(fill_ref[0, 0, k] >= 0)
            def _():
                fill_copy(k).wait()
            return c

        lax.fori_loop(0, fill_ref.shape[2], start, 0)
        lax.fori_loop(0, fill_ref.shape[2], wait, 0)

    row_ref[slot, :, :, :D] = h_ref[...].reshape(G, SUBLANES, D)
    row_ref[slot, :, :, D:] = wgt_ref[...].reshape(G, SUBLANES, LANES)

    def row_copy(p_ref, s, g, j):
        dst = p_ref[0, 0, g * SUBLANES + j]
        return pltpu.make_async_copy(row_ref.at[s, g, pl.ds(j, 1)], xs_ref.at[pl.ds(dst, 1)], sem.at[s])

    for g in range(G):
        for j in range(SUBLANES):
            row_copy(pos_ref, slot, g, j).start()

    def drain(p_ref, s):
        def body(g, c):
            for j in range(SUBLANES):
                row_copy(p_ref, s, g, j).wait()
            return c
        lax.fori_loop(0, G, body, 0)

    @pl.when(i > 0)
    def _():
        drain(pos_prev_ref, 1 - slot)

    @pl.when(i == n - 1)
    def _():
        drain(pos_ref, slot)


def _dispatch(h, wgt, pos, fill, n_rows):
    T, D = h.shape
    R = MOE_ROWS
    n_t = T // R
    pos3 = pos.reshape(n_t, 1, R)
    fill3 = fill.reshape(1, 1, -1)
    return pl.pallas_call(
        _dispatch_kernel,
        grid=(n_t,),
        in_specs=[
            pl.BlockSpec((1, 1, R), lambda i: (i, 0, 0), memory_space=pltpu.SMEM),
            pl.BlockSpec((1, 1, R), lambda i: (jnp.maximum(i - 1, 0), 0, 0), memory_space=pltpu.SMEM),
            pl.BlockSpec(fill3.shape, lambda i: (0, 0, 0), memory_space=pltpu.SMEM),
            pl.BlockSpec((R, D), lambda i: (i, 0)),
            pl.BlockSpec((R, LANES), lambda i: (i, 0)),
        ],
        out_specs=pl.BlockSpec(memory_space=pl.ANY),
        out_shape=jax.ShapeDtypeStruct((n_rows, D + LANES), F32),
        scratch_shapes=[pltpu.VMEM((2, R // SUBLANES, SUBLANES, D + LANES), F32),
                        pltpu.VMEM((MOE_TM, D + LANES), F32),
                        pltpu.SemaphoreType.DMA((2,)), pltpu.SemaphoreType.DMA],
        compiler_params=pltpu.CompilerParams(dimension_semantics=("arbitrary",), vmem_limit_bytes=VMEM_LIMIT),
        name="moe_dispatch",
    )(pos3, pos3, fill3, h, wgt)


def _experts_kernel(used_ref, lo_ref, hi_ref, xs_ref, nrm_ref, g_lo_ref, u_lo_ref, d_lo_ref, g_hi_ref, u_hi_ref,
                    d_hi_ref, ys_ref, wgu_ref, wd_ref, *, d_model, d_expert):
    i = pl.program_id(0)
    active = i < used_ref[0]
    prev = jnp.maximum(i - 1, 0)

    def refresh(k, e_ref, g_ref, u_ref, d_ref):
        @pl.when(active & ((i == 0) | (e_ref[i] != e_ref[prev])))
        def _():
            wgu_ref[k, :, :d_expert] = g_ref[0, 0].astype(BF16)
            wgu_ref[k, :, d_expert:] = u_ref[0, 0].astype(BF16)
            wd_ref[k] = d_ref[0, 0].astype(BF16)

    refresh(0, lo_ref, g_lo_ref, u_lo_ref, d_lo_ref)
    refresh(1, hi_ref, g_hi_ref, u_hi_ref, d_hi_ref)

    @pl.when(active)
    def _():
        x = xs_ref[:, :d_model]
        t = (x * _rms_scale(x) * nrm_ref[...]).astype(BF16)
        y = None
        for k in range(2):
            gu = _dot(t, wgu_ref[k])
            gate = gu[:, :d_expert]
            hdn = (gate * _sigmoid(gate) * gu[:, d_expert:]).astype(BF16)
            y_k = xs_ref[:, d_model + k:d_model + k + 1] * _dot(hdn, wd_ref[k])
            y = y_k if y is None else y + y_k
        ys_ref[...] = y

    @pl.when(jnp.logical_not(active))
    def _():
        ys_ref[...] = jnp.zeros_like(ys_ref)


def _experts(xs, n_used, tile_lo, tile_hi, m_norm, layer, w_gate, w_up, w_down):
    n_rows = xs.shape[0]
    D = m_norm.shape[0]
    d_expert = w_gate.shape[-1]
    TM = MOE_TM
    n_tiles = n_rows // TM
    in_w = lambda which: (lambda i, nu, lo, hi: (layer, which(lo, hi)[i], 0, 0))
    pick_lo = lambda lo, hi: lo
    pick_hi = lambda lo, hi: hi
    grid_spec = pltpu.PrefetchScalarGridSpec(
        num_scalar_prefetch=3,
        grid=(n_tiles,),
        in_specs=[
            pl.BlockSpec((TM, D + LANES), lambda i, nu, lo, hi: (i, 0)),
            pl.BlockSpec((1, D), lambda i, nu, lo, hi: (0, 0)),
            pl.BlockSpec((1, 1, D, d_expert), in_w(pick_lo)),
            pl.BlockSpec((1, 1, D, d_expert), in_w(pick_lo)),
            pl.BlockSpec((1, 1, d_expert, D), in_w(pick_lo)),
            pl.BlockSpec((1, 1, D, d_expert), in_w(pick_hi)),
            pl.BlockSpec((1, 1, D, d_expert), in_w(pick_hi)),
            pl.BlockSpec((1, 1, d_expert, D), in_w(pick_hi)),
        ],
        out_specs=pl.BlockSpec((TM, D), lambda i, nu, lo, hi: (i, 0)),
        scratch_shapes=[pltpu.VMEM((2, D, 2 * d_expert), BF16), pltpu.VMEM((2, d_expert, D), BF16)],
    )
    return pl.pallas_call(
        functools.partial(_experts_kernel, d_model=D, d_expert=d_expert),
        grid_spec=grid_spec,
        out_shape=jax.ShapeDtypeStruct((n_rows, D), F32),
        compiler_params=pltpu.CompilerParams(dimension_semantics=("arbitrary",), vmem_limit_bytes=VMEM_LIMIT),
        name="moe_experts",
    )(n_used, tile_lo, tile_hi, xs, m_norm.reshape(1, D), w_gate, w_up, w_down, w_gate, w_up, w_down)


def _combine_norm_kernel(pos_ref, pos_next_ref, h_ref, nrm_ref, ys_ref, out_ref, buf_ref, sem):
    i = pl.program_id(0)
    n = pl.num_programs(0)
    R, D = h_ref.shape
    G = R // SUBLANES
    slot = i % 2

    def row_copy(p_ref, s, g, j):
        src = p_ref[0, 0, g * SUBLANES + j]
        return pltpu.make_async_copy(ys_ref.at[pl.ds(src, 1)], buf_ref.at[s, g, pl.ds(j, 1)], sem.at[s])

    def issue(p_ref, s):
        def body(g, c):
            for j in range(SUBLANES):
                row_copy(p_ref, s, g, j).start()
            return c
        lax.fori_loop(0, G, body, 0)

    @pl.when(i == 0)
    def _():
        issue(pos_ref, slot)

    @pl.when(i < n - 1)
    def _():
        for g in range(G):
            for j in range(SUBLANES):
                row_copy(pos_next_ref, 1 - slot, g, j).start()

    def drain(g, c):
        for j in range(SUBLANES):
            row_copy(pos_ref, slot, g, j).wait()
        return c

    lax.fori_loop(0, G, drain, 0)

    out = h_ref[...] + buf_ref[slot].reshape(R, D)
    out_ref[...] = out * _rms_scale(out) * nrm_ref[...]


def _combine_norm(h, pos, ys, norm):
    T, D = h.shape
    R = MOE_ROWS
    n_t = T // R
    pos3 = pos.reshape(n_t, 1, R)
    return pl.pallas_call(
        _combine_norm_kernel,
        grid=(n_t,),
        in_specs=[
            pl.BlockSpec((1, 1, R), lambda i: (i, 0, 0), memory_space=pltpu.SMEM),
            pl.BlockSpec((1, 1, R), lambda i: (jnp.minimum(i + 1, n_t - 1), 0, 0), memory_space=pltpu.SMEM),
            pl.BlockSpec((R, D), lambda i: (i, 0)),
            pl.BlockSpec((1, D), lambda i: (0, 0)),
            pl.BlockSpec(memory_space=pl.ANY),
        ],
        out_specs=pl.BlockSpec((R, D), lambda i: (i, 0)),
        out_shape=jax.ShapeDtypeStruct((T, D), F32),
        scratch_shapes=[pltpu.VMEM((2, R // SUBLANES, SUBLANES, D), F32), pltpu.SemaphoreType.DMA((2,))],
        compiler_params=pltpu.CompilerParams(dimension_semantics=("arbitrary",), vmem_limit_bytes=VMEM_LIMIT),
        name="moe_combine",
    )(pos3, pos3, h, norm.reshape(1, D), ys)


def _class_experts():
    lo, hi = [], []
    for g in range(N_GROUPS):
        for a in range(EXPERTS_PER_GROUP):
            for b in range(a + 1, EXPERTS_PER_GROUP):
                lo.append(g * EXPERTS_PER_GROUP + a)
                hi.append(g * EXPERTS_PER_GROUP + b)
    return jnp.asarray(lo, jnp.int32), jnp.asarray(hi, jnp.int32)


def _moe_sorted(h, layer, m_norm, w_group, b_group, w_expert, b_expert, w_gate, w_up, w_down):
    T, D = h.shape
    TM = MOE_TM
    idx, wgt, cnt = _router(h, m_norm, w_group, b_group, w_expert, b_expert)

    counts = cnt[:N_CLASSES, 0]
    padded = ((counts + TM - 1) // TM) * TM
    ends = jnp.cumsum(padded)
    starts = ends - padded
    n_rows = T + N_CLASSES * TM
    n_tiles = n_rows // TM
    tile_start = jnp.arange(n_tiles, dtype=jnp.int32) * TM
    tile_class = jnp.minimum(
        jnp.sum((ends[None, :] <= tile_start[:, None]).astype(jnp.int32), axis=1), N_CLASSES - 1)
    class_lo, class_hi = _class_experts()
    n_used = (ends[-1] // TM).astype(jnp.int32).reshape(1)
    cls, pos = idx[:, 0, :], idx[:, 1, :]
    for c in range(N_CLASSES):
        pos = pos + jnp.where(cls == c, starts[c], 0)
    pos = pos.astype(jnp.int32)

    partial_tail = jnp.where(counts % TM != 0, ends - TM, -1)
    unused = n_used[0] + jnp.arange(N_CLASSES, dtype=jnp.int32)
    unused = jnp.where(unused < n_tiles, unused * TM, -1)
    fill = jnp.concatenate([partial_tail, unused]).astype(jnp.int32)

    xs = _dispatch(h, wgt, pos, fill, n_rows)
    ys = _experts(xs, n_used, class_lo[tile_class], class_hi[tile_class], m_norm, layer, w_gate, w_up, w_down)
    return pos, ys


def _combine_qkv_kernel(pos_ref, pos_next_ref, h_ref, ys_ref, qn_ref, kvn_ref, wq_ref, wkt_ref, wv_ref,
                        hout_ref, q_ref, kt_ref, v_ref, buf_a, buf_b, sem):
    i = pl.program_id(0)
    n = pl.num_programs(0)
    R, D = h_ref.shape
    G = R // SUBLANES

    def row_copy(p_ref, buf, s, g, j):
        src = p_ref[0, 0, g * SUBLANES + j]
        return pltpu.make_async_copy(ys_ref.at[pl.ds(src, 1)], buf.at[g, pl.ds(j, 1)], sem.at[s])

    @pl.when(i == 0)
    def _():
        def body(g, c):
            for j in range(SUBLANES):
                row_copy(pos_ref, buf_a, 0, g, j).start()
            return c
        lax.fori_loop(0, G, body, 0)

    def step(cur, cur_s, nxt, nxt_s):
        def drain(g, c):
            for j in range(SUBLANES):
                row_copy(pos_ref, cur, cur_s, g, j).wait()
            return c
        lax.fori_loop(0, G, drain, 0)

        for g in range(G):
            for j in range(SUBLANES):
                row_copy(pos_next_ref, nxt, nxt_s, g, j).start()
        h = h_ref[...] + cur[...].reshape(R, D)
        hout_ref[...] = h
        xhat = h * _rms_scale(h)
        uq = (xhat * qn_ref[...]).astype(BF16)
        ukv = (xhat * kvn_ref[...]).astype(BF16)
        q_ref[...] = _dot(uq, wq_ref[...]).astype(BF16)
        v_ref[...] = _dot(ukv, wv_ref[...]).astype(BF16)
        kt_ref[0] = _dot_nt(wkt_ref[...], ukv).astype(BF16)

        @pl.when(i == n - 1)
        def _():
            def body(g, c):
                for j in range(SUBLANES):
                    row_copy(pos_next_ref, nxt, nxt_s, g, j).wait()
                return c
            lax.fori_loop(0, G, body, 0)

    @pl.when(i % 2 == 0)
    def _():
        step(buf_a, 0, buf_b, 1)

    @pl.when(i % 2 == 1)
    def _():
        step(buf_b, 1, buf_a, 0)


def _combine_qkv(h, pos, ys, batch, seq, q_norm, kv_norm, w_q, w_kv, scale):
    T, D = h.shape
    sb_dim = w_q.shape[1]
    R = COMBINE_QKV_ROWS
    assert seq % R == 0
    n_t = T // R
    n_s = seq // R
    pos3 = pos.reshape(n_t, 1, R)
    w_qs = (w_q * scale).astype(BF16)
    w_kt = w_kv[:, :sb_dim].T.astype(BF16)
    w_v = w_kv[:, sb_dim:].astype(BF16)
    const = lambda i: (0, 0)
    return pl.pallas_call(
        _combine_qkv_kernel,
        grid=(n_t,),
        in_specs=[
            pl.BlockSpec((1, 1, R), lambda i: (i, 0, 0), memory_space=pltpu.SMEM),
            pl.BlockSpec((1, 1, R), lambda i: (jnp.minimum(i + 1, n_t - 1), 0, 0), memory_space=pltpu.SMEM),
            pl.BlockSpec((R, D), lambda i: (i, 0)),
            pl.BlockSpec(memory_space=pl.ANY),
            pl.BlockSpec((1, D), const),
            pl.BlockSpec((1, D), const),
            pl.BlockSpec((D, sb_dim), const),
            pl.BlockSpec((sb_dim, D), const),
            pl.BlockSpec((D, sb_dim), const),
        ],
        out_specs=[
            pl.BlockSpec((R, D), lambda i: (i, 0)),
            pl.BlockSpec((R, sb_dim), lambda i: (i, 0)),
            pl.BlockSpec((1, sb_dim, R), lambda i: (i // n_s, 0, i % n_s)),
            pl.BlockSpec((R, sb_dim), lambda i: (i, 0)),
        ],
        out_shape=[
            jax.ShapeDtypeStruct((T, D), F32),
            jax.ShapeDtypeStruct((T, sb_dim), BF16),
            jax.ShapeDtypeStruct((batch, sb_dim, seq), BF16),
            jax.ShapeDtypeStruct((T, sb_dim), BF16),
        ],
        scratch_shapes=[pltpu.VMEM((R // SUBLANES, SUBLANES, D), F32), pltpu.VMEM((R // SUBLANES, SUBLANES, D), F32),
                        pltpu.SemaphoreType.DMA((2,))],
        compiler_params=pltpu.CompilerParams(dimension_semantics=("arbitrary",), vmem_limit_bytes=VMEM_LIMIT),
        name="combine_qkv",
    )(pos3, pos3, h, ys, q_norm.reshape(1, D), kv_norm.reshape(1, D), w_qs, w_kt, w_v)


def _sb_attention_kernel(q_ref, kt_ref, v_ref, o_ref, *, head_dim):
    S = q_ref.shape[0]
    T = ATT_TILE
    n_q = S // T
    n_heads = LANES // head_dim
    lane_q = lax.broadcasted_iota(jnp.int32, (T, LANES), 1)
    trow = lax.broadcasted_iota(jnp.int32, (T, T), 0)
    scol = lax.broadcasted_iota(jnp.int32, (T, T), 1)
    strictly_before = scol < trow
    suffix = jnp.where(trow >= scol, 1.0, 0.0).astype(BF16)
    in_head = [(lane_q >= hd * head_dim) & (lane_q < (hd + 1) * head_dim) for hd in range(n_heads)]

    ksq = jnp.square(kt_ref[0].astype(F32))
    k_max = [jnp.sqrt(jnp.max(jnp.sum(ksq[hd * head_dim:(hd + 1) * head_dim], axis=0, keepdims=True),
                              axis=1, keepdims=True)) for hd in range(n_heads)]

    def softplus2(z):
        return jnp.maximum(z, jnp.log2(1.0 + jnp.exp2(jnp.minimum(z, ATT_EXP2_CAP))))

    def tile(q_h, k0, acc, run, mask):
        z = _dot(q_h, kt_ref[0, :, pl.ds(k0, T)])
        sp = softplus2(z)
        if mask is not None:
            sp = jnp.where(mask, sp, 0.0)
        within = _dot(sp.astype(BF16), suffix)
        a = jnp.exp2(z - within - run)
        if mask is not None:
            a = jnp.where(mask, a, 0.0)
        acc = acc + _dot(a.astype(BF16), v_ref[pl.ds(k0, T), :])
        return acc, run + within[:, 0:1]

    def all_zero_from_here(runs, z_bound):
        slack = runs[0] - z_bound[0]
        for r, zb in zip(runs[1:], z_bound[1:]):
            slack = jnp.minimum(slack, r - zb)
        return (jnp.min(slack) > ATT_ZERO_MARGIN).astype(jnp.int32)

    def block_tail(qi, with_left, q0, q_hs, z_bound, accs, runs, done):
        def cond(c):
            return (c[0] >= 0) & (c[1] == 0)

        def body(c):
            j, _, accs, runs = c
            k0 = pl.multiple_of(j * T, T)
            st = [tile(q_h, k0, a, r, None) for q_h, a, r in zip(q_hs, accs, runs)]
            accs = [a for a, _ in st]
            runs = [r for _, r in st]
            return j - 1, all_zero_from_here(runs, z_bound), accs, runs

        if with_left:
            _, _, accs, _ = lax.while_loop(cond, body, (qi - 2, done, accs, runs))
        result = accs[0]
        for m, a in zip(in_head[1:], accs[1:]):
            result = jnp.where(m, a, result)
        o_ref[pl.ds(q0, T), :] = result.astype(o_ref.dtype)

    def group_head(q_first, first_has_left):
        n_h = len(in_head)
        U = n_h * T
        G = ATT_GROUP
        static = isinstance(q_first, int)
        mult = (lambda x: x) if static else (lambda x: pl.multiple_of(x, T))
        q0 = [mult((q_first + g) * T) for g in range(G)]
        q_hs, z_bound = [], []
        for g in range(G):
            q_pair = q_ref[pl.ds(q0[g], T), :]
            qsq = jnp.square(q_pair.astype(F32))
            q_hs.append([jnp.where(m, q_pair, jnp.zeros_like(q_pair)) for m in in_head])
            z_bound.append([jnp.sqrt(jnp.sum(jnp.where(m, qsq, 0.0), axis=-1, keepdims=True)) * km
                            for m, km in zip(in_head, k_max)])
        diag_mask = jnp.concatenate([strictly_before] * n_h, axis=0)

        tiles = ([(mult(q_first * T - T), [("left", 0)])] if first_has_left else [])
        for j in range(G):
            tiles.append((q0[j], [("diag", j)] + ([("left", j + 1)] if j + 1 < G else [])))

        z, sp = {}, []
        order = []
        for k0, units in tiles:
            zz = _dot(jnp.concatenate([q_h for _, g in units for q_h in q_hs[g]], axis=0), kt_ref[0, :, pl.ds(k0, T)])
            for i, (kind, g) in enumerate(units):
                z_u = zz[i * U:(i + 1) * U]
                sp_u = softplus2(z_u)
                if kind == "diag":
                    sp_u = jnp.where(diag_mask, sp_u, 0.0)
                z[(kind, g)] = z_u
                sp.append(sp_u)
                order.append((kind, g))
        within = _dot(jnp.concatenate(sp, axis=0).astype(BF16), suffix)
        w = {key: within[i * U:(i + 1) * U] for i, key in enumerate(order)}

        a, run = {}, {}
        for g in range(G):
            run[g] = w[("diag", g)][:, 0:1]
            a[("diag", g)] = jnp.where(diag_mask, jnp.exp2(z[("diag", g)] - w[("diag", g)]), 0.0)
            if ("left", g) in w:
                a[("left", g)] = jnp.exp2(z[("left", g)] - w[("left", g)] - run[g])
                run[g] = run[g] + w[("left", g)][:, 0:1]
        acc = {}
        for k0, units in tiles:
            o = _dot(jnp.concatenate([a[key] for key in units], axis=0).astype(BF16), v_ref[pl.ds(k0, T), :])
            for i, (kind, g) in enumerate(units):
                o_u = o[i * U:(i + 1) * U]
                acc[g] = o_u if g not in acc else acc[g] + o_u
        split = lambda x: [x[hd * T:(hd + 1) * T] for hd in range(n_h)]
        out = []
        for g in range(G):
            runs = split(run[g])
            out.append((q0[g], q_hs[g], z_bound[g], split(acc[g]), runs, all_zero_from_here(runs, z_bound[g])))
        return out

    def block_group(p, first_has_left):
        q_first = ATT_GROUP * p
        heads = group_head(q_first, first_has_left)
        for g, hd in enumerate(heads):
            block_tail(q_first + g, first_has_left or g > 0, *hd)

    block_group(0, False)

    def later_group(p, carry):
        block_group(p, True)
        return carry

    lax.fori_loop(1, n_q // ATT_GROUP, later_group, 0)


def _sb_attention(q, kt, v, batch, seq, head_dim):
    T, sb_dim = q.shape
    assert LANES % head_dim == 0 and seq % (ATT_GROUP * ATT_TILE) == 0
    n_p = sb_dim // LANES
    return pl.pallas_call(
        functools.partial(_sb_attention_kernel, head_dim=head_dim),
        grid=(batch, n_p),
        in_specs=[
            pl.BlockSpec((seq, LANES), lambda b, p: (b, p)),
            pl.BlockSpec((1, LANES, seq), lambda b, p: (b, p, 0)),
            pl.BlockSpec((seq, LANES), lambda b, p: (b, p)),
        ],
        out_specs=pl.BlockSpec((seq, LANES), lambda b, p: (b, p)),
        out_shape=jax.ShapeDtypeStruct((T, sb_dim), BF16),
        compiler_params=pltpu.CompilerParams(
            dimension_semantics=("arbitrary", "arbitrary"), vmem_limit_bytes=VMEM_LIMIT),
        name="sb_attention",
    )(q, kt, v)


def _out_proj_kernel(o_ref, w_ref, h_ref, out_ref):
    out_ref[...] = h_ref[...] + _dot(o_ref[...], w_ref[...])


def _out_proj(o, w_out, h):
    T, D = h.shape
    R = PROJ_ROWS
    return pl.pallas_call(
        _out_proj_kernel,
        grid=(T // R,),
        in_specs=[
            pl.BlockSpec((R, o.shape[1]), lambda i: (i, 0)),
            pl.BlockSpec(w_out.shape, lambda i: (0, 0)),
            pl.BlockSpec((R, D), lambda i: (i, 0)),
        ],
        out_specs=pl.BlockSpec((R, D), lambda i: (i, 0)),
        out_shape=jax.ShapeDtypeStruct((T, D), F32),
        compiler_params=pltpu.CompilerParams(dimension_semantics=("arbitrary",), vmem_limit_bytes=VMEM_LIMIT),
        name="out_proj",
    )(o, w_out.astype(BF16), h)


def kernel(x, a_norm, a_w_in, a_w_gk2, a_b_gk2, a_o_norm, a_w_out, kv_norm, w_kv, b_norm, b_w_q, b_w_out,
           m_norm, m_w_group, m_b_group, m_w_expert, m_b_expert, m_w_gate, m_w_up, m_w_down, final_norm):
    B, S, D = x.shape
    assert a_norm.shape[0] == 1 and b_norm.shape[0] == 1 and m_norm.shape[0] == 2
    head_dim = b_w_q.shape[2] // SB_HEADS
    h = x.reshape(B * S, D)

    def moe_sorted(h, layer):
        return _moe_sorted(h, layer, m_norm[layer], m_w_group[layer], m_b_group[layer], m_w_expert[layer],
                           m_b_expert[layer], m_w_gate, m_w_up, m_w_down)

    h = _gla_layer(h, B, S, a_norm[0], a_w_in[0], a_w_gk2[0], a_b_gk2[0], a_o_norm[0], a_w_out[0])
    pos, ys = moe_sorted(h, 0)
    h, q, kt, v = _combine_qkv(h, pos, ys, B, S, b_norm[0], kv_norm, b_w_q[0], w_kv,
                               math.log2(math.e) / math.sqrt(head_dim))
    o = _sb_attention(q, kt, v, B, S, head_dim)
    h = _out_proj(o, b_w_out[0], h)
    pos, ys = moe_sorted(h, 1)
    h = _combine_norm(h, pos, ys, final_norm)
    return h.reshape(B, S, D)
```

```python
import functools
import math

import jax
import jax.numpy as jnp
from jax import lax
from jax.experimental import pallas as pl
from jax.experimental.pallas import tpu as pltpu

RMS_EPS = 1e-6

GLA_HEADS = 4
GLA_CHUNK = 64
CHUNK_SHIFT = GLA_CHUNK.bit_length() - 1
GATE_NORMALIZER = 16.0
SB_HEADS = 16
N_GROUPS = 4
EXPERTS_PER_GROUP = 4
N_EXPERTS = N_GROUPS * EXPERTS_PER_GROUP
PAIRS_PER_GROUP = EXPERTS_PER_GROUP * (EXPERTS_PER_GROUP - 1) // 2
N_CLASSES = N_GROUPS * PAIRS_PER_GROUP

LANES = 128
SUBLANES = 8
ROUTE_ROWS = 32
EXPERT_ROW0 = 8
VMEM_LIMIT = 56 * 1024 * 1024

GLA_ROWS = 1024
GLA_BLOCK = 256
PROJ_ROWS = 1024
ATT_TILE = 256
ATT_GROUP = 16
ATT_ZERO_MARGIN = 160.0
ATT_EXP2_CAP = 100.0
MOE_ROWS = 1024
COMBINE_ROWS = 512
MOE_TM = 512

BF16 = jnp.bfloat16
F32 = jnp.float32


def _dot(a, b):
    return jnp.dot(a, b, preferred_element_type=F32)


def _dot_nt(a, b):
    return lax.dot_general(a, b, (((1,), (1,)), ((), ())), preferred_element_type=F32)


def _split_dot(x, m01):
    hi = x.astype(BF16)
    lo = (x - hi.astype(F32)).astype(BF16)
    return _dot(hi, m01) + _dot(lo, m01)


def _rms_scale(x):
    return lax.rsqrt(jnp.mean(x * x, axis=-1, keepdims=True) + RMS_EPS)


def _log_sigmoid(x):
    return jnp.minimum(x, 0.0) - jnp.log(1.0 + jnp.exp(-jnp.abs(x)))


def _sigmoid(x):
    return 1.0 / (1.0 + jnp.exp(-x))


def _gla_block(h, states, nrm_ref, wrow_ref, wkt_ref, wgk2_ref, bgk2_ref, onorm_ref, wout_ref,
               *, dk, dv, qk_dim, v_dim, log_scale):
    R = h.shape[0]
    n_chunks = R // GLA_CHUNK
    u = (h * _rms_scale(h) * nrm_ref[...]).astype(BF16)

    proj = _dot(u, wrow_ref[...])
    q = proj[:, :qk_dim]
    v = proj[:, qk_dim:qk_dim + v_dim].astype(BF16)
    g = proj[:, qk_dim + v_dim:qk_dim + 2 * v_dim]
    lr = proj[:, qk_dim + 2 * v_dim:].astype(BF16)
    kt = _dot_nt(wkt_ref[...], u)

    gk = _log_sigmoid(_dot(lr, wgk2_ref[...]) + bgk2_ref[...]) * (1.0 / GATE_NORMALIZER)
    gkt = gk.T

    row = lax.broadcasted_iota(jnp.int32, (R, R), 0)
    col = lax.broadcasted_iota(jnp.int32, (R, R), 1)
    same_chunk = (row >> CHUNK_SHIFT) == (col >> CHUNK_SHIFT)
    causal = same_chunk & (col <= row)
    upto = jnp.where(same_chunk & (row <= col), 1.0, 0.0).astype(BF16)
    after = jnp.where(same_chunk & (row > col), 1.0, 0.0).astype(BF16)

    bt = _split_dot(gkt, upto)
    tail_t = _split_dot(gkt, after)
    b = bt.T

    q_dec = (q * jnp.exp(b + log_scale)).astype(BF16)
    k_inv_t = (kt * jnp.exp(-bt)).astype(BF16)
    k_end_t = (kt * jnp.exp(tail_t)).astype(BF16)
    chunk_decay_t = jnp.exp(bt + tail_t)

    lane_chunk = lax.broadcasted_iota(jnp.int32, (dk, R), 1) >> CHUNK_SHIFT

    acc = h
    new_states = []
    for hd in range(GLA_HEADS):
        ks = slice(hd * dk, (hd + 1) * dk)
        vs = slice(hd * dv, (hd + 1) * dv)
        qd_h = q_dec[:, ks]
        v_h = v[:, vs]
        att = _dot(qd_h, k_inv_t[ks, :])
        att = jnp.where(causal, att, 0.0).astype(BF16)
        kend_h = k_end_t[ks, :]
        kend_c = [jnp.where(lane_chunk == c, kend_h, jnp.zeros_like(kend_h)) for c in range(n_chunks)]
        ov = _dot(jnp.concatenate([att] + kend_c, axis=0), v_h)
        o_h = ov[:R]
        state = states[hd]
        inter = []
        for c in range(n_chunks):
            rows = slice(c * GLA_CHUNK, (c + 1) * GLA_CHUNK)
            inter.append(_dot(qd_h[rows], state.astype(BF16)))
            decay = chunk_decay_t[ks, c * GLA_CHUNK:c * GLA_CHUNK + 1]
            state = decay * state + ov[R + c * dk:R + (c + 1) * dk]
        new_states.append(state)
        o_h = o_h + jnp.concatenate(inter, axis=0)
        o_h = o_h * _rms_scale(o_h) * onorm_ref[...]
        g_h = g[:, vs]
        o_h = o_h * (g_h * _sigmoid(g_h))
        acc = acc + _dot(o_h.astype(BF16), wout_ref[vs, :])
    return acc, new_states


def _gla_kernel(h_ref, *refs, **dims):
    *w_refs, out_ref, state_ref = refs

    @pl.when(pl.program_id(1) == 0)
    def _():
        state_ref[...] = jnp.zeros_like(state_ref)

    states = [state_ref[hd] for hd in range(GLA_HEADS)]
    for blk in range(h_ref.shape[0] // GLA_BLOCK):
        rows = slice(blk * GLA_BLOCK, (blk + 1) * GLA_BLOCK)
        out, states = _gla_block(h_ref[rows, :], states, *w_refs, **dims)
        out_ref[rows, :] = out
    for hd in range(GLA_HEADS):
        state_ref[hd] = states[hd]


def _gla_layer(h, batch, seq, a_norm, w_in, w_gk2, b_gk2, o_norm, w_out):
    T, D = h.shape
    rank, qk_dim = w_gk2.shape
    v_dim = w_out.shape[0]
    dk = qk_dim // GLA_HEADS
    dv = v_dim // GLA_HEADS
    R = GLA_ROWS
    assert seq % R == 0 and R % GLA_CHUNK == 0 and rank <= LANES
    assert w_in.shape[1] == 2 * qk_dim + 2 * v_dim + rank
    n_s = seq // R

    o_q, o_k, o_v, o_lr, o_g = 0, qk_dim, 2 * qk_dim, 2 * qk_dim + v_dim, 2 * qk_dim + v_dim + rank
    w_lr = jnp.pad(w_in[:, o_lr:o_lr + rank], ((0, 0), (0, LANES - rank)))
    w_row = jnp.concatenate([w_in[:, o_q:o_k], w_in[:, o_v:o_lr], w_in[:, o_g:], w_lr], axis=1).astype(BF16)
    w_kt = w_in[:, o_k:o_v].T.astype(BF16)
    w_gk2p = jnp.pad(w_gk2, ((0, LANES - rank), (0, 0))).astype(BF16)
    const = lambda b, s: (0, 0)
    kern = functools.partial(_gla_kernel, dk=dk, dv=dv, qk_dim=qk_dim, v_dim=v_dim,
                             log_scale=math.log(dk ** -0.5))
    return pl.pallas_call(
        kern,
        grid=(batch, n_s),
        in_specs=[
            pl.BlockSpec((R, D), lambda b, s: (b * n_s + s, 0)),
            pl.BlockSpec((1, D), const),
            pl.BlockSpec(w_row.shape, const),
            pl.BlockSpec(w_kt.shape, const),
            pl.BlockSpec(w_gk2p.shape, const),
            pl.BlockSpec((1, qk_dim), const),
            pl.BlockSpec((1, dv), const),
            pl.BlockSpec((v_dim, D), const),
        ],
        out_specs=pl.BlockSpec((R, D), lambda b, s: (b * n_s + s, 0)),
        out_shape=jax.ShapeDtypeStruct((T, D), F32),
        scratch_shapes=[pltpu.VMEM((GLA_HEADS, dk, dv), F32)],
        compiler_params=pltpu.CompilerParams(
            dimension_semantics=("arbitrary", "arbitrary"), vmem_limit_bytes=VMEM_LIMIT),
        name="gla_layer",
    )(h, a_norm.reshape(1, D), w_row, w_kt, w_gk2p, b_gk2.reshape(1, qk_dim), o_norm.reshape(1, dv),
      w_out.astype(BF16))


def _router_kernel(h_ref, nrm_ref, wt_ref, bt_ref, upper_ref, idx_ref, wgt_ref, cnt_ref, carry_ref):
    R = h_ref.shape[0]

    @pl.when(pl.program_id(0) == 0)
    def _():
        carry_ref[...] = jnp.zeros_like(carry_ref)

    h = h_ref[...]
    t = h * _rms_scale(h) * nrm_ref[...]
    t_hi = t.astype(BF16)
    t_lo = (t - t_hi.astype(F32)).astype(BF16)
    p = _dot_nt(wt_ref[...], t_hi)
    logits = p[:ROUTE_ROWS] + p[ROUTE_ROWS:] + _dot_nt(wt_ref[:ROUTE_ROWS], t_lo) + bt_ref[:, 0:1]
    row = lax.broadcasted_iota(jnp.int32, (ROUTE_ROWS, R), 0)
    neg = jnp.float32(-jnp.inf)

    def first_argmax(vals):
        m = jnp.max(vals, axis=0, keepdims=True)
        i = jnp.min(jnp.where(vals == m, row, ROUTE_ROWS), axis=0, keepdims=True)
        return m, i

    gl = jnp.where(row < N_GROUPS, logits, neg)
    g_max, g_idx = first_argmax(gl)
    g_w = 1.0 / jnp.sum(jnp.exp(gl - g_max), axis=0, keepdims=True)

    base = EXPERT_ROW0 + g_idx * EXPERTS_PER_GROUP
    el = jnp.where((row >= base) & (row < base + EXPERTS_PER_GROUP), logits, neg)
    m1, i1 = first_argmax(el)
    m2, i2 = first_argmax(jnp.where(row == i1, neg, el))
    r = jnp.exp(m2 - m1)
    w1 = g_w / (1.0 + r)
    w2 = g_w * r / (1.0 + r)
    l1 = i1 - base
    l2 = i2 - base
    first_is_lo = l1 < l2
    lo = jnp.minimum(l1, l2)
    hi = jnp.maximum(l1, l2)
    cls = g_idx * PAIRS_PER_GROUP + ((lo * (2 * EXPERTS_PER_GROUP - 1 - lo)) >> 1) + (hi - lo - 1)
    w_lo = jnp.where(first_is_lo, w1, w2)
    w_hi = jnp.where(first_is_lo, w2, w1)

    onehot = jnp.where(row == cls, 1.0, 0.0)
    before = _dot(onehot.astype(BF16), upper_ref[...]) + carry_ref[:, 0:1]
    rank = jnp.sum(jnp.where(row == cls, before, 0.0), axis=0, keepdims=True)
    total = carry_ref[:, 0:1] + jnp.sum(onehot, axis=1, keepdims=True)
    carry_ref[...] = jnp.broadcast_to(total, carry_ref.shape)
    cnt_ref[...] = jnp.broadcast_to(total, cnt_ref.shape).astype(jnp.int32)

    row8 = lax.broadcasted_iota(jnp.int32, (SUBLANES, R), 0)
    idx_ref[0] = jnp.where(row8 == 0, cls, jnp.where(row8 == 1, rank.astype(jnp.int32), 0))
    w_rows = jnp.where(row == 0, w_lo, jnp.where(row == 1, w_hi, 0.0))
    w_rows = jnp.concatenate([w_rows, jnp.zeros((LANES - ROUTE_ROWS, R), F32)], axis=0)
    wgt_ref[...] = w_rows.T


def _router(h, m_norm, w_group, b_group, w_expert, b_expert):
    T, D = h.shape
    R = MOE_ROWS
    assert T % R == 0 and N_CLASSES <= ROUTE_ROWS
    pad_g = EXPERT_ROW0 - N_GROUPS
    pad_e = ROUTE_ROWS - EXPERT_ROW0 - N_EXPERTS
    w_t = jnp.pad(jnp.concatenate([w_group.T, jnp.zeros((pad_g, D), F32), w_expert.T], axis=0), ((0, pad_e), (0, 0)))
    w_t_hi = w_t.astype(BF16)
    w_t = jnp.concatenate([w_t_hi, (w_t - w_t_hi.astype(F32)).astype(BF16)], axis=0)
    b_t = jnp.pad(jnp.concatenate([b_group, jnp.zeros((pad_g,), F32), b_expert]), (0, pad_e))
    b_t = jnp.broadcast_to(b_t[:, None], (ROUTE_ROWS, LANES))
    idx = jnp.arange(R)
    upper = (idx[:, None] < idx[None, :]).astype(BF16)
    const = lambda i: (0, 0)
    return pl.pallas_call(
        _router_kernel,
        grid=(T // R,),
        in_specs=[
            pl.BlockSpec((R, D), lambda i: (i, 0)),
            pl.BlockSpec((1, D), const),
            pl.BlockSpec((2 * ROUTE_ROWS, D), const),
            pl.BlockSpec((ROUTE_ROWS, LANES), const),
            pl.BlockSpec((R, R), const),
        ],
        out_specs=[
            pl.BlockSpec((1, SUBLANES, R), lambda i: (i, 0, 0)),
            pl.BlockSpec((R, LANES), lambda i: (i, 0)),
            pl.BlockSpec((ROUTE_ROWS, LANES), const),
        ],
        out_shape=[
            jax.ShapeDtypeStruct((T // R, SUBLANES, R), jnp.int32),
            jax.ShapeDtypeStruct((T, LANES), F32),
            jax.ShapeDtypeStruct((ROUTE_ROWS, LANES), jnp.int32),
        ],
        scratch_shapes=[pltpu.VMEM((ROUTE_ROWS, LANES), F32)],
        compiler_params=pltpu.CompilerParams(dimension_semantics=("arbitrary",), vmem_limit_bytes=VMEM_LIMIT),
        name="moe_router",
    )(h, m_norm.reshape(1, D), w_t, b_t, upper)


def _dispatch_kernel(pos_ref, pos_prev_ref, fill_ref, h_ref, wgt_ref, xs_ref, row_ref, zero_ref, sem, fill_sem):
    i = pl.program_id(0)
    n = pl.num_programs(0)
    R, D = h_ref.shape
    G = R // SUBLANES
    slot = i % 2

    @pl.when(i == 0)
    def _():
        zero_ref[...] = jnp.zeros_like(zero_ref)
        tile_rows = zero_ref.shape[0]

        def fill_copy(k):
            first = pl.multiple_of(jnp.maximum(fill_ref[0, 0, k], 0), tile_rows)
            return pltpu.make_async_copy(zero_ref, xs_ref.at[pl.ds(first, tile_rows)], fill_sem)

        def start(k, c):
            @pl.when(fill_ref[0, 0, k] >= 0)
            def _():
                fill_copy(k).start()
            return c

        def wait(k, c):
            @pl.when(fill_ref[0, 0, k] >= 0)
            def _():
                fill_copy(k).wait()
            return c

        lax.fori_loop(0, fill_ref.shape[2], start, 0)
        lax.fori_loop(0, fill_ref.shape[2], wait, 0)

    row_ref[slot, :, :, :D] = h_ref[...].reshape(G, SUBLANES, D)
    row_ref[slot, :, :, D:] = wgt_ref[...].reshape(G, SUBLANES, LANES)

    def row_copy(p_ref, s, g, j):
        dst = p_ref[0, 0, g * SUBLANES + j]
        return pltpu.make_async_copy(row_ref.at[s, g, pl.ds(j, 1)], xs_ref.at[pl.ds(dst, 1)], sem.at[s])

    for g in range(G):
        for j in range(SUBLANES):
            row_copy(pos_ref, slot, g, j).start()

    def drain(p_ref, s):
        def body(g, c):
            for j in range(SUBLANES):
                row_copy(p_ref, s, g, j).wait()
            return c
        lax.fori_loop(0, G, body, 0)

    @pl.when(i > 0)
    def _():
        drain(pos_prev_ref, 1 - slot)

    @pl.when(i == n - 1)
    def _():
        drain(pos_ref, slot)


def _dispatch(h, wgt, pos, fill, n_rows):
    T, D = h.shape
    R = MOE_ROWS
    n_t = T // R
    pos3 = pos.reshape(n_t, 1, R)
    fill3 = fill.reshape(1, 1, -1)
    return pl.pallas_call(
        _dispatch_kernel,
        grid=(n_t,),
        in_specs=[
            pl.BlockSpec((1, 1, R), lambda i: (i, 0, 0), memory_space=pltpu.SMEM),
            pl.BlockSpec((1, 1, R), lambda i: (jnp.maximum(i - 1, 0), 0, 0), memory_space=pltpu.SMEM),
            pl.BlockSpec(fill3.shape, lambda i: (0, 0, 0), memory_space=pltpu.SMEM),
            pl.BlockSpec((R, D), lambda i: (i, 0)),
            pl.BlockSpec((R, LANES), lambda i: (i, 0)),
        ],
        out_specs=pl.BlockSpec(memory_space=pl.ANY),
        out_shape=jax.ShapeDtypeStruct((n_rows, D + LANES), F32),
        scratch_shapes=[pltpu.VMEM((2, R // SUBLANES, SUBLANES, D + LANES), F32),
                        pltpu.VMEM((MOE_TM, D + LANES), F32),
                        pltpu.SemaphoreType.DMA((2,)), pltpu.SemaphoreType.DMA],
        compiler_params=pltpu.CompilerParams(dimension_semantics=("arbitrary",), vmem_limit_bytes=VMEM_LIMIT),
        name="moe_dispatch",
    )(pos3, pos3, fill3, h, wgt)


def _experts_kernel(used_ref, lo_ref, hi_ref, xs_ref, nrm_ref, g_lo_ref, u_lo_ref, d_lo_ref, g_hi_ref, u_hi_ref,
                    d_hi_ref, ys_ref, wgu_ref, wd_ref, *, d_model, d_expert):
    i = pl.program_id(0)
    active = i < used_ref[0]
    prev = jnp.maximum(i - 1, 0)

    def refresh(k, e_ref, g_ref, u_ref, d_ref):
        @pl.when(active & ((i == 0) | (e_ref[i] != e_ref[prev])))
        def _():
            wgu_ref[k, :, :d_expert] = g_ref[0, 0].astype(BF16)
            wgu_ref[k, :, d_expert:] = u_ref[0, 0].astype(BF16)
            wd_ref[k] = d_ref[0, 0].astype(BF16)

    refresh(0, lo_ref, g_lo_ref, u_lo_ref, d_lo_ref)
    refresh(1, hi_ref, g_hi_ref, u_hi_ref, d_hi_ref)

    @pl.when(active)
    def _():
        x = xs_ref[:, :d_model]
        t = (x * _rms_scale(x) * nrm_ref[...]).astype(BF16)
        y = None
        for k in range(2):
            gu = _dot(t, wgu_ref[k])
            gate = gu[:, :d_expert]
            hdn = (gate * _sigmoid(gate) * gu[:, d_expert:]).astype(BF16)
            y_k = xs_ref[:, d_model + k:d_model + k + 1] * _dot(hdn, wd_ref[k])
            y = y_k if y is None else y + y_k
        ys_ref[...] = y

    @pl.when(jnp.logical_not(active))
    def _():
        ys_ref[...] = jnp.zeros_like(ys_ref)


def _experts(xs, n_used, tile_lo, tile_hi, m_norm, layer, w_gate, w_up, w_down):
    n_rows = xs.shape[0]
    D = m_norm.shape[0]
    d_expert = w_gate.shape[-1]
    TM = MOE_TM
    n_tiles = n_rows // TM
    in_w = lambda which: (lambda i, nu, lo, hi: (layer, which(lo, hi)[i], 0, 0))
    pick_lo = lambda lo, hi: lo
    pick_hi = lambda lo, hi: hi
    grid_spec = pltpu.PrefetchScalarGridSpec(
        num_scalar_prefetch=3,
        grid=(n_tiles,),
        in_specs=[
            pl.BlockSpec((TM, D + LANES), lambda i, nu, lo, hi: (i, 0)),
            pl.BlockSpec((1, D), lambda i, nu, lo, hi: (0, 0)),
            pl.BlockSpec((1, 1, D, d_expert), in_w(pick_lo)),
            pl.BlockSpec((1, 1, D, d_expert), in_w(pick_lo)),
            pl.BlockSpec((1, 1, d_expert, D), in_w(pick_lo)),
            pl.BlockSpec((1, 1, D, d_expert), in_w(pick_hi)),
            pl.BlockSpec((1, 1, D, d_expert), in_w(pick_hi)),
            pl.BlockSpec((1, 1, d_expert, D), in_w(pick_hi)),
        ],
        out_specs=pl.BlockSpec((TM, D), lambda i, nu, lo, hi: (i, 0)),
        scratch_shapes=[pltpu.VMEM((2, D, 2 * d_expert), BF16), pltpu.VMEM((2, d_expert, D), BF16)],
    )
    return pl.pallas_call(
        functools.partial(_experts_kernel, d_model=D, d_expert=d_expert),
        grid_spec=grid_spec,
        out_shape=jax.ShapeDtypeStruct((n_rows, D), F32),
        compiler_params=pltpu.CompilerParams(dimension_semantics=("arbitrary",), vmem_limit_bytes=VMEM_LIMIT),
        name="moe_experts",
    )(n_used, tile_lo, tile_hi, xs, m_norm.reshape(1, D), w_gate, w_up, w_down, w_gate, w_up, w_down)


def _combine_norm_kernel(pos_ref, pos_next_ref, h_ref, nrm_ref, ys_ref, out_ref, buf_ref, sem):
    i = pl.program_id(0)
    n = pl.num_programs(0)
    R, D = h_ref.shape
    G = R // SUBLANES
    slot = i % 2

    def row_copy(p_ref, s, g, j):
        src = p_ref[0, 0, g * SUBLANES + j]
        return pltpu.make_async_copy(ys_ref.at[pl.ds(src, 1)], buf_ref.at[s, g, pl.ds(j, 1)], sem.at[s])

    def issue(p_ref, s):
        def body(g, c):
            for j in range(SUBLANES):
                row_copy(p_ref, s, g, j).start()
            return c
        lax.fori_loop(0, G, body, 0)

    @pl.when(i == 0)
    def _():
        issue(pos_ref, slot)

    @pl.when(i < n - 1)
    def _():
        for g in range(G):
            for j in range(SUBLANES):
                row_copy(pos_next_ref, 1 - slot, g, j).start()

    def drain(g, c):
        for j in range(SUBLANES):
            row_copy(pos_ref, slot, g, j).wait()
        return c

    lax.fori_loop(0, G, drain, 0)

    out = h_ref[...] + buf_ref[slot].reshape(R, D)
    out_ref[...] = out * _rms_scale(out) * nrm_ref[...]


def _combine_norm(h, pos, ys, norm):
    T, D = h.shape
    R = COMBINE_ROWS
    n_t = T // R
    pos3 = pos.reshape(n_t, 1, R)
    return pl.pallas_call(
        _combine_norm_kernel,
        grid=(n_t,),
        in_specs=[
            pl.BlockSpec((1, 1, R), lambda i: (i, 0, 0), memory_space=pltpu.SMEM),
            pl.BlockSpec((1, 1, R), lambda i: (jnp.minimum(i + 1, n_t - 1), 0, 0), memory_space=pltpu.SMEM),
            pl.BlockSpec((R, D), lambda i: (i, 0)),
            pl.BlockSpec((1, D), lambda i: (0, 0)),
            pl.BlockSpec(memory_space=pl.ANY),
        ],
        out_specs=pl.BlockSpec((R, D), lambda i: (i, 0)),
        out_shape=jax.ShapeDtypeStruct((T, D), F32),
        scratch_shapes=[pltpu.VMEM((2, R // SUBLANES, SUBLANES, D), F32), pltpu.SemaphoreType.DMA((2,))],
        compiler_params=pltpu.CompilerParams(dimension_semantics=("arbitrary",), vmem_limit_bytes=VMEM_LIMIT),
        name="moe_combine",
    )(pos3, pos3, h, norm.reshape(1, D), ys)


def _class_experts():
    lo, hi = [], []
    for g in range(N_GROUPS):
        for a in range(EXPERTS_PER_GROUP):
            for b in range(a + 1, EXPERTS_PER_GROUP):
                lo.append(g * EXPERTS_PER_GROUP + a)
                hi.append(g * EXPERTS_PER_GROUP + b)
    return jnp.asarray(lo, jnp.int32), jnp.asarray(hi, jnp.int32)


def _moe_sorted(h, layer, m_norm, w_group, b_group, w_expert, b_expert, w_gate, w_up, w_down):
    T, D = h.shape
    TM = MOE_TM
    idx, wgt, cnt = _router(h, m_norm, w_group, b_group, w_expert, b_expert)

    counts = cnt[:N_CLASSES, 0]
    padded = ((counts + TM - 1) // TM) * TM
    ends = jnp.cumsum(padded)
    starts = ends - padded
    n_rows = T + N_CLASSES * TM
    n_tiles = n_rows // TM
    tile_start = jnp.arange(n_tiles, dtype=jnp.int32) * TM
    tile_class = jnp.minimum(
        jnp.sum((ends[None, :] <= tile_start[:, None]).astype(jnp.int32), axis=1), N_CLASSES - 1)
    class_lo, class_hi = _class_experts()
    n_used = (ends[-1] // TM).astype(jnp.int32).reshape(1)
    cls, pos = idx[:, 0, :], idx[:, 1, :]
    for c in range(N_CLASSES):
        pos = pos + jnp.where(cls == c, starts[c], 0)
    pos = pos.astype(jnp.int32)

    partial_tail = jnp.where(counts % TM != 0, ends - TM, -1)
    unused = n_used[0] + jnp.arange(N_CLASSES, dtype=jnp.int32)
    unused = jnp.where(unused < n_tiles, unused * TM, -1)
    fill = jnp.concatenate([partial_tail, unused]).astype(jnp.int32)

    xs = _dispatch(h, wgt, pos, fill, n_rows)
    ys = _experts(xs, n_used, class_lo[tile_class], class_hi[tile_class], m_norm, layer, w_gate, w_up, w_down)
    return pos, ys


def _combine_qkv_kernel(pos_ref, pos_next_ref, h_ref, ys_ref, qn_ref, kvn_ref, wq_ref, wkt_ref, wv_ref,
                        hout_ref, q_ref, kt_ref, v_ref, buf_a, buf_b, sem):
    i = pl.program_id(0)
    n = pl.num_programs(0)
    R, D = h_ref.shape
    G = R // SUBLANES

    def row_copy(p_ref, buf, s, g, j):
        src = p_ref[0, 0, g * SUBLANES + j]
        return pltpu.make_async_copy(ys_ref.at[pl.ds(src, 1)], buf.at[g, pl.ds(j, 1)], sem.at[s])

    @pl.when(i == 0)
    def _():
        def body(g, c):
            for j in range(SUBLANES):
                row_copy(pos_ref, buf_a, 0, g, j).start()
            return c
        lax.fori_loop(0, G, body, 0)

    def step(cur, cur_s, nxt, nxt_s):
        def drain(g, c):
            for j in range(SUBLANES):
                row_copy(pos_ref, cur, cur_s, g, j).wait()
            return c
        lax.fori_loop(0, G, drain, 0)

        for g in range(G):
            for j in range(SUBLANES):
                row_copy(pos_next_ref, nxt, nxt_s, g, j).start()
        h = h_ref[...] + cur[...].reshape(R, D)
        hout_ref[...] = h
        xhat = h * _rms_scale(h)
        uq = (xhat * qn_ref[...]).astype(BF16)
        ukv = (xhat * kvn_ref[...]).astype(BF16)
        q_ref[...] = _dot(uq, wq_ref[...]).astype(BF16)
        v_ref[...] = _dot(ukv, wv_ref[...]).astype(BF16)
        kt_ref[0] = _dot_nt(wkt_ref[...], ukv).astype(BF16)

        @pl.when(i == n - 1)
        def _():
            def body(g, c):
                for j in range(SUBLANES):
                    row_copy(pos_next_ref, nxt, nxt_s, g, j).wait()
                return c
            lax.fori_loop(0, G, body, 0)

    @pl.when(i % 2 == 0)
    def _():
        step(buf_a, 0, buf_b, 1)

    @pl.when(i % 2 == 1)
    def _():
        step(buf_b, 1, buf_a, 0)


def _combine_qkv(h, pos, ys, batch, seq, q_norm, kv_norm, w_q, w_kv, scale):
    T, D = h.shape
    sb_dim = w_q.shape[1]
    R = COMBINE_ROWS
    assert seq % R == 0
    n_t = T // R
    n_s = seq // R
    pos3 = pos.reshape(n_t, 1, R)
    w_qs = (w_q * scale).astype(BF16)
    w_kt = w_kv[:, :sb_dim].T.astype(BF16)
    w_v = w_kv[:, sb_dim:].astype(BF16)
    const = lambda i: (0, 0)
    return pl.pallas_call(
        _combine_qkv_kernel,
        grid=(n_t,),
        in_specs=[
            pl.BlockSpec((1, 1, R), lambda i: (i, 0, 0), memory_space=pltpu.SMEM),
            pl.BlockSpec((1, 1, R), lambda i: (jnp.minimum(i + 1, n_t - 1), 0, 0), memory_space=pltpu.SMEM),
            pl.BlockSpec((R, D), lambda i: (i, 0)),
            pl.BlockSpec(memory_space=pl.ANY),
            pl.BlockSpec((1, D), const),
            pl.BlockSpec((1, D), const),
            pl.BlockSpec((D, sb_dim), const),
            pl.BlockSpec((sb_dim, D), const),
            pl.BlockSpec((D, sb_dim), const),
        ],
        out_specs=[
            pl.BlockSpec((R, D), lambda i: (i, 0)),
            pl.BlockSpec((R, sb_dim), lambda i: (i, 0)),
            pl.BlockSpec((1, sb_dim, R), lambda i: (i // n_s, 0, i % n_s)),
            pl.BlockSpec((R, sb_dim), lambda i: (i, 0)),
        ],
        out_shape=[
            jax.ShapeDtypeStruct((T, D), F32),
            jax.ShapeDtypeStruct((T, sb_dim), BF16),
            jax.ShapeDtypeStruct((batch, sb_dim, seq), BF16),
            jax.ShapeDtypeStruct((T, sb_dim), BF16),
        ],
        scratch_shapes=[pltpu.VMEM((R // SUBLANES, SUBLANES, D), F32), pltpu.VMEM((R // SUBLANES, SUBLANES, D), F32),
                        pltpu.SemaphoreType.DMA((2,))],
        compiler_params=pltpu.CompilerParams(dimension_semantics=("arbitrary",), vmem_limit_bytes=VMEM_LIMIT),
        name="combine_qkv",
    )(pos3, pos3, h, ys, q_norm.reshape(1, D), kv_norm.reshape(1, D), w_qs, w_kt, w_v)


def _sb_attention_kernel(q_ref, kt_ref, v_ref, o_ref, *, head_dim):
    S = q_ref.shape[0]
    T = ATT_TILE
    n_q = S // T
    n_heads = LANES // head_dim
    lane_q = lax.broadcasted_iota(jnp.int32, (T, LANES), 1)
    trow = lax.broadcasted_iota(jnp.int32, (T, T), 0)
    scol = lax.broadcasted_iota(jnp.int32, (T, T), 1)
    strictly_before = scol < trow
    suffix = jnp.where(trow >= scol, 1.0, 0.0).astype(BF16)
    in_head = [(lane_q >= hd * head_dim) & (lane_q < (hd + 1) * head_dim) for hd in range(n_heads)]

    ksq = jnp.square(kt_ref[0].astype(F32))
    k_max = [jnp.sqrt(jnp.max(jnp.sum(ksq[hd * head_dim:(hd + 1) * head_dim], axis=0, keepdims=True),
                              axis=1, keepdims=True)) for hd in range(n_heads)]

    def softplus2(z):
        return jnp.maximum(z, jnp.log2(1.0 + jnp.exp2(jnp.minimum(z, ATT_EXP2_CAP))))

    def tile(q_h, k0, acc, run, mask):
        z = _dot(q_h, kt_ref[0, :, pl.ds(k0, T)])
        sp = softplus2(z)
        if mask is not None:
            sp = jnp.where(mask, sp, 0.0)
        within = _dot(sp.astype(BF16), suffix)
        a = jnp.exp2(z - within - run)
        if mask is not None:
            a = jnp.where(mask, a, 0.0)
        acc = acc + _dot(a.astype(BF16), v_ref[pl.ds(k0, T), :])
        return acc, run + within[:, 0:1]

    def all_zero_from_here(runs, z_bound):
        slack = runs[0] - z_bound[0]
        for r, zb in zip(runs[1:], z_bound[1:]):
            slack = jnp.minimum(slack, r - zb)
        return (jnp.min(slack) > ATT_ZERO_MARGIN).astype(jnp.int32)

    def block_tail(qi, with_left, q0, q_hs, z_bound, accs, runs, done):
        def cond(c):
            return (c[0] >= 0) & (c[1] == 0)

        def body(c):
            j, _, accs, runs = c
            k0 = pl.multiple_of(j * T, T)
            st = [tile(q_h, k0, a, r, None) for q_h, a, r in zip(q_hs, accs, runs)]
            accs = [a for a, _ in st]
            runs = [r for _, r in st]
            return j - 1, all_zero_from_here(runs, z_bound), accs, runs

        if with_left:
            _, _, accs, _ = lax.while_loop(cond, body, (qi - 2, done, accs, runs))
        result = accs[0]
        for m, a in zip(in_head[1:], accs[1:]):
            result = jnp.where(m, a, result)
        o_ref[pl.ds(q0, T), :] = result.astype(o_ref.dtype)

    def group_head(q_first, first_has_left):
        n_h = len(in_head)
        U = n_h * T
        G = ATT_GROUP
        static = isinstance(q_first, int)
        mult = (lambda x: x) if static else (lambda x: pl.multiple_of(x, T))
        q0 = [mult((q_first + g) * T) for g in range(G)]
        q_hs, z_bound = [], []
        for g in range(G):
            q_pair = q_ref[pl.ds(q0[g], T), :]
            qsq = jnp.square(q_pair.astype(F32))
            q_hs.append([jnp.where(m, q_pair, jnp.zeros_like(q_pair)) for m in in_head])
            z_bound.append([jnp.sqrt(jnp.sum(jnp.where(m, qsq, 0.0), axis=-1, keepdims=True)) * km
                            for m, km in zip(in_head, k_max)])
        diag_mask = jnp.concatenate([strictly_before] * n_h, axis=0)

        tiles = ([(mult(q_first * T - T), [("left", 0)])] if first_has_left else [])
        for j in range(G):
            tiles.append((q0[j], [("diag", j)] + ([("left", j + 1)] if j + 1 < G else [])))

        z, sp = {}, []
        order = []
        for k0, units in tiles:
            zz = _dot(jnp.concatenate([q_h for _, g in units for q_h in q_hs[g]], axis=0), kt_ref[0, :, pl.ds(k0, T)])
            for i, (kind, g) in enumerate(units):
                z_u = zz[i * U:(i + 1) * U]
                sp_u = softplus2(z_u)
                if kind == "diag":
                    sp_u = jnp.where(diag_mask, sp_u, 0.0)
                z[(kind, g)] = z_u
                sp.append(sp_u)
                order.append((kind, g))
        within = _dot(jnp.concatenate(sp, axis=0).astype(BF16), suffix)
        w = {key: within[i * U:(i + 1) * U] for i, key in enumerate(order)}

        a, run = {}, {}
        for g in range(G):
            run[g] = w[("diag", g)][:, 0:1]
            a[("diag", g)] = jnp.where(diag_mask, jnp.exp2(z[("diag", g)] - w[("diag", g)]), 0.0)
            if ("left", g) in w:
                a[("left", g)] = jnp.exp2(z[("left", g)] - w[("left", g)] - run[g])
                run[g] = run[g] + w[("left", g)][:, 0:1]
        acc = {}
        for k0, units in tiles:
            o = _dot(jnp.concatenate([a[key] for key in units], axis=0).astype(BF16), v_ref[pl.ds(k0, T), :])
            for i, (kind, g) in enumerate(units):
                o_u = o[i * U:(i + 1) * U]
                acc[g] = o_u if g not in acc else acc[g] + o_u
        split = lambda x: [x[hd * T:(hd + 1) * T] for hd in range(n_h)]
        out = []
        for g in range(G):
            runs = split(run[g])
            out.append((q0[g], q_hs[g], z_bound[g], split(acc[g]), runs, all_zero_from_here(runs, z_bound[g])))
        return out

    def block_group(p, first_has_left):
        q_first = ATT_GROUP * p
        heads = group_head(q_first, first_has_left)
        for g, hd in enumerate(heads):
            block_tail(q_first + g, first_has_left or g > 0, *hd)

    block_group(0, False)

    def later_group(p, carry):
        block_group(p, True)
        return carry

    lax.fori_loop(1, n_q // ATT_GROUP, later_group, 0)


def _sb_attention(q, kt, v, batch, seq, head_dim):
    T, sb_dim = q.shape
    assert LANES % head_dim == 0 and seq % (ATT_GROUP * ATT_TILE) == 0
    n_p = sb_dim // LANES
    return pl.pallas_call(
        functools.partial(_sb_attention_kernel, head_dim=head_dim),
        grid=(batch, n_p),
        in_specs=[
            pl.BlockSpec((seq, LANES), lambda b, p: (b, p)),
            pl.BlockSpec((1, LANES, seq), lambda b, p: (b, p, 0)),
            pl.BlockSpec((seq, LANES), lambda b, p: (b, p)),
        ],
        out_specs=pl.BlockSpec((seq, LANES), lambda b, p: (b, p)),
        out_shape=jax.ShapeDtypeStruct((T, sb_dim), BF16),
        compiler_params=pltpu.CompilerParams(
            dimension_semantics=("arbitrary", "arbitrary"), vmem_limit_bytes=VMEM_LIMIT),
        name="sb_attention",
    )(q, kt, v)


def _out_proj_kernel(o_ref, w_ref, h_ref, out_ref):
    out_ref[...] = h_ref[...] + _dot(o_ref[...], w_ref[...])


def _out_proj(o, w_out, h):
    T, D = h.shape
    R = PROJ_ROWS
    return pl.pallas_call(
        _out_proj_kernel,
        grid=(T // R,),
        in_specs=[
            pl.BlockSpec((R, o.shape[1]), lambda i: (i, 0)),
            pl.BlockSpec(w_out.shape, lambda i: (0, 0)),
            pl.BlockSpec((R, D), lambda i: (i, 0)),
        ],
        out_specs=pl.BlockSpec((R, D), lambda i: (i, 0)),
        out_shape=jax.ShapeDtypeStruct((T, D), F32),
        compiler_params=pltpu.CompilerParams(dimension_semantics=("arbitrary",), vmem_limit_bytes=VMEM_LIMIT),
        name="out_proj",
    )(o, w_out.astype(BF16), h)


def kernel(x, a_norm, a_w_in, a_w_gk2, a_b_gk2, a_o_norm, a_w_out, kv_norm, w_kv, b_norm, b_w_q, b_w_out,
           m_norm, m_w_group, m_b_group, m_w_expert, m_b_expert, m_w_gate, m_w_up, m_w_down, final_norm):
    B, S, D = x.shape
    assert a_norm.shape[0] == 1 and b_norm.shape[0] == 1 and m_norm.shape[0] == 2
    head_dim = b_w_q.shape[2] // SB_HEADS
    h = x.reshape(B * S, D)

    def moe_sorted(h, layer):
        return _moe_sorted(h, layer, m_norm[layer], m_w_group[layer], m_b_group[layer], m_w_expert[layer],
                           m_b_expert[layer], m_w_gate, m_w_up, m_w_down)

    h = _gla_layer(h, B, S, a_norm[0], a_w_in[0], a_w_gk2[0], a_b_gk2[0], a_o_norm[0], a_w_out[0])
    pos, ys = moe_sorted(h, 0)
    h, q, kt, v = _combine_qkv(h, pos, ys, B, S, b_norm[0], kv_norm, b_w_q[0], w_kv,
                               math.log2(math.e) / math.sqrt(head_dim))
    o = _sb_attention(q, kt, v, B, S, head_dim)
    h = _out_proj(o, b_w_out[0], h)
    pos, ys = moe_sorted(h, 1)
    h = _combine_norm(h, pos, ys, final_norm)
    return h.reshape(B, S, D)
```

```python
import functools
import math

import jax
import jax.numpy as jnp
from jax import lax
from jax.experimental import pallas as pl
from jax.experimental.pallas import tpu as pltpu

RMS_EPS = 1e-6

GLA_HEADS = 4
GLA_CHUNK = 64
CHUNK_SHIFT = GLA_CHUNK.bit_length() - 1
GATE_NORMALIZER = 16.0
SB_HEADS = 16
N_GROUPS = 4
EXPERTS_PER_GROUP = 4
N_EXPERTS = N_GROUPS * EXPERTS_PER_GROUP
PAIRS_PER_GROUP = EXPERTS_PER_GROUP * (EXPERTS_PER_GROUP - 1) // 2
N_CLASSES = N_GROUPS * PAIRS_PER_GROUP

LANES = 128
SUBLANES = 8
DMA_PRIORITIES = 2
ROUTE_ROWS = 32
EXPERT_ROW0 = 8
VMEM_LIMIT = 56 * 1024 * 1024

GLA_ROWS = 1024
GLA_BLOCK = 256
PROJ_ROWS = 1024
ATT_TILE = 256
ATT_GROUP = 16
ATT_ZERO_MARGIN = 160.0
ATT_EXP2_CAP = 100.0
MOE_ROWS = 1024
COMBINE_ROWS = 512
MOE_TM = 512

BF16 = jnp.bfloat16
F32 = jnp.float32


def _dot(a, b):
    return jnp.dot(a, b, preferred_element_type=F32)


def _dot_nt(a, b):
    return lax.dot_general(a, b, (((1,), (1,)), ((), ())), preferred_element_type=F32)


def _split_dot(x, m01):
    hi = x.astype(BF16)
    lo = (x - hi.astype(F32)).astype(BF16)
    return _dot(hi, m01) + _dot(lo, m01)


def _rms_scale(x):
    return lax.rsqrt(jnp.mean(x * x, axis=-1, keepdims=True) + RMS_EPS)


def _log_sigmoid(x):
    return jnp.minimum(x, 0.0) - jnp.log(1.0 + jnp.exp(-jnp.abs(x)))


def _sigmoid(x):
    return 1.0 / (1.0 + jnp.exp(-x))


def _gla_block(h, states, nrm_ref, wrow_ref, wkt_ref, wgk2_ref, bgk2_ref, onorm_ref, wout_ref,
               *, dk, dv, qk_dim, v_dim, log_scale):
    R = h.shape[0]
    n_chunks = R // GLA_CHUNK
    u = (h * _rms_scale(h) * nrm_ref[...]).astype(BF16)

    proj = _dot(u, wrow_ref[...])
    q = proj[:, :qk_dim]
    v = proj[:, qk_dim:qk_dim + v_dim].astype(BF16)
    g = proj[:, qk_dim + v_dim:qk_dim + 2 * v_dim]
    lr = proj[:, qk_dim + 2 * v_dim:].astype(BF16)
    kt = _dot_nt(wkt_ref[...], u)

    gk = _log_sigmoid(_dot(lr, wgk2_ref[...]) + bgk2_ref[...]) * (1.0 / GATE_NORMALIZER)
    gkt = gk.T

    row = lax.broadcasted_iota(jnp.int32, (R, R), 0)
    col = lax.broadcasted_iota(jnp.int32, (R, R), 1)
    same_chunk = (row >> CHUNK_SHIFT) == (col >> CHUNK_SHIFT)
    causal = same_chunk & (col <= row)
    upto = jnp.where(same_chunk & (row <= col), 1.0, 0.0).astype(BF16)
    after = jnp.where(same_chunk & (row > col), 1.0, 0.0).astype(BF16)

    bt = _split_dot(gkt, upto)
    tail_t = _split_dot(gkt, after)
    b = bt.T

    q_dec = (q * jnp.exp(b + log_scale)).astype(BF16)
    k_inv_t = (kt * jnp.exp(-bt)).astype(BF16)
    k_end_t = (kt * jnp.exp(tail_t)).astype(BF16)
    chunk_decay_t = jnp.exp(bt + tail_t)

    lane_chunk = lax.broadcasted_iota(jnp.int32, (dk, R), 1) >> CHUNK_SHIFT

    acc = h
    new_states = []
    for hd in range(GLA_HEADS):
        ks = slice(hd * dk, (hd + 1) * dk)
        vs = slice(hd * dv, (hd + 1) * dv)
        qd_h = q_dec[:, ks]
        v_h = v[:, vs]
        att = _dot(qd_h, k_inv_t[ks, :])
        att = jnp.where(causal, att, 0.0).astype(BF16)
        kend_h = k_end_t[ks, :]
        kend_c = [jnp.where(lane_chunk == c, kend_h, jnp.zeros_like(kend_h)) for c in range(n_chunks)]
        ov = _dot(jnp.concatenate([att] + kend_c, axis=0), v_h)
        o_h = ov[:R]
        state = states[hd]
        inter = []
        for c in range(n_chunks):
            rows = slice(c * GLA_CHUNK, (c + 1) * GLA_CHUNK)
            inter.append(_dot(qd_h[rows], state.astype(BF16)))
            decay = chunk_decay_t[ks, c * GLA_CHUNK:c * GLA_CHUNK + 1]
            state = decay * state + ov[R + c * dk:R + (c + 1) * dk]
        new_states.append(state)
        o_h = o_h + jnp.concatenate(inter, axis=0)
        o_h = o_h * _rms_scale(o_h) * onorm_ref[...]
        g_h = g[:, vs]
        o_h = o_h * (g_h * _sigmoid(g_h))
        acc = acc + _dot(o_h.astype(BF16), wout_ref[vs, :])
    return acc, new_states


def _gla_kernel(h_ref, *refs, **dims):
    *w_refs, out_ref, state_ref = refs

    @pl.when(pl.program_id(1) == 0)
    def _():
        state_ref[...] = jnp.zeros_like(state_ref)

    states = [state_ref[hd] for hd in range(GLA_HEADS)]
    for blk in range(h_ref.shape[0] // GLA_BLOCK):
        rows = slice(blk * GLA_BLOCK, (blk + 1) * GLA_BLOCK)
        out, states = _gla_block(h_ref[rows, :], states, *w_refs, **dims)
        out_ref[rows, :] = out
    for hd in range(GLA_HEADS):
        state_ref[hd] = states[hd]


def _gla_layer(h, batch, seq, a_norm, w_in, w_gk2, b_gk2, o_norm, w_out):
    T, D = h.shape
    rank, qk_dim = w_gk2.shape
    v_dim = w_out.shape[0]
    dk = qk_dim // GLA_HEADS
    dv = v_dim // GLA_HEADS
    R = GLA_ROWS
    assert seq % R == 0 and R % GLA_CHUNK == 0 and rank <= LANES
    assert w_in.shape[1] == 2 * qk_dim + 2 * v_dim + rank
    n_s = seq // R

    o_q, o_k, o_v, o_lr, o_g = 0, qk_dim, 2 * qk_dim, 2 * qk_dim + v_dim, 2 * qk_dim + v_dim + rank
    w_lr = jnp.pad(w_in[:, o_lr:o_lr + rank], ((0, 0), (0, LANES - rank)))
    w_row = jnp.concatenate([w_in[:, o_q:o_k], w_in[:, o_v:o_lr], w_in[:, o_g:], w_lr], axis=1).astype(BF16)
    w_kt = w_in[:, o_k:o_v].T.astype(BF16)
    w_gk2p = jnp.pad(w_gk2, ((0, LANES - rank), (0, 0))).astype(BF16)
    const = lambda b, s: (0, 0)
    kern = functools.partial(_gla_kernel, dk=dk, dv=dv, qk_dim=qk_dim, v_dim=v_dim,
                             log_scale=math.log(dk ** -0.5))
    return pl.pallas_call(
        kern,
        grid=(batch, n_s),
        in_specs=[
            pl.BlockSpec((R, D), lambda b, s: (b * n_s + s, 0)),
            pl.BlockSpec((1, D), const),
            pl.BlockSpec(w_row.shape, const),
            pl.BlockSpec(w_kt.shape, const),
            pl.BlockSpec(w_gk2p.shape, const),
            pl.BlockSpec((1, qk_dim), const),
            pl.BlockSpec((1, dv), const),
            pl.BlockSpec((v_dim, D), const),
        ],
        out_specs=pl.BlockSpec((R, D), lambda b, s: (b * n_s + s, 0)),
        out_shape=jax.ShapeDtypeStruct((T, D), F32),
        scratch_shapes=[pltpu.VMEM((GLA_HEADS, dk, dv), F32)],
        compiler_params=pltpu.CompilerParams(
            dimension_semantics=("arbitrary", "arbitrary"), vmem_limit_bytes=VMEM_LIMIT),
        name="gla_layer",
    )(h, a_norm.reshape(1, D), w_row, w_kt, w_gk2p, b_gk2.reshape(1, qk_dim), o_norm.reshape(1, dv),
      w_out.astype(BF16))


def _router_kernel(h_ref, nrm_ref, wt_ref, bt_ref, upper_ref, idx_ref, wgt_ref, cnt_ref, carry_ref):
    R = h_ref.shape[0]

    @pl.when(pl.program_id(0) == 0)
    def _():
        carry_ref[...] = jnp.zeros_like(carry_ref)

    h = h_ref[...]
    t = h * _rms_scale(h) * nrm_ref[...]
    t_hi = t.astype(BF16)
    t_lo = (t - t_hi.astype(F32)).astype(BF16)
    p = _dot_nt(wt_ref[...], t_hi)
    logits = p[:ROUTE_ROWS] + p[ROUTE_ROWS:] + _dot_nt(wt_ref[:ROUTE_ROWS], t_lo) + bt_ref[:, 0:1]
    row = lax.broadcasted_iota(jnp.int32, (ROUTE_ROWS, R), 0)
    neg = jnp.float32(-jnp.inf)

    def first_argmax(vals):
        m = jnp.max(vals, axis=0, keepdims=True)
        i = jnp.min(jnp.where(vals == m, row, ROUTE_ROWS), axis=0, keepdims=True)
        return m, i

    gl = jnp.where(row < N_GROUPS, logits, neg)
    g_max, g_idx = first_argmax(gl)
    g_w = 1.0 / jnp.sum(jnp.exp(gl - g_max), axis=0, keepdims=True)

    base = EXPERT_ROW0 + g_idx * EXPERTS_PER_GROUP
    el = jnp.where((row >= base) & (row < base + EXPERTS_PER_GROUP), logits, neg)
    m1, i1 = first_argmax(el)
    m2, i2 = first_argmax(jnp.where(row == i1, neg, el))
    r = jnp.exp(m2 - m1)
    w1 = g_w / (1.0 + r)
    w2 = g_w * r / (1.0 + r)
    l1 = i1 - base
    l2 = i2 - base
    first_is_lo = l1 < l2
    lo = jnp.minimum(l1, l2)
    hi = jnp.maximum(l1, l2)
    cls = g_idx * PAIRS_PER_GROUP + ((lo * (2 * EXPERTS_PER_GROUP - 1 - lo)) >> 1) + (hi - lo - 1)
    w_lo = jnp.where(first_is_lo, w1, w2)
    w_hi = jnp.where(first_is_lo, w2, w1)

    onehot = jnp.where(row == cls, 1.0, 0.0)
    before = _dot(onehot.astype(BF16), upper_ref[...]) + carry_ref[:, 0:1]
    rank = jnp.sum(jnp.where(row == cls, before, 0.0), axis=0, keepdims=True)
    total = carry_ref[:, 0:1] + jnp.sum(onehot, axis=1, keepdims=True)
    carry_ref[...] = jnp.broadcast_to(total, carry_ref.shape)
    cnt_ref[...] = jnp.broadcast_to(total, cnt_ref.shape).astype(jnp.int32)

    row8 = lax.broadcasted_iota(jnp.int32, (SUBLANES, R), 0)
    idx_ref[0] = jnp.where(row8 == 0, cls, jnp.where(row8 == 1, rank.astype(jnp.int32), 0))
    w_rows = jnp.where(row == 0, w_lo, jnp.where(row == 1, w_hi, 0.0))
    w_rows = jnp.concatenate([w_rows, jnp.zeros((LANES - ROUTE_ROWS, R), F32)], axis=0)
    wgt_ref[...] = w_rows.T


def _router(h, m_norm, w_group, b_group, w_expert, b_expert):
    T, D = h.shape
    R = MOE_ROWS
    assert T % R == 0 and N_CLASSES <= ROUTE_ROWS
    pad_g = EXPERT_ROW0 - N_GROUPS
    pad_e = ROUTE_ROWS - EXPERT_ROW0 - N_EXPERTS
    w_t = jnp.pad(jnp.concatenate([w_group.T, jnp.zeros((pad_g, D), F32), w_expert.T], axis=0), ((0, pad_e), (0, 0)))
    w_t_hi = w_t.astype(BF16)
    w_t = jnp.concatenate([w_t_hi, (w_t - w_t_hi.astype(F32)).astype(BF16)], axis=0)
    b_t = jnp.pad(jnp.concatenate([b_group, jnp.zeros((pad_g,), F32), b_expert]), (0, pad_e))
    b_t = jnp.broadcast_to(b_t[:, None], (ROUTE_ROWS, LANES))
    idx = jnp.arange(R)
    upper = (idx[:, None] < idx[None, :]).astype(BF16)
    const = lambda i: (0, 0)
    return pl.pallas_call(
        _router_kernel,
        grid=(T // R,),
        in_specs=[
            pl.BlockSpec((R, D), lambda i: (i, 0)),
            pl.BlockSpec((1, D), const),
            pl.BlockSpec((2 * ROUTE_ROWS, D), const),
            pl.BlockSpec((ROUTE_ROWS, LANES), const),
            pl.BlockSpec((R, R), const),
        ],
        out_specs=[
            pl.BlockSpec((1, SUBLANES, R), lambda i: (i, 0, 0)),
            pl.BlockSpec((R, LANES), lambda i: (i, 0)),
            pl.BlockSpec((ROUTE_ROWS, LANES), const),
        ],
        out_shape=[
            jax.ShapeDtypeStruct((T // R, SUBLANES, R), jnp.int32),
            jax.ShapeDtypeStruct((T, LANES), F32),
            jax.ShapeDtypeStruct((ROUTE_ROWS, LANES), jnp.int32),
        ],
        scratch_shapes=[pltpu.VMEM((ROUTE_ROWS, LANES), F32)],
        compiler_params=pltpu.CompilerParams(dimension_semantics=("arbitrary",), vmem_limit_bytes=VMEM_LIMIT),
        name="moe_router",
    )(h, m_norm.reshape(1, D), w_t, b_t, upper)


def _dispatch_kernel(pos_ref, pos_prev_ref, fill_ref, h_ref, wgt_ref, xs_ref, row_ref, zero_ref, sem, fill_sem):
    i = pl.program_id(0)
    n = pl.num_programs(0)
    R, D = h_ref.shape
    G = R // SUBLANES
    slot = i % 2

    @pl.when(i == 0)
    def _():
        zero_ref[...] = jnp.zeros_like(zero_ref)
        tile_rows = zero_ref.shape[0]

        def fill_copy(k):
            first = pl.multiple_of(jnp.maximum(fill_ref[0, 0, k], 0), tile_rows)
            return pltpu.make_async_copy(zero_ref, xs_ref.at[pl.ds(first, tile_rows)], fill_sem)

        def start(k, c):
            @pl.when(fill_ref[0, 0, k] >= 0)
            def _():
                fill_copy(k).start()
            return c

        def wait(k, c):
            @pl.when(fill_ref[0, 0, k] >= 0)
            def _():
                fill_copy(k).wait()
            return c

        lax.fori_loop(0, fill_ref.shape[2], start, 0)
        lax.fori_loop(0, fill_ref.shape[2], wait, 0)

    row_ref[slot, :, :, :D] = h_ref[...].reshape(G, SUBLANES, D)
    row_ref[slot, :, :, D:] = wgt_ref[...].reshape(G, SUBLANES, LANES)

    def row_copy(p_ref, s, g, j):
        dst = p_ref[0, 0, g * SUBLANES + j]
        return pltpu.make_async_copy(row_ref.at[s, g, pl.ds(j, 1)], xs_ref.at[pl.ds(dst, 1)], sem.at[s])

    for g in range(G):
        for j in range(SUBLANES):
            row_copy(pos_ref, slot, g, j).start(priority=j % DMA_PRIORITIES)

    def drain(p_ref, s):
        def body(g, c):
            for j in range(SUBLANES):
                row_copy(p_ref, s, g, j).wait()
            return c
        lax.fori_loop(0, G, body, 0)

    @pl.when(i > 0)
    def _():
        drain(pos_prev_ref, 1 - slot)

    @pl.when(i == n - 1)
    def _():
        drain(pos_ref, slot)


def _dispatch(h, wgt, pos, fill, n_rows):
    T, D = h.shape
    R = MOE_ROWS
    n_t = T // R
    pos3 = pos.reshape(n_t, 1, R)
    fill3 = fill.reshape(1, 1, -1)
    return pl.pallas_call(
        _dispatch_kernel,
        grid=(n_t,),
        in_specs=[
            pl.BlockSpec((1, 1, R), lambda i: (i, 0, 0), memory_space=pltpu.SMEM),
            pl.BlockSpec((1, 1, R), lambda i: (jnp.maximum(i - 1, 0), 0, 0), memory_space=pltpu.SMEM),
            pl.BlockSpec(fill3.shape, lambda i: (0, 0, 0), memory_space=pltpu.SMEM),
            pl.BlockSpec((R, D), lambda i: (i, 0)),
            pl.BlockSpec((R, LANES), lambda i: (i, 0)),
        ],
        out_specs=pl.BlockSpec(memory_space=pl.ANY),
        out_shape=jax.ShapeDtypeStruct((n_rows, D + LANES), F32),
        scratch_shapes=[pltpu.VMEM((2, R // SUBLANES, SUBLANES, D + LANES), F32),
                        pltpu.VMEM((MOE_TM, D + LANES), F32),
                        pltpu.SemaphoreType.DMA((2,)), pltpu.SemaphoreType.DMA],
        compiler_params=pltpu.CompilerParams(dimension_semantics=("arbitrary",), vmem_limit_bytes=VMEM_LIMIT),
        name="moe_dispatch",
    )(pos3, pos3, fill3, h, wgt)


def _experts_kernel(used_ref, lo_ref, hi_ref, xs_ref, nrm_ref, g_lo_ref, u_lo_ref, d_lo_ref, g_hi_ref, u_hi_ref,
                    d_hi_ref, ys_ref, wgu_ref, wd_ref, *, d_model, d_expert):
    i = pl.program_id(0)
    active = i < used_ref[0]
    prev = jnp.maximum(i - 1, 0)

    def refresh(k, e_ref, g_ref, u_ref, d_ref):
        @pl.when(active & ((i == 0) | (e_ref[i] != e_ref[prev])))
        def _():
            wgu_ref[k, :, :d_expert] = g_ref[0, 0].astype(BF16)
            wgu_ref[k, :, d_expert:] = u_ref[0, 0].astype(BF16)
            wd_ref[k] = d_ref[0, 0].astype(BF16)

    refresh(0, lo_ref, g_lo_ref, u_lo_ref, d_lo_ref)
    refresh(1, hi_ref, g_hi_ref, u_hi_ref, d_hi_ref)

    @pl.when(active)
    def _():
        x = xs_ref[:, :d_model]
        t = (x * _rms_scale(x) * nrm_ref[...]).astype(BF16)
        y = None
        for k in range(2):
            gu = _dot(t, wgu_ref[k])
            gate = gu[:, :d_expert]
            hdn = (gate * _sigmoid(gate) * gu[:, d_expert:]).astype(BF16)
            y_k = xs_ref[:, d_model + k:d_model + k + 1] * _dot(hdn, wd_ref[k])
            y = y_k if y is None else y + y_k
        ys_ref[...] = y

    @pl.when(jnp.logical_not(active))
    def _():
        ys_ref[...] = jnp.zeros_like(ys_ref)


def _experts(xs, n_used, tile_lo, tile_hi, m_norm, layer, w_gate, w_up, w_down):
    n_rows = xs.shape[0]
    D = m_norm.shape[0]
    d_expert = w_gate.shape[-1]
    TM = MOE_TM
    n_tiles = n_rows // TM
    in_w = lambda which: (lambda i, nu, lo, hi: (layer, which(lo, hi)[i], 0, 0))
    pick_lo = lambda lo, hi: lo
    pick_hi = lambda lo, hi: hi
    grid_spec = pltpu.PrefetchScalarGridSpec(
        num_scalar_prefetch=3,
        grid=(n_tiles,),
        in_specs=[
            pl.BlockSpec((TM, D + LANES), lambda i, nu, lo, hi: (i, 0)),
            pl.BlockSpec((1, D), lambda i, nu, lo, hi: (0, 0)),
            pl.BlockSpec((1, 1, D, d_expert), in_w(pick_lo)),
            pl.BlockSpec((1, 1, D, d_expert), in_w(pick_lo)),
            pl.BlockSpec((1, 1, d_expert, D), in_w(pick_lo)),
            pl.BlockSpec((1, 1, D, d_expert), in_w(pick_hi)),
            pl.BlockSpec((1, 1, D, d_expert), in_w(pick_hi)),
            pl.BlockSpec((1, 1, d_expert, D), in_w(pick_hi)),
        ],
        out_specs=pl.BlockSpec((TM, D), lambda i, nu, lo, hi: (i, 0)),
        scratch_shapes=[pltpu.VMEM((2, D, 2 * d_expert), BF16), pltpu.VMEM((2, d_expert, D), BF16)],
    )
    return pl.pallas_call(
        functools.partial(_experts_kernel, d_model=D, d_expert=d_expert),
        grid_spec=grid_spec,
        out_shape=jax.ShapeDtypeStruct((n_rows, D), F32),
        compiler_params=pltpu.CompilerParams(dimension_semantics=("arbitrary",), vmem_limit_bytes=VMEM_LIMIT),
        name="moe_experts",
    )(n_used, tile_lo, tile_hi, xs, m_norm.reshape(1, D), w_gate, w_up, w_down, w_gate, w_up, w_down)


def _combine_norm_kernel(pos_ref, pos_next_ref, h_ref, nrm_ref, ys_ref, out_ref, buf_ref, sem):
    i = pl.program_id(0)
    n = pl.num_programs(0)
    R, D = h_ref.shape
    G = R // SUBLANES
    slot = i % 2

    def row_copy(p_ref, s, g, j):
        src = p_ref[0, 0, g * SUBLANES + j]
        return pltpu.make_async_copy(ys_ref.at[pl.ds(src, 1)], buf_ref.at[s, g, pl.ds(j, 1)], sem.at[s])

    def issue(p_ref, s):
        def body(g, c):
            for j in range(SUBLANES):
                row_copy(p_ref, s, g, j).start(priority=j % DMA_PRIORITIES)
            return c
        lax.fori_loop(0, G, body, 0)

    @pl.when(i == 0)
    def _():
        issue(pos_ref, slot)

    @pl.when(i < n - 1)
    def _():
        for g in range(G):
            for j in range(SUBLANES):
                row_copy(pos_next_ref, 1 - slot, g, j).start(priority=j % DMA_PRIORITIES)

    def drain(g, c):
        for j in range(SUBLANES):
            row_copy(pos_ref, slot, g, j).wait()
        return c

    lax.fori_loop(0, G, drain, 0)

    out = h_ref[...] + buf_ref[slot].reshape(R, D)
    out_ref[...] = out * _rms_scale(out) * nrm_ref[...]


def _combine_norm(h, pos, ys, norm):
    T, D = h.shape
    R = COMBINE_ROWS
    n_t = T // R
    pos3 = pos.reshape(n_t, 1, R)
    return pl.pallas_call(
        _combine_norm_kernel,
        grid=(n_t,),
        in_specs=[
            pl.BlockSpec((1, 1, R), lambda i: (i, 0, 0), memory_space=pltpu.SMEM),
            pl.BlockSpec((1, 1, R), lambda i: (jnp.minimum(i + 1, n_t - 1), 0, 0), memory_space=pltpu.SMEM),
            pl.BlockSpec((R, D), lambda i: (i, 0)),
            pl.BlockSpec((1, D), lambda i: (0, 0)),
            pl.BlockSpec(memory_space=pl.ANY),
        ],
        out_specs=pl.BlockSpec((R, D), lambda i: (i, 0)),
        out_shape=jax.ShapeDtypeStruct((T, D), F32),
        scratch_shapes=[pltpu.VMEM((2, R // SUBLANES, SUBLANES, D), F32), pltpu.SemaphoreType.DMA((2,))],
        compiler_params=pltpu.CompilerParams(dimension_semantics=("arbitrary",), vmem_limit_bytes=VMEM_LIMIT),
        name="moe_combine",
    )(pos3, pos3, h, norm.reshape(1, D), ys)


def _class_experts():
    lo, hi = [], []
    for g in range(N_GROUPS):
        for a in range(EXPERTS_PER_GROUP):
            for b in range(a + 1, EXPERTS_PER_GROUP):
                lo.append(g * EXPERTS_PER_GROUP + a)
                hi.append(g * EXPERTS_PER_GROUP + b)
    return jnp.asarray(lo, jnp.int32), jnp.asarray(hi, jnp.int32)


def _moe_sorted(h, layer, m_norm, w_group, b_group, w_expert, b_expert, w_gate, w_up, w_down):
    T, D = h.shape
    TM = MOE_TM
    idx, wgt, cnt = _router(h, m_norm, w_group, b_group, w_expert, b_expert)

    counts = cnt[:N_CLASSES, 0]
    padded = ((counts + TM - 1) // TM) * TM
    ends = jnp.cumsum(padded)
    starts = ends - padded
    n_rows = T + N_CLASSES * TM
    n_tiles = n_rows // TM
    tile_start = jnp.arange(n_tiles, dtype=jnp.int32) * TM
    tile_class = jnp.minimum(
        jnp.sum((ends[None, :] <= tile_start[:, None]).astype(jnp.int32), axis=1), N_CLASSES - 1)
    class_lo, class_hi = _class_experts()
    n_used = (ends[-1] // TM).astype(jnp.int32).reshape(1)
    cls, pos = idx[:, 0, :], idx[:, 1, :]
    for c in range(N_CLASSES):
        pos = pos + jnp.where(cls == c, starts[c], 0)
    pos = pos.astype(jnp.int32)

    partial_tail = jnp.where(counts % TM != 0, ends - TM, -1)
    unused = n_used[0] + jnp.arange(N_CLASSES, dtype=jnp.int32)
    unused = jnp.where(unused < n_tiles, unused * TM, -1)
    fill = jnp.concatenate([partial_tail, unused]).astype(jnp.int32)

    xs = _dispatch(h, wgt, pos, fill, n_rows)
    ys = _experts(xs, n_used, class_lo[tile_class], class_hi[tile_class], m_norm, layer, w_gate, w_up, w_down)
    return pos, ys


def _combine_qkv_kernel(pos_ref, pos_next_ref, h_ref, ys_ref, qn_ref, kvn_ref, wq_ref, wkt_ref, wv_ref,
                        hout_ref, q_ref, kt_ref, v_ref, buf_a, buf_b, sem):
    i = pl.program_id(0)
    n = pl.num_programs(0)
    R, D = h_ref.shape
    G = R // SUBLANES

    def row_copy(p_ref, buf, s, g, j):
        src = p_ref[0, 0, g * SUBLANES + j]
        return pltpu.make_async_copy(ys_ref.at[pl.ds(src, 1)], buf.at[g, pl.ds(j, 1)], sem.at[s])

    @pl.when(i == 0)
    def _():
        def body(g, c):
            for j in range(SUBLANES):
                row_copy(pos_ref, buf_a, 0, g, j).start(priority=j % DMA_PRIORITIES)
            return c
        lax.fori_loop(0, G, body, 0)

    def step(cur, cur_s, nxt, nxt_s):
        def drain(g, c):
            for j in range(SUBLANES):
                row_copy(pos_ref, cur, cur_s, g, j).wait()
            return c
        lax.fori_loop(0, G, drain, 0)

        for g in range(G):
            for j in range(SUBLANES):
                row_copy(pos_next_ref, nxt, nxt_s, g, j).start(priority=j % DMA_PRIORITIES)
        h = h_ref[...] + cur[...].reshape(R, D)
        hout_ref[...] = h
        xhat = h * _rms_scale(h)
        uq = (xhat * qn_ref[...]).astype(BF16)
        ukv = (xhat * kvn_ref[...]).astype(BF16)
        q_ref[...] = _dot(uq, wq_ref[...]).astype(BF16)
        v_ref[...] = _dot(ukv, wv_ref[...]).astype(BF16)
        kt_ref[0] = _dot_nt(wkt_ref[...], ukv).astype(BF16)

        @pl.when(i == n - 1)
        def _():
            def body(g, c):
                for j in range(SUBLANES):
                    row_copy(pos_next_ref, nxt, nxt_s, g, j).wait()
                return c
            lax.fori_loop(0, G, body, 0)

    @pl.when(i % 2 == 0)
    def _():
        step(buf_a, 0, buf_b, 1)

    @pl.when(i % 2 == 1)
    def _():
        step(buf_b, 1, buf_a, 0)


def _combine_qkv(h, pos, ys, batch, seq, q_norm, kv_norm, w_q, w_kv, scale):
    T, D = h.shape
    sb_dim = w_q.shape[1]
    R = COMBINE_ROWS
    assert seq % R == 0
    n_t = T // R
    n_s = seq // R
    pos3 = pos.reshape(n_t, 1, R)
    w_qs = (w_q * scale).astype(BF16)
    w_kt = w_kv[:, :sb_dim].T.astype(BF16)
    w_v = w_kv[:, sb_dim:].astype(BF16)
    const = lambda i: (0, 0)
    return pl.pallas_call(
        _combine_qkv_kernel,
        grid=(n_t,),
        in_specs=[
            pl.BlockSpec((1, 1, R), lambda i: (i, 0, 0), memory_space=pltpu.SMEM),
            pl.BlockSpec((1, 1, R), lambda i: (jnp.minimum(i + 1, n_t - 1), 0, 0), memory_space=pltpu.SMEM),
            pl.BlockSpec((R, D), lambda i: (i, 0)),
            pl.BlockSpec(memory_space=pl.ANY),
            pl.BlockSpec((1, D), const),
            pl.BlockSpec((1, D), const),
            pl.BlockSpec((D, sb_dim), const),
            pl.BlockSpec((sb_dim, D), const),
            pl.BlockSpec((D, sb_dim), const),
        ],
        out_specs=[
            pl.BlockSpec((R, D), lambda i: (i, 0)),
            pl.BlockSpec((R, sb_dim), lambda i: (i, 0)),
            pl.BlockSpec((1, sb_dim, R), lambda i: (i // n_s, 0, i % n_s)),
            pl.BlockSpec((R, sb_dim), lambda i: (i, 0)),
        ],
        out_shape=[
            jax.ShapeDtypeStruct((T, D), F32),
            jax.ShapeDtypeStruct((T, sb_dim), BF16),
            jax.ShapeDtypeStruct((batch, sb_dim, seq), BF16),
            jax.ShapeDtypeStruct((T, sb_dim), BF16),
        ],
        scratch_shapes=[pltpu.VMEM((R // SUBLANES, SUBLANES, D), F32), pltpu.VMEM((R // SUBLANES, SUBLANES, D), F32),
                        pltpu.SemaphoreType.DMA((2,))],
        compiler_params=pltpu.CompilerParams(dimension_semantics=("arbitrary",), vmem_limit_bytes=VMEM_LIMIT),
        name="combine_qkv",
    )(pos3, pos3, h, ys, q_norm.reshape(1, D), kv_norm.reshape(1, D), w_qs, w_kt, w_v)


def _sb_attention_kernel(q_ref, kt_ref, v_ref, o_ref, *, head_dim):
    S = q_ref.shape[0]
    T = ATT_TILE
    n_q = S // T
    n_heads = LANES // head_dim
    lane_q = lax.broadcasted_iota(jnp.int32, (T, LANES), 1)
    trow = lax.broadcasted_iota(jnp.int32, (T, T), 0)
    scol = lax.broadcasted_iota(jnp.int32, (T, T), 1)
    strictly_before = scol < trow
    suffix = jnp.where(trow >= scol, 1.0, 0.0).astype(BF16)
    in_head = [(lane_q >= hd * head_dim) & (lane_q < (hd + 1) * head_dim) for hd in range(n_heads)]

    ksq = jnp.square(kt_ref[0].astype(F32))
    k_max = [jnp.sqrt(jnp.max(jnp.sum(ksq[hd * head_dim:(hd + 1) * head_dim], axis=0, keepdims=True),
                              axis=1, keepdims=True)) for hd in range(n_heads)]

    def softplus2(z):
        return jnp.maximum(z, jnp.log2(1.0 + jnp.exp2(jnp.minimum(z, ATT_EXP2_CAP))))

    def tile(q_h, k0, acc, run, mask):
        z = _dot(q_h, kt_ref[0, :, pl.ds(k0, T)])
        sp = softplus2(z)
        if mask is not None:
            sp = jnp.where(mask, sp, 0.0)
        within = _dot(sp.astype(BF16), suffix)
        a = jnp.exp2(z - within - run)
        if mask is not None:
            a = jnp.where(mask, a, 0.0)
        acc = acc + _dot(a.astype(BF16), v_ref[pl.ds(k0, T), :])
        return acc, run + within[:, 0:1]

    def all_zero_from_here(runs, z_bound):
        slack = runs[0] - z_bound[0]
        for r, zb in zip(runs[1:], z_bound[1:]):
            slack = jnp.minimum(slack, r - zb)
        return (jnp.min(slack) > ATT_ZERO_MARGIN).astype(jnp.int32)

    def block_tail(qi, with_left, q0, q_hs, z_bound, accs, runs, done):
        def cond(c):
            return (c[0] >= 0) & (c[1] == 0)

        def body(c):
            j, _, accs, runs = c
            k0 = pl.multiple_of(j * T, T)
            st = [tile(q_h, k0, a, r, None) for q_h, a, r in zip(q_hs, accs, runs)]
            accs = [a for a, _ in st]
            runs = [r for _, r in st]
            return j - 1, all_zero_from_here(runs, z_bound), accs, runs

        if with_left:
            _, _, accs, _ = lax.while_loop(cond, body, (qi - 2, done, accs, runs))
        result = accs[0]
        for m, a in zip(in_head[1:], accs[1:]):
            result = jnp.where(m, a, result)
        o_ref[pl.ds(q0, T), :] = result.astype(o_ref.dtype)

    def group_head(q_first, first_has_left):
        n_h = len(in_head)
        U = n_h * T
        G = ATT_GROUP
        static = isinstance(q_first, int)
        mult = (lambda x: x) if static else (lambda x: pl.multiple_of(x, T))
        q0 = [mult((q_first + g) * T) for g in range(G)]
        q_hs, z_bound = [], []
        for g in range(G):
            q_pair = q_ref[pl.ds(q0[g], T), :]
            qsq = jnp.square(q_pair.astype(F32))
            q_hs.append([jnp.where(m, q_pair, jnp.zeros_like(q_pair)) for m in in_head])
            z_bound.append([jnp.sqrt(jnp.sum(jnp.where(m, qsq, 0.0), axis=-1, keepdims=True)) * km
                            for m, km in zip(in_head, k_max)])
        diag_mask = jnp.concatenate([strictly_before] * n_h, axis=0)

        tiles = ([(mult(q_first * T - T), [("left", 0)])] if first_has_left else [])
        for j in range(G):
            tiles.append((q0[j], [("diag", j)] + ([("left", j + 1)] if j + 1 < G else [])))

        z, sp = {}, []
        order = []
        for k0, units in tiles:
            zz = _dot(jnp.concatenate([q_h for _, g in units for q_h in q_hs[g]], axis=0), kt_ref[0, :, pl.ds(k0, T)])
            for i, (kind, g) in enumerate(units):
                z_u = zz[i * U:(i + 1) * U]
                sp_u = softplus2(z_u)
                if kind == "diag":
                    sp_u = jnp.where(diag_mask, sp_u, 0.0)
                z[(kind, g)] = z_u
                sp.append(sp_u)
                order.append((kind, g))
        within = _dot(jnp.concatenate(sp, axis=0).astype(BF16), suffix)
        w = {key: within[i * U:(i + 1) * U] for i, key in enumerate(order)}

        a, run = {}, {}
        for g in range(G):
            run[g] = w[("diag", g)][:, 0:1]
            a[("diag", g)] = jnp.where(diag_mask, jnp.exp2(z[("diag", g)] - w[("diag", g)]), 0.0)
            if ("left", g) in w:
                a[("left", g)] = jnp.exp2(z[("left", g)] - w[("left", g)] - run[g])
                run[g] = run[g] + w[("left", g)][:, 0:1]
        acc = {}
        for k0, units in tiles:
            o = _dot(jnp.concatenate([a[key] for key in units], axis=0).astype(BF16), v_ref[pl.ds(k0, T), :])
            for i, (kind, g) in enumerate(units):
                o_u = o[i * U:(i + 1) * U]
                acc[g] = o_u if g not in acc else acc[g] + o_u
        split = lambda x: [x[hd * T:(hd + 1) * T] for hd in range(n_h)]
        out = []
        for g in range(G):
            runs = split(run[g])
            out.append((q0[g], q_hs[g], z_bound[g], split(acc[g]), runs, all_zero_from_here(runs, z_bound[g])))
        return out

    def block_group(p, first_has_left):
        q_first = ATT_GROUP * p
        heads = group_head(q_first, first_has_left)
        for g, hd in enumerate(heads):
            block_tail(q_first + g, first_has_left or g > 0, *hd)

    block_group(0, False)

    def later_group(p, carry):
        block_group(p, True)
        return carry

    lax.fori_loop(1, n_q // ATT_GROUP, later_group, 0)


def _sb_attention(q, kt, v, batch, seq, head_dim):
    T, sb_dim = q.shape
    assert LANES % head_dim == 0 and seq % (ATT_GROUP * ATT_TILE) == 0
    n_p = sb_dim // LANES
    return pl.pallas_call(
        functools.partial(_sb_attention_kernel, head_dim=head_dim),
        grid=(batch, n_p),
        in_specs=[
            pl.BlockSpec((seq, LANES), lambda b, p: (b, p)),
            pl.BlockSpec((1, LANES, seq), lambda b, p: (b, p, 0)),
            pl.BlockSpec((seq, LANES), lambda b, p: (b, p)),
        ],
        out_specs=pl.BlockSpec((seq, LANES), lambda b, p: (b, p)),
        out_shape=jax.ShapeDtypeStruct((T, sb_dim), BF16),
        compiler_params=pltpu.CompilerParams(
            dimension_semantics=("arbitrary", "arbitrary"), vmem_limit_bytes=VMEM_LIMIT),
        name="sb_attention",
    )(q, kt, v)


def _out_proj_kernel(o_ref, w_ref, h_ref, out_ref):
    out_ref[...] = h_ref[...] + _dot(o_ref[...], w_ref[...])


def _out_proj(o, w_out, h):
    T, D = h.shape
    R = PROJ_ROWS
    return pl.pallas_call(
        _out_proj_kernel,
        grid=(T // R,),
        in_specs=[
            pl.BlockSpec((R, o.shape[1]), lambda i: (i, 0)),
            pl.BlockSpec(w_out.shape, lambda i: (0, 0)),
            pl.BlockSpec((R, D), lambda i: (i, 0)),
        ],
        out_specs=pl.BlockSpec((R, D), lambda i: (i, 0)),
        out_shape=jax.ShapeDtypeStruct((T, D), F32),
        compiler_params=pltpu.CompilerParams(dimension_semantics=("arbitrary",), vmem_limit_bytes=VMEM_LIMIT),
        name="out_proj",
    )(o, w_out.astype(BF16), h)


def kernel(x, a_norm, a_w_in, a_w_gk2, a_b_gk2, a_o_norm, a_w_out, kv_norm, w_kv, b_norm, b_w_q, b_w_out,
           m_norm, m_w_group, m_b_group, m_w_expert, m_b_expert, m_w_gate, m_w_up, m_w_down, final_norm):
    B, S, D = x.shape
    assert a_norm.shape[0] == 1 and b_norm.shape[0] == 1 and m_norm.shape[0] == 2
    head_dim = b_w_q.shape[2] // SB_HEADS
    h = x.reshape(B * S, D)

    def moe_sorted(h, layer):
        return _moe_sorted(h, layer, m_norm[layer], m_w_group[layer], m_b_group[layer], m_w_expert[layer],
                           m_b_expert[layer], m_w_gate, m_w_up, m_w_down)

    h = _gla_layer(h, B, S, a_norm[0], a_w_in[0], a_w_gk2[0], a_b_gk2[0], a_o_norm[0], a_w_out[0])
    pos, ys = moe_sorted(h, 0)
    h, q, kt, v = _combine_qkv(h, pos, ys, B, S, b_norm[0], kv_norm, b_w_q[0], w_kv,
                               math.log2(math.e) / math.sqrt(head_dim))
    o = _sb_attention(q, kt, v, B, S, head_dim)
    h = _out_proj(o, b_w_out[0], h)
    pos, ys = moe_sorted(h, 1)
    h = _combine_norm(h, pos, ys, final_norm)
    return h.reshape(B, S, D)
```
